```python
import math
import jax, jax.numpy as jnp
from jax import lax
import numpy as np

D_MODEL = 2048
BATCH = 16
SEQ = 2048
DEPTH = 1

N_Q_HEADS = 16
N_KV_GROUPS = 4
HEADS_PER_GROUP = N_Q_HEADS // N_KV_GROUPS
HEAD_DIM = 64
ATTN_WIDTH = N_Q_HEADS * HEAD_DIM
KV_WIDTH = N_KV_GROUPS * HEAD_DIM
SCALE = HEAD_DIM ** -0.5
CMP_BLOCK = 32
CMP_STRIDE = 16
CMP_HIDDEN = 256
SEL_BLOCK = 64
N_SELECT = 16
WINDOW = 512
Q_BLOCK = 128
SEL_Q_CHUNK = 32
N_BRANCH = 3
RNN_WIDTH = D_MODEL - ATTN_WIDTH
RNN_BLOCKS = 16
RNN_BLOCK_DIM = RNN_WIDTH // RNN_BLOCKS
RNN_CONV_WIDTH = 4
RG_LRU_C = 8.0
N_BUCKETS = 32
MAX_DISTANCE = 128
D_FF = 5632
FFN_CONV_WIDTH = 3
NORM_EPS = 1e-6
NEG_INF = -1e30
FORCE_BONUS = 1e3
IN_WIDTH = ATTN_WIDTH + 6 * KV_WIDTH + N_BRANCH * N_Q_HEADS + 2 * RNN_WIDTH

kernel_name = "hybrid_nsa_rglru_convffn"


def rms_norm(x, g):
    xf = x.astype(jnp.float32)
    y = xf * lax.rsqrt(jnp.mean(xf * xf, axis=-1, keepdims=True) + NORM_EPS)
    return (y * g.astype(jnp.float32)).astype(x.dtype)


def t5_bucket(dist):
    n = jnp.maximum(dist, 0)
    max_exact = N_BUCKETS // 2
    nf = jnp.maximum(n, 1).astype(jnp.float32)
    large = max_exact + (jnp.log(nf / max_exact) / math.log(MAX_DISTANCE / max_exact)
                         * (N_BUCKETS - max_exact)).astype(jnp.int32)
    large = jnp.minimum(large, N_BUCKETS - 1)
    return jnp.where(n < max_exact, n, large)


def causal_depthwise_conv(x, w, b):
    k = w.shape[0]
    y = lax.conv_general_dilated(x, w[:, None, :].astype(x.dtype), window_strides=(1,),
                                 padding=[(k - 1, 0)], dimension_numbers=('NWC', 'WIO', 'NWC'),
                                 feature_group_count=x.shape[-1])
    return y + b.astype(x.dtype)


def compress_tokens(t, pe, w1, w2):
    B, S, G, _ = t.shape
    n_cmp = (S - CMP_BLOCK) // CMP_STRIDE + 1
    idx = jnp.arange(n_cmp)[:, None] * CMP_STRIDE + jnp.arange(CMP_BLOCK)[None, :]
    blk = t[:, idx] + pe[None, None, :, None, :]
    blk = jnp.swapaxes(blk, 2, 3).reshape(B, n_cmp, G, CMP_BLOCK * HEAD_DIM)
    return jax.nn.gelu(blk @ w1) @ w2


def compressed_attention(q, k_cmp, v_cmp, bias_table):
    B, S, G, R, _ = q.shape
    n_cmp = k_cmp.shape[1]
    t = jnp.arange(S)[:, None]
    blk_end = jnp.arange(n_cmp)[None, :] * CMP_STRIDE + CMP_BLOCK - 1
    dist = t - blk_end
    mask = dist >= 0
    bias = jnp.moveaxis(bias_table[t5_bucket(dist)], -1, 0).reshape(G, R, S, n_cmp)
    s = jnp.einsum('bsgrd,bcgd->bgrsc', q, k_cmp) * SCALE
    logits = jnp.where(mask, s.astype(jnp.float32) + bias.astype(jnp.float32), NEG_INF)
    has_any = jnp.any(mask, axis=-1)[:, None].astype(jnp.float32)
    p = jax.nn.softmax(logits, axis=-1) * has_any
    o = jnp.einsum('bgrsc,bcgd->bsgrd', p.astype(v_cmp.dtype), v_cmp)
    return o, p


def select_blocks(p_cmp):
    S, n_cmp = p_cmp.shape[3], p_cmp.shape[4]
    n_sb = S // SEL_BLOCK
    c = np.arange(n_cmp)[:, None]
    j = np.arange(n_sb)[None, :]
    lo = np.maximum(c * CMP_STRIDE, j * SEL_BLOCK)
    hi = np.minimum(c * CMP_STRIDE + CMP_BLOCK, (j + 1) * SEL_BLOCK)
    overlap = jnp.asarray(np.maximum(hi - lo, 0) / CMP_BLOCK, dtype=jnp.float32)
    imp = jnp.einsum('bgrsc,cj->bgsj', p_cmp, overlap)
    t = jnp.arange(S)[:, None]
    jj = jnp.arange(n_sb)[None, :]
    cur = t // SEL_BLOCK
    valid = jj <= cur
    forced = ((jj == 0) | (jj == cur) | (jj == cur - 1)).astype(jnp.float32)
    score = jnp.where(valid, imp + FORCE_BONUS * forced, -1.0)
    n_sel = min(N_SELECT, n_sb)
    top_val, top_idx = lax.top_k(score, n_sel)
    return top_idx, top_val >= 0.0


def selected_attention(q, k_slc, v_slc, top_idx, top_valid, bias_table):
    B, S, G, R, hd = q.shape
    n_sb = S // SEL_BLOCK
    n_sel = top_idx.shape[-1]
    kb = k_slc.reshape(B, n_sb, SEL_BLOCK, G, hd).transpose(0, 3, 1, 2, 4)
    vb = v_slc.reshape(B, n_sb, SEL_BLOCK, G, hd).transpose(0, 3, 1, 2, 4)
    n_chunks = S // SEL_Q_CHUNK
    qc = jnp.moveaxis(q.reshape(B, n_chunks, SEL_Q_CHUNK, G, R, hd), 1, 0)
    ic = jnp.moveaxis(top_idx.reshape(B, G, n_chunks, SEL_Q_CHUNK, n_sel), 2, 0)
    vc = jnp.moveaxis(top_valid.reshape(B, G, n_chunks, SEL_Q_CHUNK, n_sel), 2, 0)
    tc = jnp.arange(S).reshape(n_chunks, SEL_Q_CHUNK)
    bias_g = bias_table.reshape(N_BUCKETS, G, R).transpose(1, 0, 2)
    bi = jnp.arange(B)[:, None, None, None]
    gi = jnp.arange(G)[None, :, None, None]

    def chunk(args):
        q_c, idx, val, t = args
        C = t.shape[0]
        kg = kb[bi, gi, idx]
        vg = vb[bi, gi, idx]
        tok = idx[..., None] * SEL_BLOCK + jnp.arange(SEL_BLOCK)
        dist = t[None, None, :, None, None] - tok
        mask = (dist >= 0) & val[..., None]
        bias = bias_g[gi[..., None], t5_bucket(dist)]
        s = jnp.einsum('bcgrd,bgcnld->bgrcnl', q_c, kg) * SCALE
        logits = s.astype(jnp.float32) + jnp.moveaxis(bias, -1, 2).astype(jnp.float32)
        logits = jnp.where(mask[:, :, None], logits, NEG_INF)
        p = jax.nn.softmax(logits.reshape(B, G, R, C, n_sel * SEL_BLOCK), axis=-1)
        return jnp.einsum('bgrcm,bgcmd->bcgrd', p.astype(vg.dtype),
                          vg.reshape(B, G, C, n_sel * SEL_BLOCK, hd))

    out = lax.map(chunk, (qc, ic, vc, tc))
    return jnp.moveaxis(out, 0, 1).reshape(B, S, G, R, hd)


def window_attention(q, k_win, v_win, bias_table):
    B, S, G, R, hd = q.shape
    n_qb = S // Q_BLOCK
    span = Q_BLOCK + WINDOW
    kp = jnp.pad(k_win, ((0, 0), (WINDOW, 0), (0, 0), (0, 0)))
    vp = jnp.pad(v_win, ((0, 0), (WINDOW, 0), (0, 0), (0, 0)))
    i = jnp.arange(Q_BLOCK)[:, None]
    j = jnp.arange(span)[None, :]
    rel = WINDOW + i - j
    band = (rel >= 0) & (rel < WINDOW)
    bias = jnp.moveaxis(bias_table[t5_bucket(rel)], -1, 0).reshape(G, R, Q_BLOCK, span)
    qb = jnp.moveaxis(q.reshape(B, n_qb, Q_BLOCK, G, R, hd), 1, 0)

    def block(args):
        q_b, b_idx = args
        start = b_idx * Q_BLOCK
        kw = lax.dynamic_slice_in_dim(kp, start, span, axis=1)
        vw = lax.dynamic_slice_in_dim(vp, start, span, axis=1)
        in_seq = (start + jnp.arange(span)) >= WINDOW
        mask = band & in_seq[None, :]
        s = jnp.einsum('bigrd,bjgd->bgrij', q_b, kw) * SCALE
        logits = jnp.where(mask, s.astype(jnp.float32) + bias.astype(jnp.float32), NEG_INF)
        p = jax.nn.softmax(logits, axis=-1)
        return jnp.einsum('bgrij,bjgd->bigrd', p.astype(vw.dtype), vw)

    out = lax.map(block, (qb, jnp.arange(n_qb)))
    return jnp.moveaxis(out, 0, 1).reshape(B, S, G, R, hd)


def rg_lru(x, w_a, b_a, w_x, b_x, lam):
    B, S, _ = x.shape
    xb = x.reshape(B, S, RNN_BLOCKS, RNN_BLOCK_DIM)
    r = jax.nn.sigmoid(jnp.einsum('bshi,hij->bshj', xb, w_a).reshape(B, S, RNN_WIDTH) + b_a)
    gi = jax.nn.sigmoid(jnp.einsum('bshi,hij->bshj', xb, w_x).reshape(B, S, RNN_WIDTH) + b_x)
    log_a = (-RG_LRU_C * jax.nn.softplus(-lam.astype(jnp.float32))) * r.astype(jnp.float32)
    a = jnp.exp(log_a)
    mult = jnp.sqrt(-jnp.expm1(2.0 * log_a))
    u = mult * (gi * x).astype(jnp.float32)

    def combine(left, right):
        a_l, u_l = left
        a_r, u_r = right
        return a_l * a_r, a_r * u_l + u_r

    _, h = lax.associative_scan(combine, (a, u), axis=1)
    return h.astype(x.dtype)


def setup_inputs(seed: int = 0) -> dict:
    key = jax.random.key(seed)
    ks = jax.random.split(key, 32)
    f32 = jnp.float32
    L = DEPTH

    def nrm(k, shape, scale):
        return jax.random.normal(k, shape, f32) * scale

    def gain(k, shape):
        return 1.0 + 0.02 * jax.random.normal(k, shape, f32)

    u = jax.random.uniform(ks[18], (L, RNN_WIDTH), f32, 0.9, 0.999)
    a = u ** (1.0 / RG_LRU_C)
    lam = jnp.log(a) - jnp.log1p(-a)
    return {
        "x": jax.random.normal(ks[0], (BATCH, SEQ, D_MODEL), f32),
        "mix_norm_g": gain(ks[1], (L, D_MODEL)),
        "w_in": nrm(ks[2], (L, D_MODEL, IN_WIDTH), D_MODEL ** -0.5),
        "b_gate": nrm(ks[3], (L, N_BRANCH * N_Q_HEADS), 0.02),
        "cmp_pe_k": nrm(ks[4], (L, CMP_BLOCK, HEAD_DIM), 0.5),
        "cmp_pe_v": nrm(ks[5], (L, CMP_BLOCK, HEAD_DIM), 0.5),
        "cmp_k_w1": nrm(ks[6], (L, CMP_BLOCK * HEAD_DIM, CMP_HIDDEN), (CMP_BLOCK * HEAD_DIM) ** -0.5),
        "cmp_k_w2": nrm(ks[7], (L, CMP_HIDDEN, HEAD_DIM), CMP_HIDDEN ** -0.5),
        "cmp_v_w1": nrm(ks[8], (L, CMP_BLOCK * HEAD_DIM, CMP_HIDDEN), (CMP_BLOCK * HEAD_DIM) ** -0.5),
        "cmp_v_w2": nrm(ks[9], (L, CMP_HIDDEN, HEAD_DIM), CMP_HIDDEN ** -0.5),
        "rel_bias": nrm(ks[10], (N_BUCKETS, N_Q_HEADS), 0.5),
        "rnn_conv_w": nrm(ks[11], (L, RNN_CONV_WIDTH, RNN_WIDTH), RNN_CONV_WIDTH ** -0.5),
        "rnn_conv_b": nrm(ks[12], (L, RNN_WIDTH), 0.02),
        "rg_a_w": nrm(ks[13], (L, RNN_BLOCKS, RNN_BLOCK_DIM, RNN_BLOCK_DIM), RNN_BLOCK_DIM ** -0.5),
        "rg_a_b": nrm(ks[14], (L, RNN_WIDTH), 0.02),
        "rg_x_w": nrm(ks[15], (L, RNN_BLOCKS, RNN_BLOCK_DIM, RNN_BLOCK_DIM), RNN_BLOCK_DIM ** -0.5),
        "rg_x_b": nrm(ks[16], (L, RNN_WIDTH), 0.02),
        "rg_lambda": lam,
        "attn_out_g": gain(ks[19], (L, ATTN_WIDTH)),
        "rnn_out_g": gain(ks[20], (L, RNN_WIDTH)),
        "w_out": nrm(ks[21], (L, D_MODEL, D_MODEL), D_MODEL ** -0.5),
        "ffn_norm_g": gain(ks[22], (L, D_MODEL)),
        "w_ffn_gate": nrm(ks[23], (L, D_MODEL, D_FF), D_MODEL ** -0.5),
        "w_ffn_up": nrm(ks[24], (L, D_MODEL, D_FF), D_MODEL ** -0.5),
        "ffn_conv_w": nrm(ks[25], (L, FFN_CONV_WIDTH, D_FF), FFN_CONV_WIDTH ** -0.5),
        "ffn_conv_b": nrm(ks[26], (L, D_FF), 0.02),
        "w_ffn_down": nrm(ks[27], (L, D_FF, D_MODEL), D_FF ** -0.5),
        "final_norm_g": gain(ks[28], (D_MODEL,)),
    }


def reference(x, mix_norm_g, w_in, b_gate, cmp_pe_k, cmp_pe_v, cmp_k_w1, cmp_k_w2, cmp_v_w1, cmp_v_w2,
              rel_bias, rnn_conv_w, rnn_conv_b, rg_a_w, rg_a_b, rg_x_w, rg_x_b, rg_lambda,
              attn_out_g, rnn_out_g, w_out, ffn_norm_g, w_ffn_gate, w_ffn_up, ffn_conv_w, ffn_conv_b,
              w_ffn_down, final_norm_g):
    B, S, _ = x.shape
    G, R, hd = N_KV_GROUPS, HEADS_PER_GROUP, HEAD_DIM
    sizes = [ATTN_WIDTH] + [KV_WIDTH] * 6 + [N_BRANCH * N_Q_HEADS, RNN_WIDTH, RNN_WIDTH]
    offsets = np.cumsum(sizes)[:-1].tolist()
    h = x
    for layer in range(DEPTH):
        y = rms_norm(h, mix_norm_g[layer])
        proj = y @ w_in[layer]
        q, kc, vc, ks_, vs_, kw, vw, gate, rx, ry = jnp.split(proj, offsets, axis=-1)
        q = q.reshape(B, S, G, R, hd)
        kc = kc.reshape(B, S, G, hd)
        vc = vc.reshape(B, S, G, hd)
        ks_ = ks_.reshape(B, S, G, hd)
        vs_ = vs_.reshape(B, S, G, hd)
        kw = kw.reshape(B, S, G, hd)
        vw = vw.reshape(B, S, G, hd)
        k_cmp = compress_tokens(kc, cmp_pe_k[layer], cmp_k_w1[layer], cmp_k_w2[layer])
        v_cmp = compress_tokens(vc, cmp_pe_v[layer], cmp_v_w1[layer], cmp_v_w2[layer])
        o_cmp, p_cmp = compressed_attention(q, k_cmp, v_cmp, rel_bias)
        top_idx, top_valid = select_blocks(p_cmp)
        o_slc = selected_attention(q, ks_, vs_, top_idx, top_valid, rel_bias)
        o_win = window_attention(q, kw, vw, rel_bias)
        g = jax.nn.sigmoid(gate + b_gate[layer]).reshape(B, S, G, R, N_BRANCH)
        o_attn = (g[..., 0:1] * o_cmp + g[..., 1:2] * o_slc + g[..., 2:3] * o_win).reshape(B, S, ATTN_WIDTH)
        xr = causal_depthwise_conv(rx, rnn_conv_w[layer], rnn_conv_b[layer])
        hr = rg_lru(xr, rg_a_w[layer], rg_a_b[layer], rg_x_w[layer], rg_x_b[layer], rg_lambda[layer])
        o_rnn = hr * jax.nn.gelu(ry)
        mixed = jnp.concatenate([rms_norm(o_attn, attn_out_g[layer]),
                                 rms_norm(o_rnn, rnn_out_g[layer])], axis=-1)
        h = h + mixed @ w_out[layer]
        y = rms_norm(h, ffn_norm_g[layer])
        u = jax.nn.gelu(causal_depthwise_conv(y @ w_ffn_gate[layer], ffn_conv_w[layer], ffn_conv_b[layer]))
        h = h + (u * (y @ w_ffn_up[layer])) @ w_ffn_down[layer]
    return rms_norm(h, final_norm_g)
```

```python
import functools
import math

import jax
import jax.numpy as jnp
import numpy as np
from jax import lax
from jax.experimental import pallas as pl
from jax.experimental.pallas import tpu as pltpu

F32 = jnp.float32
BF16 = jnp.bfloat16

D_MODEL = 2048
N_Q_HEADS = 16
N_KV_GROUPS = 4
HEADS_PER_GROUP = 4
HEAD_DIM = 64
ATTN_WIDTH = 1024
KV_WIDTH = 256
SCALE = HEAD_DIM ** -0.5
CMP_BLOCK = 32
CMP_STRIDE = 16
CMP_HIDDEN = 256
SEL_BLOCK = 64
N_SELECT = 16
WINDOW = 512
N_BRANCH = 3
RNN_WIDTH = 1024
RNN_BLOCKS = 16
RNN_CONV_WIDTH = 4
RG_LRU_C = 8.0
N_BUCKETS = 32
MAX_DISTANCE = 128
D_FF = 5632
NORM_EPS = 1e-6
NEG_INF = -1e30
FORCE_BONUS = 1e3

LANES = 128
SUBLANES = 8
VMEM_LIMIT = 56 * 1024 * 1024

TM_PROJ = 512
COL_CHUNK = 512
TM_FFN = 1024
TF_FFN = 512
TS_RNN = 256
CB_RNN = 256
TQ = 256
TK = 256
N_CMP_PAD = 128
GATE_PAD = 128

ATT_COLS = ATTN_WIDTH + 6 * KV_WIDTH
RNN_COLS = 2 * RNN_WIDTH + GATE_PAD


def _rms(x, g):
    return x * lax.rsqrt(jnp.mean(x * x, axis=-1, keepdims=True) + NORM_EPS) * g


def _gelu(x):
    return jax.nn.gelu(x, approximate=True)


def _dot(a, b):
    return jnp.dot(a, b, preferred_element_type=F32)


def _dot_nt(a, b):
    return lax.dot_general(a, b, (((1,), (1,)), ((), ())), preferred_element_type=F32)


def _dot_tn(a, b):
    return lax.dot_general(a, b, (((0,), (0,)), ((), ())), preferred_element_type=F32)


def _in_proj_kernel(x_ref, g_ref, w_ref, att_ref, rnn_ref):
    xn = _rms(x_ref[...], g_ref[...]).astype(BF16)
    for c0 in range(0, ATT_COLS, COL_CHUNK):
        att_ref[:, c0:c0 + COL_CHUNK] = _dot(xn, w_ref[:, c0:c0 + COL_CHUNK]).astype(BF16)
    for c0 in range(0, RNN_COLS, COL_CHUNK):
        c1 = min(c0 + COL_CHUNK, RNN_COLS)
        rnn_ref[:, c0:c1] = _dot(xn, w_ref[:, ATT_COLS + c0:ATT_COLS + c1])


def _in_proj(x2, g, w):
    n = x2.shape[0]
    return pl.pallas_call(
        _in_proj_kernel,
        grid=(n // TM_PROJ,),
        in_specs=[
            pl.BlockSpec((TM_PROJ, D_MODEL), lambda i: (i, 0)),
            pl.BlockSpec((1, D_MODEL), lambda i: (0, 0)),
            pl.BlockSpec((D_MODEL, ATT_COLS + RNN_COLS), lambda i: (0, 0), pipeline_mode=pl.Buffered(1)),
        ],
        out_specs=[
            pl.BlockSpec((TM_PROJ, ATT_COLS), lambda i: (i, 0)),
            pl.BlockSpec((TM_PROJ, RNN_COLS), lambda i: (i, 0)),
        ],
        out_shape=[
            jax.ShapeDtypeStruct((n, ATT_COLS), BF16),
            jax.ShapeDtypeStruct((n, RNN_COLS), F32),
        ],
        compiler_params=pltpu.CompilerParams(dimension_semantics=("parallel",), vmem_limit_bytes=VMEM_LIMIT),
        name="in_proj",
    )(x2, g, w)


def _out_proj_kernel(oa_ref, nr_ref, x_ref, ga_ref, gf_ref, w_ref, h_ref, y_ref):
    na = _rms(oa_ref[...], ga_ref[...]).astype(BF16)
    nr = nr_ref[...]
    for c0 in range(0, D_MODEL, COL_CHUNK):
        sl = slice(c0, c0 + COL_CHUNK)
        h_ref[:, sl] = (x_ref[:, sl] + _dot(na, w_ref[0:ATTN_WIDTH, sl])
                        + _dot(nr, w_ref[ATTN_WIDTH:D_MODEL, sl]))
    y_ref[...] = _rms(h_ref[...], gf_ref[...]).astype(BF16)


def _out_proj(o_attn, nr, x2, ga, gf, w):
    n = x2.shape[0]
    return pl.pallas_call(
        _out_proj_kernel,
        grid=(n // TM_PROJ,),
        in_specs=[
            pl.BlockSpec((TM_PROJ, ATTN_WIDTH), lambda i: (i, 0)),
            pl.BlockSpec((TM_PROJ, RNN_WIDTH), lambda i: (i, 0)),
            pl.BlockSpec((TM_PROJ, D_MODEL), lambda i: (i, 0)),
            pl.BlockSpec((1, ATTN_WIDTH), lambda i: (0, 0)),
            pl.BlockSpec((1, D_MODEL), lambda i: (0, 0)),
            pl.BlockSpec((D_MODEL, D_MODEL), lambda i: (0, 0), pipeline_mode=pl.Buffered(1)),
        ],
        out_specs=[
            pl.BlockSpec((TM_PROJ, D_MODEL), lambda i: (i, 0)),
            pl.BlockSpec((TM_PROJ, D_MODEL), lambda i: (i, 0)),
        ],
        out_shape=[
            jax.ShapeDtypeStruct((n, D_MODEL), F32),
            jax.ShapeDtypeStruct((n, D_MODEL), BF16),
        ],
        compiler_params=pltpu.CompilerParams(dimension_semantics=("parallel",), vmem_limit_bytes=VMEM_LIMIT),
        name="out_proj",
    )(o_attn, nr, x2, ga, gf, w)


def _ffn_kernel(tiles_per_seq, y_ref, h_ref, wg_ref, wu_ref, wd_ref, cw_ref, cb_ref, gf_ref, out_ref, carry_ref):
    i = pl.program_id(0)
    j = pl.program_id(1)
    nj = pl.num_programs(1)
    y = y_ref[...]
    gt = _dot(y, wg_ref[...])
    row = lax.broadcasted_iota(jnp.int32, gt.shape, 0)

    @pl.when((i % tiles_per_seq) == 0)
    def _():
        carry_ref[j] = jnp.zeros((SUBLANES, TF_FFN), F32)

    prev = carry_ref[j]
    carry_ref[j] = gt[TM_FFN - SUBLANES:TM_FFN, :]
    p1 = prev[SUBLANES - 1:SUBLANES, :]
    p2 = prev[SUBLANES - 2:SUBLANES - 1, :]
    s1 = jnp.where(row == 0, p1, pltpu.roll(gt, 1, 0))
    s2 = jnp.where(row == 0, p2, jnp.where(row == 1, p1, pltpu.roll(gt, 2, 0)))
    cw = cw_ref[...]
    conv = cw[2:3, :] * gt + cw[1:2, :] * s1 + cw[0:1, :] * s2 + cb_ref[...]
    u = _gelu(conv)
    up = _dot(y, wu_ref[...])
    z = (u * up).astype(BF16)

    @pl.when(j == 0)
    def _():
        out_ref[...] = h_ref[...]

    out_ref[...] += _dot(z, wd_ref[...])

    @pl.when(j == nj - 1)
    def _():
        out_ref[...] = _rms(out_ref[...], gf_ref[...])


def _conv_ffn(y, h, wg, wu, wd, cw, cb, gf, seq_len):
    n = y.shape[0]
    nj = D_FF // TF_FFN
    return pl.pallas_call(
        functools.partial(_ffn_kernel, seq_len // TM_FFN),
        grid=(n // TM_FFN, nj),
        in_specs=[
            pl.BlockSpec((TM_FFN, D_MODEL), lambda i, j: (i, 0)),
            pl.BlockSpec((TM_FFN, D_MODEL), lambda i, j: (i, 0), pipeline_mode=pl.Buffered(1)),
            pl.BlockSpec((D_MODEL, TF_FFN), lambda i, j: (0, j)),
            pl.BlockSpec((D_MODEL, TF_FFN), lambda i, j: (0, j)),
            pl.BlockSpec((TF_FFN, D_MODEL), lambda i, j: (j, 0)),
            pl.BlockSpec((3, TF_FFN), lambda i, j: (0, j)),
            pl.BlockSpec((1, TF_FFN), lambda i, j: (0, j)),
            pl.BlockSpec((1, D_MODEL), lambda i, j: (0, 0)),
        ],
        out_specs=pl.BlockSpec((TM_FFN, D_MODEL), lambda i, j: (i, 0)),
        out_shape=jax.ShapeDtypeStruct((n, D_MODEL), F32),
        scratch_shapes=[pltpu.VMEM((nj, SUBLANES, TF_FFN), F32)],
        compiler_params=pltpu.CompilerParams(dimension_semantics=("arbitrary", "arbitrary"),
                                             vmem_limit_bytes=VMEM_LIMIT),
        name="conv_ffn",
    )(y, h, wg, wu, wd, cw, cb, gf)


def _rglru_kernel(rx_ref, ry_ref, cw_ref, cb_ref, wa_ref, ba_ref, wx_ref, bx_ref, lam_ref, g_ref,
                  out_ref, tail_ref, hc_ref, a_ref, u_ref, o_ref):
    c = pl.program_id(1)

    @pl.when(c == 0)
    def _():
        tail_ref[...] = jnp.zeros(tail_ref.shape, F32)
        hc_ref[...] = jnp.zeros(hc_ref.shape, F32)

    row = lax.broadcasted_iota(jnp.int32, (TS_RNN, CB_RNN), 0)
    rmod = row % SUBLANES
    lam = lam_ref[...]
    nlam = -lam
    softplus = jnp.maximum(nlam, 0.0) + jnp.log1p(jnp.exp(-jnp.abs(nlam)))
    log_a_unit = -RG_LRU_C * softplus

    for cb in range(RNN_WIDTH // CB_RNN):
        sl = slice(cb * CB_RNN, (cb + 1) * CB_RNN)
        rx = rx_ref[:, sl]
        prev = tail_ref[:, sl]
        tail_ref[:, sl] = rx[TS_RNN - SUBLANES:TS_RNN, :]
        cw = cw_ref[:, sl]
        xr = cw[3:4, :] * rx + cb_ref[:, sl]
        for d in (1, 2, 3):
            sh = pltpu.roll(rx, d, 0)
            for k in range(d):
                sh = jnp.where(row == k, prev[SUBLANES - d + k:SUBLANES - d + k + 1, :], sh)
            xr = xr + cw[3 - d:4 - d, :] * sh
        xb = xr.astype(BF16)
        r = jax.nn.sigmoid(_dot(xb, wa_ref[cb]) + ba_ref[:, sl])
        gi = jax.nn.sigmoid(_dot(xb, wx_ref[cb]) + bx_ref[:, sl])
        log_a = log_a_unit[:, sl] * r
        a = jnp.exp(log_a)
        th = jnp.tanh(log_a)
        u = jnp.sqrt(-2.0 * th / (1.0 - th)) * (gi * xr)
        for d in (1, 2, 4):
            keep = rmod >= d
            a_s = pltpu.roll(a, d, 0)
            u_s = pltpu.roll(u, d, 0)
            u = jnp.where(keep, a * u_s + u, u)
            a = jnp.where(keep, a * a_s, a)
        a_ref[...] = a
        u_ref[...] = u

        def group(k, hprev):
            r0 = pl.multiple_of(k * SUBLANES, SUBLANES)
            h8 = u_ref[pl.ds(r0, SUBLANES), :] + a_ref[pl.ds(r0, SUBLANES), :] * hprev
            u_ref[pl.ds(r0, SUBLANES), :] = h8
            return jnp.broadcast_to(h8[SUBLANES - 1:SUBLANES, :], (SUBLANES, CB_RNN))

        hlast = lax.fori_loop(0, TS_RNN // SUBLANES, group, hc_ref[:, sl], unroll=4)
        hc_ref[:, sl] = hlast
        o_ref[:, sl] = u_ref[...] * _gelu(ry_ref[:, sl])

    out_ref[...] = _rms(o_ref[...], g_ref[...]).astype(BF16)


def _rglru(rnn_in, cw, cb, wa, ba, wx, bx, lam, g, batch, seq_len):
    n = rnn_in.shape[0]
    nc = seq_len // TS_RNN
    nb = RNN_WIDTH // CB_RNN
    vec = pl.BlockSpec((1, RNN_WIDTH), lambda b, c: (0, 0))
    wspec = pl.BlockSpec((nb, CB_RNN, CB_RNN), lambda b, c: (0, 0, 0))
    return pl.pallas_call(
        _rglru_kernel,
        grid=(batch, nc),
        in_specs=[
            pl.BlockSpec((TS_RNN, RNN_WIDTH), lambda b, c: (b * nc + c, 0)),
            pl.BlockSpec((TS_RNN, RNN_WIDTH), lambda b, c: (b * nc + c, 1)),
            pl.BlockSpec((RNN_CONV_WIDTH, RNN_WIDTH), lambda b, c: (0, 0)),
            vec, wspec, vec, wspec, vec, vec, vec,
        ],
        out_specs=pl.BlockSpec((TS_RNN, RNN_WIDTH), lambda b, c: (b * nc + c, 0)),
        out_shape=jax.ShapeDtypeStruct((n, RNN_WIDTH), BF16),
        scratch_shapes=[
            pltpu.VMEM((SUBLANES, RNN_WIDTH), F32),
            pltpu.VMEM((SUBLANES, RNN_WIDTH), F32),
            pltpu.VMEM((TS_RNN, CB_RNN), F32),
            pltpu.VMEM((TS_RNN, CB_RNN), F32),
            pltpu.VMEM((TS_RNN, RNN_WIDTH), F32),
        ],
        compiler_params=pltpu.CompilerParams(dimension_semantics=("arbitrary", "arbitrary"),
                                             vmem_limit_bytes=VMEM_LIMIT),
        name="rglru",
    )(rnn_in, rnn_in, cw, cb, wa, ba, wx, bx, lam, g)


def _block_diag(w):
    per = CB_RNN // (RNN_WIDTH // RNN_BLOCKS)
    bd = w.shape[-1]
    w4 = w.reshape(RNN_BLOCKS // per, per, bd, bd)
    eye = jnp.eye(per, dtype=w.dtype)
    return jnp.einsum('cpij,pq->cpiqj', w4, eye).reshape(RNN_BLOCKS // per, per * bd, per * bd)


CMP_ROWS = 1024
CHUNK_W = CMP_STRIDE * HEAD_DIM


def _cmp_tokens_kernel(x_ref, pe_ref, w1_ref, w2_ref, out_ref):
    x = x_ref[...]
    ya = _dot(x, w1_ref[0:CHUNK_W, :])
    yb = _dot(x, w1_ref[CHUNK_W:2 * CHUNK_W, :])
    pe = jnp.broadcast_to(pe_ref[...], (SUBLANES, 2 * CHUNK_W))
    pterm = _dot(pe, w1_ref[...])[0:1, :]
    hid = ya + pltpu.roll(yb, x.shape[0] - 1, 0) + pterm
    out_ref[...] = _dot(_gelu(hid).astype(BF16), w2_ref[...]).astype(BF16)


def _cmp_tokens(xkv, pe, w1, w2):
    rows = xkv.shape[1]
    tr = min(CMP_ROWS, rows)
    return pl.pallas_call(
        _cmp_tokens_kernel,
        grid=(2, rows // tr),
        in_specs=[
            pl.BlockSpec((None, tr, CHUNK_W), lambda s, i: (s, i, 0)),
            pl.BlockSpec((None, 1, 2 * CHUNK_W), lambda s, i: (s, 0, 0)),
            pl.BlockSpec((None, 2 * CHUNK_W, CMP_HIDDEN), lambda s, i: (s, 0, 0)),
            pl.BlockSpec((None, CMP_HIDDEN, LANES), lambda s, i: (s, 0, 0)),
        ],
        out_specs=pl.BlockSpec((None, tr, LANES), lambda s, i: (s, i, 0)),
        out_shape=jax.ShapeDtypeStruct((2, rows, LANES), BF16),
        compiler_params=pltpu.CompilerParams(dimension_semantics=("parallel", "parallel")),
        name="cmp_tokens",
    )(xkv, pe, w1, w2)


N_TILE_TYPES = 4


def _t5_bucket_np(dist):
    n = np.maximum(dist, 0)
    max_exact = N_BUCKETS // 2
    nf = np.maximum(n, 1).astype(np.float64)
    large = max_exact + (np.log(nf / max_exact) / math.log(MAX_DISTANCE / max_exact)
                         * (N_BUCKETS - max_exact)).astype(np.int32)
    large = np.minimum(large, N_BUCKETS - 1)
    return np.where(n < max_exact, n, large).astype(np.int32)


def _bucket_constants(seq_len):
    i = np.arange(TQ)[:, None]
    j = np.arange(TK)[None, :]
    d0 = i - j
    d1 = TK + i - j
    d2 = np.full((TQ, TK), 2 * TK)
    d3 = 2 * TK + i - j
    valid = [d0 >= 0, np.ones_like(d1, bool), np.ones_like(d2, bool), d3 < WINDOW]
    tiles = np.stack([np.where(v, _t5_bucket_np(d), -1) for d, v in zip([d0, d1, d2, d3], valid)])
    t = np.arange(seq_len)[:, None]
    c = np.arange(N_CMP_PAD)[None, :]
    dc = t - (c * CMP_STRIDE + CMP_BLOCK - 1)
    cmp = np.where(dc >= 0, _t5_bucket_np(dc), -1)
    return tiles.astype(np.int32), cmp.astype(np.int32)


def _bias_kernel(rb_ref, tb_ref, cbk_ref, tiles_ref, cmpb_ref):
    h = pl.program_id(0)

    def lookup(bkt):
        acc = jnp.full(bkt.shape, NEG_INF, F32)
        for b in range(N_BUCKETS):
            acc = jnp.where(bkt == b, rb_ref[b, h], acc)
        return acc

    for t in range(N_TILE_TYPES):
        tiles_ref[t] = lookup(tb_ref[t])
    cmpb_ref[...] = lookup(cbk_ref[...])


def _bias_tiles(rel_bias, seq_len):
    tb, cbk = _bucket_constants(seq_len)
    r = HEADS_PER_GROUP
    return pl.pallas_call(
        _bias_kernel,
        grid=(N_Q_HEADS,),
        in_specs=[
            pl.BlockSpec(memory_space=pltpu.SMEM),
            pl.BlockSpec((N_TILE_TYPES, TQ, TK), lambda h: (0, 0, 0)),
            pl.BlockSpec((seq_len, N_CMP_PAD), lambda h: (0, 0)),
        ],
        out_specs=[
            pl.BlockSpec((None, N_TILE_TYPES, None, TQ, TK), lambda h: (h // r, 0, h % r, 0, 0)),
            pl.BlockSpec((None, None, seq_len, N_CMP_PAD), lambda h: (h // r, h % r, 0, 0)),
        ],
        out_shape=[
            jax.ShapeDtypeStruct((N_KV_GROUPS, N_TILE_TYPES, r, TQ, TK), F32),
            jax.ShapeDtypeStruct((N_KV_GROUPS, r, seq_len, N_CMP_PAD), F32),
        ],
        compiler_params=pltpu.CompilerParams(dimension_semantics=("parallel",)),
        name="bias_tiles",
    )(rel_bias, jnp.asarray(tb), jnp.asarray(cbk))


def _overlap_constants(seq_len):
    n_sb = seq_len // SEL_BLOCK
    c = np.arange(N_CMP_PAD)[None, :]
    j = np.arange(n_sb)[:, None]
    lo = np.maximum(c * CMP_STRIDE, j * SEL_BLOCK)
    hi = np.minimum(c * CMP_STRIDE + CMP_BLOCK, (j + 1) * SEL_BLOCK)
    ov = np.maximum(hi - lo, 0) / CMP_BLOCK
    ov[:, N_CMP_PAD - 1] = 0.0
    s = np.arange(seq_len)[None, :]
    member = (s // SEL_BLOCK == j).astype(np.float32)
    return ov.astype(np.float32), member


def _nsa_kernel(n_sb, q_ref, kc_ref, vc_ref, ks_ref, vs_ref, kw_ref, vw_ref, gate_ref, bg_ref,
                tiles_ref, cmpb_ref, ov_ref, mem_ref, out_ref,
                qs_ref, mask_ref, m_ref, l_ref, acc_ref):
    g = pl.program_id(0)
    qi = pl.program_id(1)
    par = g % 2
    r_heads = HEADS_PER_GROUP
    lane = lax.broadcasted_iota(jnp.int32, (TQ, LANES), 1)
    in_half = (lane // HEAD_DIM) == par

    for blk in range(2):
        qb = q_ref[:, blk * LANES:(blk + 1) * LANES].astype(F32)
        qr = pltpu.roll(qb, HEAD_DIM, 1)
        for hh in range(2):
            r = blk * 2 + hh
            src = jnp.where(par == hh, qb, qr)
            qs_ref[r * TQ:(r + 1) * TQ, :] = (jnp.where(in_half, src, 0.0) * SCALE).astype(BF16)

    trow = qi * TQ + lax.broadcasted_iota(jnp.int32, (TQ, 1), 0)
    has_any = jnp.where(trow >= CMP_BLOCK - 1, 1.0, 0.0)
    kc = kc_ref[...]
    vc = vc_ref[...]
    psum = jnp.zeros((TQ, N_CMP_PAD), F32)
    o_cmp = []
    for r in range(r_heads):
        s = _dot_nt(qs_ref[r * TQ:(r + 1) * TQ, :], kc) + cmpb_ref[r]
        e = jnp.exp(s - jnp.max(s, axis=-1, keepdims=True))
        p = e / jnp.sum(e, axis=-1, keepdims=True) * has_any
        psum = psum + p
        o_cmp.append(_dot(p.astype(BF16), vc))

    p_hi = psum.astype(BF16)
    p_lo = (psum - p_hi.astype(F32)).astype(BF16)
    ov = ov_ref[...]
    imp = _dot_nt(ov, p_hi) + _dot_nt(ov, p_lo)
    jrow = lax.broadcasted_iota(jnp.int32, (n_sb, TQ), 0)
    tq = qi * TQ + lax.broadcasted_iota(jnp.int32, (n_sb, TQ), 1)
    cur = tq // SEL_BLOCK
    valid = jrow <= cur
    forced = jnp.where(jrow == 0, 1.0, jnp.where(jrow == cur, 1.0, jnp.where(jrow == cur - 1, 1.0, 0.0)))
    score = jnp.where(valid, imp + FORCE_BONUS * forced, -1.0)
    rank = jnp.zeros((n_sb, TQ), F32)
    for jp in range(n_sb):
        other = jnp.broadcast_to(score[jp:jp + 1, :], (n_sb, TQ))
        before = jnp.where(jrow > jp, 1.0, 0.0)
        rank = rank + jnp.where(other > score, 1.0, jnp.where(other == score, before, 0.0))
    sel = jnp.where(valid, jnp.where(rank < float(min(N_SELECT, n_sb)), 1.0, 0.0), 0.0).astype(BF16)
    for kj in range(mask_ref.shape[0]):
        hit = _dot_tn(sel, mem_ref[:, kj * TK:(kj + 1) * TK])
        mask_ref[kj] = (hit - 1.0) * (-NEG_INF)

    def attend(k_ref, v_ref, trips, tile_of, masked):
        m_ref[...] = jnp.full(m_ref.shape, NEG_INF, F32)
        l_ref[...] = jnp.zeros(l_ref.shape, F32)
        acc_ref[...] = jnp.zeros(acc_ref.shape, F32)

        def body(idx, carry):
            kj = qi - idx
            k0 = pl.multiple_of(kj * TK, TK)
            s = _dot_nt(qs_ref[...], k_ref[pl.ds(k0, TK), :])
            s3 = s.reshape(r_heads, TQ, TK) + tiles_ref[tile_of(idx)]
            if masked:
                s3 = s3 + mask_ref[kj][None]
            s = s3.reshape(r_heads * TQ, TK)
            m_prev = m_ref[...]
            m_new = jnp.maximum(m_prev, jnp.max(s, axis=-1, keepdims=True))
            alpha = jnp.exp(m_prev - m_new)
            p = jnp.exp(s - jnp.tile(m_new, (1, TK // LANES)))
            l_ref[...] = alpha * l_ref[...] + jnp.sum(p, axis=-1, keepdims=True)
            acc_ref[...] = alpha * acc_ref[...] + _dot(p.astype(BF16), v_ref[pl.ds(k0, TK), :])
            m_ref[...] = m_new
            return carry

        lax.fori_loop(0, trips, body, 0)
        return acc_ref[...] / l_ref[...]

    o_slc = attend(ks_ref, vs_ref, qi + 1, lambda idx: jnp.minimum(idx, 2), True)
    o_win = attend(kw_ref, vw_ref, jnp.minimum(qi, 2) + 1, lambda idx: jnp.where(idx == 2, 3, idx), False)

    gs = jax.nn.sigmoid(gate_ref[...] + bg_ref[...])
    heads = []
    for r in range(r_heads):
        base = N_BRANCH * (r_heads * g + r)
        w = [jnp.sum(jnp.where(lane == base + br, gs, 0.0), axis=-1, keepdims=True) for br in range(N_BRANCH)]
        heads.append(w[0] * o_cmp[r] + w[1] * o_slc[r * TQ:(r + 1) * TQ, :] + w[2] * o_win[r * TQ:(r + 1) * TQ, :])
    for blk in range(2):
        lo, hi = heads[2 * blk], heads[2 * blk + 1]
        lo = jnp.where(par == 0, lo, pltpu.roll(lo, HEAD_DIM, 1))
        hi = jnp.where(par == 1, hi, pltpu.roll(hi, HEAD_DIM, 1))
        out_ref[:, blk * LANES:(blk + 1) * LANES] = jnp.where(lane < HEAD_DIM, lo, hi)


def _nsa_attn(att, rnn_in, kvc, bgate, tiles, cmpb, batch, seq_len):
    n = att.shape[0]
    nq = seq_len // TQ
    nk = seq_len // TK
    n_sb = seq_len // SEL_BLOCK
    ng = N_KV_GROUPS
    r = HEADS_PER_GROUP
    ov, member = _overlap_constants(seq_len)
    qw = r * HEAD_DIM
    kcol = ATTN_WIDTH + 2 * KV_WIDTH
    kvb = lambda off: (lambda g, qi, b: (b, (kcol + off * KV_WIDTH) // LANES + g // 2))
    return pl.pallas_call(
        functools.partial(_nsa_kernel, n_sb),
        grid=(ng, nq, batch),
        in_specs=[
            pl.BlockSpec((TQ, qw), lambda g, qi, b: (b * nq + qi, g)),
            pl.BlockSpec((None, None, N_CMP_PAD, LANES), lambda g, qi, b: (0, b * ng + g, 0, 0)),
            pl.BlockSpec((None, None, N_CMP_PAD, LANES), lambda g, qi, b: (1, b * ng + g, 0, 0)),
            pl.BlockSpec((seq_len, LANES), kvb(0)),
            pl.BlockSpec((seq_len, LANES), kvb(1)),
            pl.BlockSpec((seq_len, LANES), kvb(2)),
            pl.BlockSpec((seq_len, LANES), kvb(3)),
            pl.BlockSpec((TQ, GATE_PAD), lambda g, qi, b: (b * nq + qi, 2 * RNN_WIDTH // GATE_PAD)),
            pl.BlockSpec((1, GATE_PAD), lambda g, qi, b: (0, 0)),
            pl.BlockSpec((None, N_TILE_TYPES, r, TQ, TK), lambda g, qi, b: (g, 0, 0, 0, 0)),
            pl.BlockSpec((None, r, TQ, N_CMP_PAD), lambda g, qi, b: (g, 0, qi, 0)),
            pl.BlockSpec((n_sb, N_CMP_PAD), lambda g, qi, b: (0, 0)),
            pl.BlockSpec((n_sb, seq_len), lambda g, qi, b: (0, 0)),
        ],
        out_specs=pl.BlockSpec((TQ, qw), lambda g, qi, b: (b * nq + qi, g)),
        out_shape=jax.ShapeDtypeStruct((n, ATTN_WIDTH), F32),
        scratch_shapes=[
            pltpu.VMEM((r * TQ, LANES), BF16),
            pltpu.VMEM((nk, TQ, TK), F32),
            pltpu.VMEM((r * TQ, LANES), F32),
            pltpu.VMEM((r * TQ, LANES), F32),
            pltpu.VMEM((r * TQ, LANES), F32),
        ],
        compiler_params=pltpu.CompilerParams(dimension_semantics=("arbitrary", "arbitrary", "arbitrary"),
                                             vmem_limit_bytes=VMEM_LIMIT),
        name="nsa_attn",
    )(att, kvc, kvc, att, att, att, att, rnn_in, bgate, tiles, cmpb,
      jnp.asarray(ov, BF16), jnp.asarray(member, BF16))


def kernel(x, mix_norm_g, w_in, b_gate, cmp_pe_k, cmp_pe_v, cmp_k_w1, cmp_k_w2, cmp_v_w1, cmp_v_w2, rel_bias, rnn_conv_w, rnn_conv_b, rg_a_w, rg_a_b, rg_x_w, rg_x_b, rg_lambda, attn_out_g, rnn_out_g, w_out, ffn_norm_g, w_ffn_gate, w_ffn_up, ffn_conv_w, ffn_conv_b, w_ffn_down, final_norm_g):
    batch, seq_len, d_model = x.shape
    assert d_model == D_MODEL and w_in.shape[0] == 1, "single-layer model of the stated width"
    assert seq_len // CMP_STRIDE == N_CMP_PAD and seq_len % TM_FFN == 0
    n = batch * seq_len
    ng, hd = N_KV_GROUPS, HEAD_DIM
    x2 = x.reshape(n, D_MODEL)

    w0 = w_in[0]
    gate0 = ATT_COLS
    rx0 = gate0 + N_BRANCH * N_Q_HEADS
    w_perm = jnp.concatenate(
        [w0[:, :ATT_COLS], w0[:, rx0:rx0 + 2 * RNN_WIDTH], w0[:, gate0:rx0],
         jnp.zeros((D_MODEL, GATE_PAD - N_BRANCH * N_Q_HEADS), w0.dtype)], axis=1).astype(BF16)
    att, rnn_in = _in_proj(x2, mix_norm_g[0][None], w_perm)

    nch = seq_len // CMP_STRIDE
    chunks = att[:, ATTN_WIDTH:ATTN_WIDTH + 2 * KV_WIDTH].reshape(batch, nch, CMP_STRIDE, 2, ng, hd)
    chunks = chunks.transpose(3, 0, 4, 1, 2, 5).reshape(2, batch * ng * nch, CHUNK_W)
    pe = jnp.stack([cmp_pe_k[0].reshape(1, -1), cmp_pe_v[0].reshape(1, -1)]).astype(BF16)
    w1 = jnp.stack([cmp_k_w1[0], cmp_v_w1[0]]).astype(BF16)
    w2 = jnp.stack([jnp.tile(cmp_k_w2[0], (1, 2)), jnp.tile(cmp_v_w2[0], (1, 2))]).astype(BF16)
    kvc = _cmp_tokens(chunks, pe, w1, w2).reshape(2, batch * ng, N_CMP_PAD, LANES)

    tiles, cmpb = _bias_tiles(rel_bias, seq_len)
    bgate = jnp.pad(b_gate[0], (0, GATE_PAD - b_gate.shape[1]))[None]
    o_attn = _nsa_attn(att, rnn_in, kvc, bgate, tiles, cmpb, batch, seq_len)

    nr = _rglru(rnn_in, rnn_conv_w[0], rnn_conv_b[0][None],
                _block_diag(rg_a_w[0]).astype(BF16), rg_a_b[0][None],
                _block_diag(rg_x_w[0]).astype(BF16), rg_x_b[0][None],
                rg_lambda[0][None], rnn_out_g[0][None], batch, seq_len)

    h, y = _out_proj(o_attn, nr, x2, attn_out_g[0][None], ffn_norm_g[0][None], w_out[0].astype(BF16))
    out = _conv_ffn(y, h, w_ffn_gate[0].astype(BF16), w_ffn_up[0].astype(BF16), w_ffn_down[0].astype(BF16),
                    ffn_conv_w[0], ffn_conv_b[0][None], final_norm_g[None], seq_len)
    return out.reshape(batch, seq_len, D_MODEL)
```

```python
import functools
import math

import jax
import jax.numpy as jnp
import numpy as np
from jax import lax
from jax.experimental import pallas as pl
from jax.experimental.pallas import tpu as pltpu

F32 = jnp.float32
BF16 = jnp.bfloat16

D_MODEL = 2048
N_Q_HEADS = 16
N_KV_GROUPS = 4
HEADS_PER_GROUP = 4
HEAD_DIM = 64
ATTN_WIDTH = 1024
KV_WIDTH = 256
SCALE = HEAD_DIM ** -0.5
CMP_BLOCK = 32
CMP_STRIDE = 16
CMP_HIDDEN = 256
SEL_BLOCK = 64
N_SELECT = 16
WINDOW = 512
N_BRANCH = 3
RNN_WIDTH = 1024
RNN_BLOCKS = 16
RNN_CONV_WIDTH = 4
RG_LRU_C = 8.0
N_BUCKETS = 32
MAX_DISTANCE = 128
D_FF = 5632
NORM_EPS = 1e-6
NEG_INF = -1e30
FORCE_BONUS = 1e3
GELU_C0 = math.sqrt(2.0 / math.pi)
GELU_C1 = GELU_C0 * 0.044715

LANES = 128
SUBLANES = 8
VMEM_LIMIT = 56 * 1024 * 1024

TM_PROJ = 512
COL_CHUNK = 512
TM_FFN = 1024
TF_FFN = 512
TS_RNN = 256
CB_RNN = 256
TQ = 256
TK = 256
ROW_SPLIT = 512
N_CMP_PAD = 128
GATE_PAD = 128

ATT_COLS = ATTN_WIDTH + 6 * KV_WIDTH
RNN_COLS = 2 * RNN_WIDTH + GATE_PAD


def _rms(x, g):
    return x * lax.rsqrt(jnp.mean(x * x, axis=-1, keepdims=True) + NORM_EPS) * g


def _gelu(x):
    return jax.nn.gelu(x, approximate=True)


def _dot(a, b):
    return jnp.dot(a, b, preferred_element_type=F32)


def _dot_nt(a, b):
    return lax.dot_general(a, b, (((1,), (1,)), ((), ())), preferred_element_type=F32)


def _dot_tn(a, b):
    return lax.dot_general(a, b, (((0,), (0,)), ((), ())), preferred_element_type=F32)


def _in_proj_kernel(x_ref, g_ref, w_ref, att_ref, rnn_ref):
    xn = _rms(x_ref[...], g_ref[...]).astype(BF16)
    for c0 in range(0, ATT_COLS, COL_CHUNK):
        att_ref[:, c0:c0 + COL_CHUNK] = _dot(xn, w_ref[:, c0:c0 + COL_CHUNK]).astype(BF16)
    for c0 in range(0, RNN_COLS, COL_CHUNK):
        c1 = min(c0 + COL_CHUNK, RNN_COLS)
        rnn_ref[:, c0:c1] = _dot(xn, w_ref[:, ATT_COLS + c0:ATT_COLS + c1])


def _in_proj(x2, g, w):
    n = x2.shape[0]
    return pl.pallas_call(
        _in_proj_kernel,
        grid=(n // TM_PROJ,),
        in_specs=[
            pl.BlockSpec((TM_PROJ, D_MODEL), lambda i: (i, 0)),
            pl.BlockSpec((1, D_MODEL), lambda i: (0, 0)),
            pl.BlockSpec((D_MODEL, ATT_COLS + RNN_COLS), lambda i: (0, 0), pipeline_mode=pl.Buffered(1)),
        ],
        out_specs=[
            pl.BlockSpec((TM_PROJ, ATT_COLS), lambda i: (i, 0)),
            pl.BlockSpec((TM_PROJ, RNN_COLS), lambda i: (i, 0)),
        ],
        out_shape=[
            jax.ShapeDtypeStruct((n, ATT_COLS), BF16),
            jax.ShapeDtypeStruct((n, RNN_COLS), F32),
        ],
        compiler_params=pltpu.CompilerParams(dimension_semantics=("parallel",), vmem_limit_bytes=VMEM_LIMIT),
        name="in_proj",
    )(x2, g, w)


def _out_proj_kernel(oa_ref, nr_ref, x_ref, ga_ref, gf_ref, w_ref, h_ref, y_ref):
    na = _rms(oa_ref[...], ga_ref[...]).astype(BF16)
    nr = nr_ref[...]
    for c0 in range(0, D_MODEL, COL_CHUNK):
        sl = slice(c0, c0 + COL_CHUNK)
        h_ref[:, sl] = (x_ref[:, sl] + _dot(na, w_ref[0:ATTN_WIDTH, sl])
                        + _dot(nr, w_ref[ATTN_WIDTH:D_MODEL, sl]))
    y_ref[...] = _rms(h_ref[...], gf_ref[...]).astype(BF16)


def _out_proj(o_attn, nr, x2, ga, gf, w):
    n = x2.shape[0]
    return pl.pallas_call(
        _out_proj_kernel,
        grid=(n // TM_PROJ,),
        in_specs=[
            pl.BlockSpec((TM_PROJ, ATTN_WIDTH), lambda i: (i, 0)),
            pl.BlockSpec((TM_PROJ, RNN_WIDTH), lambda i: (i, 0)),
            pl.BlockSpec((TM_PROJ, D_MODEL), lambda i: (i, 0)),
            pl.BlockSpec((1, ATTN_WIDTH), lambda i: (0, 0)),
            pl.BlockSpec((1, D_MODEL), lambda i: (0, 0)),
            pl.BlockSpec((D_MODEL, D_MODEL), lambda i: (0, 0), pipeline_mode=pl.Buffered(1)),
        ],
        out_specs=[
            pl.BlockSpec((TM_PROJ, D_MODEL), lambda i: (i, 0)),
            pl.BlockSpec((TM_PROJ, D_MODEL), lambda i: (i, 0)),
        ],
        out_shape=[
            jax.ShapeDtypeStruct((n, D_MODEL), F32),
            jax.ShapeDtypeStruct((n, D_MODEL), BF16),
        ],
        compiler_params=pltpu.CompilerParams(dimension_semantics=("parallel",), vmem_limit_bytes=VMEM_LIMIT),
        name="out_proj",
    )(o_attn, nr, x2, ga, gf, w)


def _ffn_kernel(tiles_per_seq, y_ref, h_ref, wg_ref, wu_ref, wd_ref, cw_ref, cb_ref, gf_ref, out_ref, carry_ref):
    i = pl.program_id(0)
    j = pl.program_id(1)
    nj = pl.num_programs(1)
    y = y_ref[...]
    gt = _dot(y, wg_ref[...])
    row = lax.broadcasted_iota(jnp.int32, (SUBLANES, TF_FFN), 0)

    @pl.when((i % tiles_per_seq) == 0)
    def _():
        carry_ref[j] = jnp.zeros((SUBLANES, TF_FFN), F32)

    prev = carry_ref[j]
    carry_ref[j] = gt[TM_FFN - SUBLANES:TM_FFN, :]
    p1 = prev[SUBLANES - 1:SUBLANES, :]
    p2 = prev[SUBLANES - 2:SUBLANES - 1, :]
    r1 = pltpu.roll(gt, 1, 0)
    r2 = pltpu.roll(gt, 2, 0)
    h1 = jnp.where(row == 0, p1, r1[0:SUBLANES])
    h2 = jnp.where(row == 0, p2, jnp.where(row == 1, p1, r2[0:SUBLANES]))
    s1 = jnp.concatenate([h1, r1[SUBLANES:]], axis=0)
    s2 = jnp.concatenate([h2, r2[SUBLANES:]], axis=0)
    cw = cw_ref[...]
    conv = cw[2:3, :] * gt + cw[1:2, :] * s1 + cw[0:1, :] * s2 + cb_ref[...]
    inner = conv * (GELU_C0 + GELU_C1 * (conv * conv))
    hx = 0.5 * conv
    u = hx + hx * jnp.tanh(inner)
    up = _dot(y, wu_ref[...])
    z = (u * up).astype(BF16)

    @pl.when(j == 0)
    def _():
        out_ref[...] = h_ref[...]

    out_ref[...] += _dot(z, wd_ref[...])

    @pl.when(j == nj - 1)
    def _():
        out_ref[...] = _rms(out_ref[...], gf_ref[...])


def _conv_ffn(y, h, wg, wu, wd, cw, cb, gf, seq_len):
    n = y.shape[0]
    nj = D_FF // TF_FFN
    return pl.pallas_call(
        functools.partial(_ffn_kernel, seq_len // TM_FFN),
        grid=(n // TM_FFN, nj),
        in_specs=[
            pl.BlockSpec((TM_FFN, D_MODEL), lambda i, j: (i, 0)),
            pl.BlockSpec((TM_FFN, D_MODEL), lambda i, j: (i, 0), pipeline_mode=pl.Buffered(1)),
            pl.BlockSpec((D_MODEL, TF_FFN), lambda i, j: (0, j)),
            pl.BlockSpec((D_MODEL, TF_FFN), lambda i, j: (0, j)),
            pl.BlockSpec((TF_FFN, D_MODEL), lambda i, j: (j, 0)),
            pl.BlockSpec((3, TF_FFN), lambda i, j: (0, j)),
            pl.BlockSpec((1, TF_FFN), lambda i, j: (0, j)),
            pl.BlockSpec((1, D_MODEL), lambda i, j: (0, 0)),
        ],
        out_specs=pl.BlockSpec((TM_FFN, D_MODEL), lambda i, j: (i, 0)),
        out_shape=jax.ShapeDtypeStruct((n, D_MODEL), F32),
        scratch_shapes=[pltpu.VMEM((nj, SUBLANES, TF_FFN), F32)],
        compiler_params=pltpu.CompilerParams(dimension_semantics=("arbitrary", "arbitrary"),
                                             vmem_limit_bytes=VMEM_LIMIT),
        name="conv_ffn",
    )(y, h, wg, wu, wd, cw, cb, gf)


def _rglru_kernel(rx_ref, ry_ref, cw_ref, cb_ref, wa_ref, ba_ref, wx_ref, bx_ref, lam_ref, g_ref,
                  out_ref, tail_ref, hc_ref, a_ref, u_ref, o_ref):
    c = pl.program_id(1)

    @pl.when(c == 0)
    def _():
        tail_ref[...] = jnp.zeros(tail_ref.shape, F32)
        hc_ref[...] = jnp.zeros(hc_ref.shape, F32)

    row = lax.broadcasted_iota(jnp.int32, (TS_RNN, CB_RNN), 0)
    rmod = row % SUBLANES
    lam = lam_ref[...]
    nlam = -lam
    softplus = jnp.maximum(nlam, 0.0) + jnp.log1p(jnp.exp(-jnp.abs(nlam)))
    log_a_unit = -RG_LRU_C * softplus

    for cb in range(RNN_WIDTH // CB_RNN):
        sl = slice(cb * CB_RNN, (cb + 1) * CB_RNN)
        rx = rx_ref[:, sl]
        prev = tail_ref[:, sl]
        tail_ref[:, sl] = rx[TS_RNN - SUBLANES:TS_RNN, :]
        cw = cw_ref[:, sl]
        xr = cw[3:4, :] * rx + cb_ref[:, sl]
        for d in (1, 2, 3):
            sh = pltpu.roll(rx, d, 0)
            for k in range(d):
                sh = jnp.where(row == k, prev[SUBLANES - d + k:SUBLANES - d + k + 1, :], sh)
            xr = xr + cw[3 - d:4 - d, :] * sh
        xb = xr.astype(BF16)
        r = jax.nn.sigmoid(_dot(xb, wa_ref[cb]) + ba_ref[:, sl])
        gi = jax.nn.sigmoid(_dot(xb, wx_ref[cb]) + bx_ref[:, sl])
        log_a = log_a_unit[:, sl] * r
        a = jnp.exp(log_a)
        th = jnp.tanh(log_a)
        u = jnp.sqrt(-2.0 * th / (1.0 - th)) * (gi * xr)
        for d in (1, 2, 4):
            keep = rmod >= d
            a_s = pltpu.roll(a, d, 0)
            u_s = pltpu.roll(u, d, 0)
            u = jnp.where(keep, a * u_s + u, u)
            a = jnp.where(keep, a * a_s, a)
        a_ref[...] = a
        u_ref[...] = u

        def group(k, hprev):
            r0 = pl.multiple_of(k * SUBLANES, SUBLANES)
            h8 = u_ref[pl.ds(r0, SUBLANES), :] + a_ref[pl.ds(r0, SUBLANES), :] * hprev
            u_ref[pl.ds(r0, SUBLANES), :] = h8
            return jnp.broadcast_to(h8[SUBLANES - 1:SUBLANES, :], (SUBLANES, CB_RNN))

        hlast = lax.fori_loop(0, TS_RNN // SUBLANES, group, hc_ref[:, sl], unroll=4)
        hc_ref[:, sl] = hlast
        o_ref[:, sl] = u_ref[...] * _gelu(ry_ref[:, sl])

    out_ref[...] = _rms(o_ref[...], g_ref[...]).astype(BF16)


def _rglru(rnn_in, cw, cb, wa, ba, wx, bx, lam, g, batch, seq_len):
    n = rnn_in.shape[0]
    nc = seq_len // TS_RNN
    nb = RNN_WIDTH // CB_RNN
    vec = pl.BlockSpec((1, RNN_WIDTH), lambda b, c: (0, 0))
    wspec = pl.BlockSpec((nb, CB_RNN, CB_RNN), lambda b, c: (0, 0, 0))
    return pl.pallas_call(
        _rglru_kernel,
        grid=(batch, nc),
        in_specs=[
            pl.BlockSpec((TS_RNN, RNN_WIDTH), lambda b, c: (b * nc + c, 0)),
            pl.BlockSpec((TS_RNN, RNN_WIDTH), lambda b, c: (b * nc + c, 1)),
            pl.BlockSpec((RNN_CONV_WIDTH, RNN_WIDTH), lambda b, c: (0, 0)),
            vec, wspec, vec, wspec, vec, vec, vec,
        ],
        out_specs=pl.BlockSpec((TS_RNN, RNN_WIDTH), lambda b, c: (b * nc + c, 0)),
        out_shape=jax.ShapeDtypeStruct((n, RNN_WIDTH), BF16),
        scratch_shapes=[
            pltpu.VMEM((SUBLANES, RNN_WIDTH), F32),
            pltpu.VMEM((SUBLANES, RNN_WIDTH), F32),
            pltpu.VMEM((TS_RNN, CB_RNN), F32),
            pltpu.VMEM((TS_RNN, CB_RNN), F32),
            pltpu.VMEM((TS_RNN, RNN_WIDTH), F32),
        ],
        compiler_params=pltpu.CompilerParams(dimension_semantics=("arbitrary", "arbitrary"),
                                             vmem_limit_bytes=VMEM_LIMIT),
        name="rglru",
    )(rnn_in, rnn_in, cw, cb, wa, ba, wx, bx, lam, g)


def _block_diag(w):
    per = CB_RNN // (RNN_WIDTH // RNN_BLOCKS)
    bd = w.shape[-1]
    w4 = w.reshape(RNN_BLOCKS // per, per, bd, bd)
    eye = jnp.eye(per, dtype=w.dtype)
    return jnp.einsum('cpij,pq->cpiqj', w4, eye).reshape(RNN_BLOCKS // per, per * bd, per * bd)


CMP_ROWS = 1024
CHUNK_W = CMP_STRIDE * HEAD_DIM


def _cmp_tokens_kernel(x_ref, pe_ref, w1_ref, w2_ref, out_ref):
    x = x_ref[...]
    ya = _dot(x, w1_ref[0:CHUNK_W, :])
    yb = _dot(x, w1_ref[CHUNK_W:2 * CHUNK_W, :])
    pe = jnp.broadcast_to(pe_ref[...], (SUBLANES, 2 * CHUNK_W))
    pterm = _dot(pe, w1_ref[...])[0:1, :]
    hid = ya + pltpu.roll(yb, x.shape[0] - 1, 0) + pterm
    out_ref[...] = _dot(_gelu(hid).astype(BF16), w2_ref[...]).astype(BF16)


def _cmp_tokens(xkv, pe, w1, w2):
    rows = xkv.shape[1]
    tr = min(CMP_ROWS, rows)
    return pl.pallas_call(
        _cmp_tokens_kernel,
        grid=(2, rows // tr),
        in_specs=[
            pl.BlockSpec((None, tr, CHUNK_W), lambda s, i: (s, i, 0)),
            pl.BlockSpec((None, 1, 2 * CHUNK_W), lambda s, i: (s, 0, 0)),
            pl.BlockSpec((None, 2 * CHUNK_W, CMP_HIDDEN), lambda s, i: (s, 0, 0)),
            pl.BlockSpec((None, CMP_HIDDEN, LANES), lambda s, i: (s, 0, 0)),
        ],
        out_specs=pl.BlockSpec((None, tr, LANES), lambda s, i: (s, i, 0)),
        out_shape=jax.ShapeDtypeStruct((2, rows, LANES), BF16),
        compiler_params=pltpu.CompilerParams(dimension_semantics=("parallel", "parallel")),
        name="cmp_tokens",
    )(xkv, pe, w1, w2)


N_TILE_TYPES = 4
TILE_DIAG, TILE_NEAR, TILE_WIN_LAST, TILE_NONE = 0, 1, 2, 3


def _t5_bucket_np(dist):
    n = np.maximum(dist, 0)
    max_exact = N_BUCKETS // 2
    nf = np.maximum(n, 1).astype(np.float64)
    large = max_exact + (np.log(nf / max_exact) / math.log(MAX_DISTANCE / max_exact)
                         * (N_BUCKETS - max_exact)).astype(np.int32)
    large = np.minimum(large, N_BUCKETS - 1)
    return np.where(n < max_exact, n, large).astype(np.int32)


def _bucket_constants(seq_len):
    i = np.arange(TQ)[:, None]
    j = np.arange(TK)[None, :]
    d0 = i - j
    d1 = TK + i - j
    d2 = 2 * TK + i - j
    valid = [d0 >= 0, np.ones_like(d1, bool), d2 < WINDOW, np.zeros_like(d0, bool)]
    tiles = np.stack([np.where(v, _t5_bucket_np(d), -1) for d, v in zip([d0, d1, d2, d0], valid)])
    t = np.arange(seq_len)[:, None]
    c = np.arange(N_CMP_PAD)[None, :]
    dc = t - (c * CMP_STRIDE + CMP_BLOCK - 1)
    cmp = np.where(dc >= 0, _t5_bucket_np(dc), -1)
    return tiles.astype(np.int32), cmp.astype(np.int32)


def _bias_kernel(rb_ref, tb_ref, cbk_ref, tiles_ref, cmpb_ref):
    h = pl.program_id(0)

    def lookup(bkt):
        acc = jnp.full(bkt.shape, NEG_INF, F32)
        for b in range(N_BUCKETS):
            acc = jnp.where(bkt == b, rb_ref[b, h], acc)
        return acc

    for t in range(N_TILE_TYPES):
        tiles_ref[t] = lookup(tb_ref[t])
    cmpb_ref[...] = lookup(cbk_ref[...])


def _bias_tiles(rel_bias, seq_len):
    tb, cbk = _bucket_constants(seq_len)
    r = HEADS_PER_GROUP
    return pl.pallas_call(
        _bias_kernel,
        grid=(N_Q_HEADS,),
        in_specs=[
            pl.BlockSpec(memory_space=pltpu.SMEM),
            pl.BlockSpec((N_TILE_TYPES, TQ, TK), lambda h: (0, 0, 0)),
            pl.BlockSpec((seq_len, N_CMP_PAD), lambda h: (0, 0)),
        ],
        out_specs=[
            pl.BlockSpec((None, N_TILE_TYPES, None, TQ, TK), lambda h: (h // r, 0, h % r, 0, 0)),
            pl.BlockSpec((None, None, seq_len, N_CMP_PAD), lambda h: (h // r, h % r, 0, 0)),
        ],
        out_shape=[
            jax.ShapeDtypeStruct((N_KV_GROUPS, N_TILE_TYPES, r, TQ, TK), F32),
            jax.ShapeDtypeStruct((N_KV_GROUPS, r, seq_len, N_CMP_PAD), F32),
        ],
        compiler_params=pltpu.CompilerParams(dimension_semantics=("parallel",)),
        name="bias_tiles",
    )(rel_bias, jnp.asarray(tb), jnp.asarray(cbk))


SIDE_FAR_HI, SIDE_FAR_LO, SIDE_OFF = 32, 33, 34
KSIDE_NEAR, KSIDE_FAR, KSIDE_OFF = 0, 1, 2


def _key_side_constants(seq_len):
    s = np.arange(seq_len)
    side = np.zeros((3, seq_len, LANES), np.float32)
    side[:, s, s // SEL_BLOCK] = 1.0
    side[KSIDE_FAR, :, SIDE_FAR_HI] = 1.0
    side[KSIDE_FAR, :, SIDE_FAR_LO] = 1.0
    side[KSIDE_OFF, :, SIDE_OFF] = 1.0
    return side


def _overlap_constants(seq_len):
    n_sb = seq_len // SEL_BLOCK
    c = np.arange(N_CMP_PAD)[None, :]
    j = np.arange(n_sb)[:, None]
    lo = np.maximum(c * CMP_STRIDE, j * SEL_BLOCK)
    hi = np.minimum(c * CMP_STRIDE + CMP_BLOCK, (j + 1) * SEL_BLOCK)
    ov = np.maximum(hi - lo, 0) / CMP_BLOCK
    ov[:, N_CMP_PAD - 1] = 0.0
    return ov.astype(np.float32)


def _nsa_kernel(n_sb, rb_ref, q_ref, kc_ref, vc_ref, ks_ref, vs_ref, kw_ref, vw_ref, gate_ref, bg_ref,
                tiles_ref, cmpb_ref, ov_ref, kside_ref, out_ref,
                qs_ref, m_ref, acc_ref):
    g = pl.program_id(0)
    qi = pl.program_id(1)
    par = g % 2
    r_heads = HEADS_PER_GROUP
    rows = r_heads * TQ
    lane = lax.broadcasted_iota(jnp.int32, (TQ, LANES), 1)
    in_half = (lane // HEAD_DIM) == par

    for blk in range(2):
        qb = q_ref[:, blk * LANES:(blk + 1) * LANES].astype(F32)
        qr = pltpu.roll(qb, HEAD_DIM, 1)
        for hh in range(2):
            r = blk * 2 + hh
            src = jnp.where(par == hh, qb, qr)
            qs_ref[r * TQ:(r + 1) * TQ, 0:LANES] = (jnp.where(in_half, src, 0.0) * SCALE).astype(BF16)

    trow = qi * TQ + lax.broadcasted_iota(jnp.int32, (TQ, 1), 0)
    has_any = jnp.where(trow >= CMP_BLOCK - 1, 1.0, 0.0)
    kc = kc_ref[...]
    vc = vc_ref[...]
    psum = jnp.zeros((TQ, N_CMP_PAD), F32)
    o_cmp = []
    for r in range(r_heads):
        s = _dot_nt(qs_ref[r * TQ:(r + 1) * TQ, 0:LANES], kc) + cmpb_ref[r]
        e = jnp.exp(s - jnp.max(s, axis=-1, keepdims=True))
        p = e * (has_any / jnp.sum(e, axis=-1, keepdims=True))
        psum = psum + p
        o_cmp.append(_dot(p.astype(BF16), vc))

    p_hi = psum.astype(BF16)
    p_lo = (psum - p_hi.astype(F32)).astype(BF16)
    ov = ov_ref[...]
    imp = _dot_nt(ov, p_hi) + _dot_nt(ov, p_lo)
    jrow = lax.broadcasted_iota(jnp.int32, (n_sb, TQ), 0)
    tq = qi * TQ + lax.broadcasted_iota(jnp.int32, (n_sb, TQ), 1)
    cur = tq // SEL_BLOCK
    valid = jrow <= cur
    forced = jnp.where(jrow == 0, 1.0, jnp.where(jrow == cur, 1.0, jnp.where(jrow == cur - 1, 1.0, 0.0)))
    score = jnp.where(valid, imp + FORCE_BONUS * forced, -1.0)
    rank = jnp.zeros((n_sb, TQ), F32)
    for jp in range(n_sb):
        other = jnp.broadcast_to(score[jp:jp + 1, :], (n_sb, TQ))
        before = jnp.where(jrow > jp, 1.0, 0.0)
        rank = rank + jnp.where(other > score, 1.0, jnp.where(other == score, before, 0.0))
    chosen = jnp.where(valid, jnp.where(rank < float(min(N_SELECT, n_sb)), 0.0, NEG_INF), NEG_INF)

    srow = lax.broadcasted_iota(jnp.int32, (LANES - n_sb, TQ), 0) + n_sb
    side_t = jnp.concatenate([chosen, jnp.where(srow == SIDE_OFF, NEG_INF, 0.0)], axis=0)
    side = side_t.T
    for r in range(r_heads):
        far = jnp.full((TQ, LANES), rb_ref[N_BUCKETS - 1, r_heads * g + r], F32)
        far_hi = far.astype(BF16).astype(F32)
        side_r = jnp.where(lane == SIDE_FAR_HI, far_hi, jnp.where(lane == SIDE_FAR_LO, far - far_hi, side))
        qs_ref[r * TQ:(r + 1) * TQ, LANES:2 * LANES] = side_r.astype(BF16)

    def tile_rows(kj):
        return pl.ds(pl.multiple_of(kj * TK, TK), TK)

    ones = jnp.ones((TK, LANES), BF16)

    def scores(q, key_tiles, tile_types):
        keys = jnp.concatenate(key_tiles, axis=0)
        s = jnp.concatenate([_dot_nt(q[c:c + ROW_SPLIT], keys) for c in range(0, rows, ROW_SPLIT)], axis=0)
        out = []
        for t, kind in enumerate(tile_types):
            st = s[:, t * TK:(t + 1) * TK]
            if kind is not None:
                st = (st.reshape(r_heads, TQ, TK) + tiles_ref[kind]).reshape(rows, TK)
            out.append(st)
        return out

    def weighted(ps, v_ref, kjs):
        p = jnp.concatenate([x.astype(BF16) for x in ps], axis=1)
        v = jnp.concatenate([jnp.concatenate([v_ref[tile_rows(kj), :], ones], axis=1) for kj in kjs], axis=0)
        return jnp.concatenate([_dot(p[c:c + ROW_SPLIT], v) for c in range(0, rows, ROW_SPLIT)], axis=0)

    def row_max(ss):
        return jnp.max(functools.reduce(jnp.maximum, ss), axis=-1, keepdims=True)

    def normalised(acc):
        return acc[:, 0:LANES] * (1.0 / acc[:, LANES:2 * LANES])

    kj1 = jnp.maximum(qi - 1, 0)
    near_type = jnp.where(qi >= 1, TILE_NEAR, TILE_NONE)

    kj2 = jnp.maximum(qi - 2, 0)
    last_type = jnp.where(qi >= 2, TILE_WIN_LAST, TILE_NONE)
    sw = scores(qs_ref[:, 0:LANES], [kw_ref[tile_rows(kj), :] for kj in (qi, kj1, kj2)],
                (TILE_DIAG, near_type, last_type))
    mw = row_max(sw)
    o_win = normalised(weighted([jnp.exp(s - mw) for s in sw], vw_ref, (qi, kj1, kj2)))

    def key_side(kj, side_kind):
        return jnp.concatenate([ks_ref[tile_rows(kj), :], kside_ref[side_kind, tile_rows(kj), :]], axis=1)

    sn = scores(qs_ref[...], [key_side(qi, KSIDE_NEAR), key_side(kj1, KSIDE_NEAR)], (TILE_DIAG, near_type))
    m0 = row_max(sn)
    m_ref[...] = jnp.broadcast_to(m0, (rows, LANES))
    acc_ref[...] = weighted([jnp.exp(s - m0) for s in sn], vs_ref, (qi, kj1))

    def far_pair(idx, carry):
        ka = qi - 2 - 2 * idx
        kb_raw = ka - 1
        kb = jnp.maximum(kb_raw, 0)
        sf = scores(qs_ref[...], [key_side(ka, KSIDE_FAR),
                                  key_side(kb, jnp.where(kb_raw >= 0, KSIDE_FAR, KSIDE_OFF))], (None, None))
        m_prev = m_ref[...]
        m_new = jnp.maximum(m_prev, row_max(sf))
        alpha = jnp.exp(m_prev - m_new)
        m_wide = jnp.tile(m_new, (1, TK // LANES))
        acc_ref[...] = (jnp.tile(alpha, (1, 2)) * acc_ref[...]
                        + weighted([jnp.exp(s - m_wide) for s in sf], vs_ref, (ka, kb)))
        m_ref[...] = m_new
        return carry

    lax.fori_loop(0, jnp.maximum(qi, 1) // 2, far_pair, 0)
    o_slc = normalised(acc_ref[...])

    gs = jax.nn.sigmoid(gate_ref[...] + bg_ref[...])
    heads = []
    for r in range(r_heads):
        base = N_BRANCH * (r_heads * g + r)
        w = [jnp.sum(jnp.where(lane == base + br, gs, 0.0), axis=-1, keepdims=True) for br in range(N_BRANCH)]
        heads.append(w[0] * o_cmp[r] + w[1] * o_slc[r * TQ:(r + 1) * TQ, :] + w[2] * o_win[r * TQ:(r + 1) * TQ, :])
    for blk in range(2):
        lo, hi = heads[2 * blk], heads[2 * blk + 1]
        lo = jnp.where(par == 0, lo, pltpu.roll(lo, HEAD_DIM, 1))
        hi = jnp.where(par == 1, hi, pltpu.roll(hi, HEAD_DIM, 1))
        out_ref[:, blk * LANES:(blk + 1) * LANES] = jnp.where(lane < HEAD_DIM, lo, hi)


def _nsa_attn(rel_bias, att, rnn_in, kvc, bgate, tiles, cmpb, batch, seq_len):
    n = att.shape[0]
    nq = seq_len // TQ
    n_sb = seq_len // SEL_BLOCK
    ng = N_KV_GROUPS
    r = HEADS_PER_GROUP
    ov = _overlap_constants(seq_len)
    kside = _key_side_constants(seq_len)
    qw = r * HEAD_DIM
    kcol = ATTN_WIDTH + 2 * KV_WIDTH
    kvb = lambda off: (lambda g, qi, b: (b, (kcol + off * KV_WIDTH) // LANES + g // 2))
    return pl.pallas_call(
        functools.partial(_nsa_kernel, n_sb),
        grid=(ng, nq, batch),
        in_specs=[
            pl.BlockSpec(memory_space=pltpu.SMEM),
            pl.BlockSpec((TQ, qw), lambda g, qi, b: (b * nq + qi, g)),
            pl.BlockSpec((None, None, N_CMP_PAD, LANES), lambda g, qi, b: (0, b * ng + g, 0, 0)),
            pl.BlockSpec((None, None, N_CMP_PAD, LANES), lambda g, qi, b: (1, b * ng + g, 0, 0)),
            pl.BlockSpec((seq_len, LANES), kvb(0)),
            pl.BlockSpec((seq_len, LANES), kvb(1)),
            pl.BlockSpec((seq_len, LANES), kvb(2)),
            pl.BlockSpec((seq_len, LANES), kvb(3)),
            pl.BlockSpec((TQ, GATE_PAD), lambda g, qi, b: (b * nq + qi, 2 * RNN_WIDTH // GATE_PAD)),
            pl.BlockSpec((1, GATE_PAD), lambda g, qi, b: (0, 0)),
            pl.BlockSpec((None, N_TILE_TYPES, r, TQ, TK), lambda g, qi, b: (g, 0, 0, 0, 0)),
            pl.BlockSpec((None, r, TQ, N_CMP_PAD), lambda g, qi, b: (g, 0, qi, 0)),
            pl.BlockSpec((n_sb, N_CMP_PAD), lambda g, qi, b: (0, 0)),
            pl.BlockSpec((3, seq_len, LANES), lambda g, qi, b: (0, 0, 0)),
        ],
        out_specs=pl.BlockSpec((TQ, qw), lambda g, qi, b: (b * nq + qi, g)),
        out_shape=jax.ShapeDtypeStruct((n, ATTN_WIDTH), F32),
        scratch_shapes=[
            pltpu.VMEM((r * TQ, 2 * LANES), BF16),
            pltpu.VMEM((r * TQ, LANES), F32),
            pltpu.VMEM((r * TQ, 2 * LANES), F32),
        ],
        compiler_params=pltpu.CompilerParams(dimension_semantics=("arbitrary", "arbitrary", "arbitrary"),
                                             vmem_limit_bytes=VMEM_LIMIT),
        name="nsa_attn",
    )(rel_bias, att, kvc, kvc, att, att, att, att, rnn_in, bgate, tiles, cmpb,
      jnp.asarray(ov, BF16), jnp.asarray(kside, BF16))


def kernel(x, mix_norm_g, w_in, b_gate, cmp_pe_k, cmp_pe_v, cmp_k_w1, cmp_k_w2, cmp_v_w1, cmp_v_w2, rel_bias, rnn_conv_w, rnn_conv_b, rg_a_w, rg_a_b, rg_x_w, rg_x_b, rg_lambda, attn_out_g, rnn_out_g, w_out, ffn_norm_g, w_ffn_gate, w_ffn_up, ffn_conv_w, ffn_conv_b, w_ffn_down, final_norm_g):
    batch, seq_len, d_model = x.shape
    assert d_model == D_MODEL and w_in.shape[0] == 1, "single-layer model of the stated width"
    assert seq_len // CMP_STRIDE == N_CMP_PAD and seq_len % TM_FFN == 0
    n = batch * seq_len
    ng, hd = N_KV_GROUPS, HEAD_DIM
    x2 = x.reshape(n, D_MODEL)

    w0 = w_in[0]
    gate0 = ATT_COLS
    rx0 = gate0 + N_BRANCH * N_Q_HEADS
    w_perm = jnp.concatenate(
        [w0[:, :ATT_COLS], w0[:, rx0:rx0 + 2 * RNN_WIDTH], w0[:, gate0:rx0],
         jnp.zeros((D_MODEL, GATE_PAD - N_BRANCH * N_Q_HEADS), w0.dtype)], axis=1).astype(BF16)
    att, rnn_in = _in_proj(x2, mix_norm_g[0][None], w_perm)

    nch = seq_len // CMP_STRIDE
    chunks = att[:, ATTN_WIDTH:ATTN_WIDTH + 2 * KV_WIDTH].reshape(batch, nch, CMP_STRIDE, 2, ng, hd)
    chunks = chunks.transpose(3, 0, 4, 1, 2, 5).reshape(2, batch * ng * nch, CHUNK_W)
    pe = jnp.stack([cmp_pe_k[0].reshape(1, -1), cmp_pe_v[0].reshape(1, -1)]).astype(BF16)
    w1 = jnp.stack([cmp_k_w1[0], cmp_v_w1[0]]).astype(BF16)
    w2 = jnp.stack([jnp.tile(cmp_k_w2[0], (1, 2)), jnp.tile(cmp_v_w2[0], (1, 2))]).astype(BF16)
    kvc = _cmp_tokens(chunks, pe, w1, w2).reshape(2, batch * ng, N_CMP_PAD, LANES)

    tiles, cmpb = _bias_tiles(rel_bias, seq_len)
    bgate = jnp.pad(b_gate[0], (0, GATE_PAD - b_gate.shape[1]))[None]
    o_attn = _nsa_attn(rel_bias, att, rnn_in, kvc, bgate, tiles, cmpb, batch, seq_len)

    nr = _rglru(rnn_in, rnn_conv_w[0], rnn_conv_b[0][None],
                _block_diag(rg_a_w[0]).astype(BF16), rg_a_b[0][None],
                _block_diag(rg_x_w[0]).astype(BF16), rg_x_b[0][None],
                rg_lambda[0][None], rnn_out_g[0][None], batch, seq_len)

    h, y = _out_proj(o_attn, nr, x2, attn_out_g[0][None], ffn_norm_g[0][None], w_out[0].astype(BF16))
    out = _conv_ffn(y, h, w_ffn_gate[0].astype(BF16), w_ffn_up[0].astype(BF16), w_ffn_down[0].astype(BF16),
                    ffn_conv_w[0], ffn_conv_b[0][None], final_norm_g[None], seq_len)
    return out.reshape(batch, seq_len, D_MODEL)
```

```python
import functools
import math

import jax
import jax.numpy as jnp
import numpy as np
from jax import lax
from jax.experimental import pallas as pl
from jax.experimental.pallas import tpu as pltpu

F32 = jnp.float32
BF16 = jnp.bfloat16

D_MODEL = 2048
N_Q_HEADS = 16
N_KV_GROUPS = 4
HEADS_PER_GROUP = 4
HEAD_DIM = 64
ATTN_WIDTH = 1024
KV_WIDTH = 256
SCALE = HEAD_DIM ** -0.5
CMP_BLOCK = 32
CMP_STRIDE = 16
CMP_HIDDEN = 256
SEL_BLOCK = 64
N_SELECT = 16
WINDOW = 512
N_BRANCH = 3
RNN_WIDTH = 1024
RNN_BLOCKS = 16
RNN_CONV_WIDTH = 4
RG_LRU_C = 8.0
N_BUCKETS = 32
MAX_DISTANCE = 128
D_FF = 5632
NORM_EPS = 1e-6
NEG_INF = -1e30
FORCE_BONUS = 1e3
GELU_C0 = math.sqrt(2.0 / math.pi)
GELU_C1 = GELU_C0 * 0.044715

LANES = 128
SUBLANES = 8
VMEM_LIMIT = 56 * 1024 * 1024

TM_PROJ = 512
COL_CHUNK = 512
TM_FFN = 1024
TF_FFN = 512
FFN_ROW_BLOCKS = 2
TS_RNN = 256
CB_RNN = 256
TQ = 256
TK = 256
ROW_SPLIT = 512
N_CMP_PAD = 128
GATE_PAD = 128

ATT_COLS = ATTN_WIDTH + 6 * KV_WIDTH
RNN_COLS = 2 * RNN_WIDTH + GATE_PAD


def _rms(x, g):
    return x * lax.rsqrt(jnp.mean(x * x, axis=-1, keepdims=True) + NORM_EPS) * g


def _gelu(x):
    return jax.nn.gelu(x, approximate=True)


def _dot(a, b):
    return jnp.dot(a, b, preferred_element_type=F32)


def _dot_nt(a, b):
    return lax.dot_general(a, b, (((1,), (1,)), ((), ())), preferred_element_type=F32)


def _dot_tn(a, b):
    return lax.dot_general(a, b, (((0,), (0,)), ((), ())), preferred_element_type=F32)


def _in_proj_kernel(x_ref, g_ref, w_ref, att_ref, rnn_ref):
    xn = _rms(x_ref[...], g_ref[...]).astype(BF16)
    for c0 in range(0, ATT_COLS, COL_CHUNK):
        att_ref[:, c0:c0 + COL_CHUNK] = _dot(xn, w_ref[:, c0:c0 + COL_CHUNK]).astype(BF16)
    for c0 in range(0, RNN_COLS, COL_CHUNK):
        c1 = min(c0 + COL_CHUNK, RNN_COLS)
        rnn_ref[:, c0:c1] = _dot(xn, w_ref[:, ATT_COLS + c0:ATT_COLS + c1])


def _in_proj(x2, g, w):
    n = x2.shape[0]
    return pl.pallas_call(
        _in_proj_kernel,
        grid=(n // TM_PROJ,),
        in_specs=[
            pl.BlockSpec((TM_PROJ, D_MODEL), lambda i: (i, 0)),
            pl.BlockSpec((1, D_MODEL), lambda i: (0, 0)),
            pl.BlockSpec((D_MODEL, ATT_COLS + RNN_COLS), lambda i: (0, 0), pipeline_mode=pl.Buffered(1)),
        ],
        out_specs=[
            pl.BlockSpec((TM_PROJ, ATT_COLS), lambda i: (i, 0)),
            pl.BlockSpec((TM_PROJ, RNN_COLS), lambda i: (i, 0)),
        ],
        out_shape=[
            jax.ShapeDtypeStruct((n, ATT_COLS), BF16),
            jax.ShapeDtypeStruct((n, RNN_COLS), F32),
        ],
        compiler_params=pltpu.CompilerParams(dimension_semantics=("parallel",), vmem_limit_bytes=VMEM_LIMIT),
        name="in_proj",
    )(x2, g, w)


def _out_proj_kernel(oa_ref, nr_ref, x_ref, ga_ref, gf_ref, w_ref, h_ref, y_ref):
    na = _rms(oa_ref[...], ga_ref[...]).astype(BF16)
    nr = nr_ref[...]
    for c0 in range(0, D_MODEL, COL_CHUNK):
        sl = slice(c0, c0 + COL_CHUNK)
        h_ref[:, sl] = (x_ref[:, sl] + _dot(na, w_ref[0:ATTN_WIDTH, sl])
                        + _dot(nr, w_ref[ATTN_WIDTH:D_MODEL, sl]))
    y_ref[...] = _rms(h_ref[...], gf_ref[...]).astype(BF16)


def _out_proj(o_attn, nr, x2, ga, gf, w):
    n = x2.shape[0]
    return pl.pallas_call(
        _out_proj_kernel,
        grid=(n // TM_PROJ,),
        in_specs=[
            pl.BlockSpec((TM_PROJ, ATTN_WIDTH), lambda i: (i, 0)),
            pl.BlockSpec((TM_PROJ, RNN_WIDTH), lambda i: (i, 0)),
            pl.BlockSpec((TM_PROJ, D_MODEL), lambda i: (i, 0)),
            pl.BlockSpec((1, ATTN_WIDTH), lambda i: (0, 0)),
            pl.BlockSpec((1, D_MODEL), lambda i: (0, 0)),
            pl.BlockSpec((D_MODEL, D_MODEL), lambda i: (0, 0), pipeline_mode=pl.Buffered(1)),
        ],
        out_specs=[
            pl.BlockSpec((TM_PROJ, D_MODEL), lambda i: (i, 0)),
            pl.BlockSpec((TM_PROJ, D_MODEL), lambda i: (i, 0)),
        ],
        out_shape=[
            jax.ShapeDtypeStruct((n, D_MODEL), F32),
            jax.ShapeDtypeStruct((n, D_MODEL), BF16),
        ],
        compiler_params=pltpu.CompilerParams(dimension_semantics=("parallel",), vmem_limit_bytes=VMEM_LIMIT),
        name="out_proj",
    )(o_attn, nr, x2, ga, gf, w)


def _ffn_kernel(tiles_per_seq, y_ref, h_ref, wg_ref, wu_ref, wd_ref, cw_ref, cb_ref, gf_ref, out_ref, carry_ref):
    i = pl.program_id(0)
    j = pl.program_id(1)
    nj = pl.num_programs(1)

    @pl.when((i % tiles_per_seq) == 0)
    def _():
        carry_ref[j] = jnp.zeros((SUBLANES, TF_FFN), F32)

    @pl.when(j == 0)
    def _():
        out_ref[...] = h_ref[...]

    row = lax.broadcasted_iota(jnp.int32, (SUBLANES, TF_FFN), 0)
    cw = cw_ref[...]
    cb = cb_ref[...]
    prev = carry_ref[j]
    rows_blk = TM_FFN // FFN_ROW_BLOCKS
    for blk in range(FFN_ROW_BLOCKS):
        sl = slice(blk * rows_blk, (blk + 1) * rows_blk)
        y = y_ref[sl, :]
        gt = _dot(y, wg_ref[...])
        p1 = prev[SUBLANES - 1:SUBLANES, :]
        p2 = prev[SUBLANES - 2:SUBLANES - 1, :]
        r1 = pltpu.roll(gt, 1, 0)
        r2 = pltpu.roll(gt, 2, 0)
        h1 = jnp.where(row == 0, p1, r1[0:SUBLANES])
        h2 = jnp.where(row == 0, p2, jnp.where(row == 1, p1, r2[0:SUBLANES]))
        s1 = jnp.concatenate([h1, r1[SUBLANES:]], axis=0)
        s2 = jnp.concatenate([h2, r2[SUBLANES:]], axis=0)
        conv = cw[2:3, :] * gt + cw[1:2, :] * s1 + cw[0:1, :] * s2 + cb
        inner = conv * (GELU_C0 + GELU_C1 * (conv * conv))
        hx = 0.5 * conv
        u = hx + hx * jnp.tanh(inner)
        z = (u * _dot(y, wu_ref[...])).astype(BF16)
        out_ref[sl, :] += _dot(z, wd_ref[...])
        prev = gt[rows_blk - SUBLANES:rows_blk, :]
    carry_ref[j] = prev

    @pl.when(j == nj - 1)
    def _():
        out_ref[...] = _rms(out_ref[...], gf_ref[...])


def _conv_ffn(y, h, wg, wu, wd, cw, cb, gf, seq_len):
    n = y.shape[0]
    nj = D_FF // TF_FFN
    return pl.pallas_call(
        functools.partial(_ffn_kernel, seq_len // TM_FFN),
        grid=(n // TM_FFN, nj),
        in_specs=[
            pl.BlockSpec((TM_FFN, D_MODEL), lambda i, j: (i, 0)),
            pl.BlockSpec((TM_FFN, D_MODEL), lambda i, j: (i, 0), pipeline_mode=pl.Buffered(1)),
            pl.BlockSpec((D_MODEL, TF_FFN), lambda i, j: (0, j)),
            pl.BlockSpec((D_MODEL, TF_FFN), lambda i, j: (0, j)),
            pl.BlockSpec((TF_FFN, D_MODEL), lambda i, j: (j, 0)),
            pl.BlockSpec((3, TF_FFN), lambda i, j: (0, j)),
            pl.BlockSpec((1, TF_FFN), lambda i, j: (0, j)),
            pl.BlockSpec((1, D_MODEL), lambda i, j: (0, 0)),
        ],
        out_specs=pl.BlockSpec((TM_FFN, D_MODEL), lambda i, j: (i, 0)),
        out_shape=jax.ShapeDtypeStruct((n, D_MODEL), F32),
        scratch_shapes=[pltpu.VMEM((nj, SUBLANES, TF_FFN), F32)],
        compiler_params=pltpu.CompilerParams(dimension_semantics=("arbitrary", "arbitrary"),
                                             vmem_limit_bytes=VMEM_LIMIT),
        name="conv_ffn",
    )(y, h, wg, wu, wd, cw, cb, gf)


def _rglru_kernel(rx_ref, ry_ref, cw_ref, cb_ref, wa_ref, ba_ref, wx_ref, bx_ref, lam_ref, g_ref,
                  out_ref, tail_ref, hc_ref, a_ref, u_ref, o_ref):
    c = pl.program_id(1)

    @pl.when(c == 0)
    def _():
        tail_ref[...] = jnp.zeros(tail_ref.shape, F32)
        hc_ref[...] = jnp.zeros(hc_ref.shape, F32)

    row8 = lax.broadcasted_iota(jnp.int32, (SUBLANES, CB_RNN), 0)
    rmod = lax.broadcasted_iota(jnp.int32, (TS_RNN // SUBLANES, SUBLANES, CB_RNN), 1)
    lam = lam_ref[...]
    nlam = -lam
    softplus = jnp.maximum(nlam, 0.0) + jnp.log1p(jnp.exp(-jnp.abs(nlam)))
    log_a_unit = -RG_LRU_C * softplus

    for cb in range(RNN_WIDTH // CB_RNN):
        sl = slice(cb * CB_RNN, (cb + 1) * CB_RNN)
        rx = rx_ref[:, sl]
        prev = tail_ref[:, sl]
        tail_ref[:, sl] = rx[TS_RNN - SUBLANES:TS_RNN, :]
        cw = cw_ref[:, sl]
        xr = cw[3:4, :] * rx + cb_ref[:, sl]
        for d in (1, 2, 3):
            sh = pltpu.roll(rx, d, 0)
            head = sh[0:SUBLANES]
            for k in range(d):
                head = jnp.where(row8 == k, prev[SUBLANES - d + k:SUBLANES - d + k + 1, :], head)
            xr = xr + cw[3 - d:4 - d, :] * jnp.concatenate([head, sh[SUBLANES:]], axis=0)
        xb = xr.astype(BF16)
        r = jax.nn.sigmoid(_dot(xb, wa_ref[cb]) + ba_ref[:, sl])
        gi = jax.nn.sigmoid(_dot(xb, wx_ref[cb]) + bx_ref[:, sl])
        log_a = log_a_unit[:, sl] * r
        a = jnp.exp(log_a)
        th = jnp.tanh(log_a)
        u = jnp.sqrt(-2.0 * th / (1.0 - th)) * (gi * xr)
        a = a.reshape(TS_RNN // SUBLANES, SUBLANES, CB_RNN)
        u = u.reshape(TS_RNN // SUBLANES, SUBLANES, CB_RNN)
        for d in (1, 2, 4):
            keep = rmod >= d
            a_s = pltpu.roll(a, d, 1)
            u_s = pltpu.roll(u, d, 1)
            u = jnp.where(keep, a * u_s + u, u)
            a = jnp.where(keep, a * a_s, a)
        a_ref[...] = a.reshape(TS_RNN, CB_RNN)
        u_ref[...] = u.reshape(TS_RNN, CB_RNN)

        def group(k, hprev):
            r0 = pl.multiple_of(k * SUBLANES, SUBLANES)
            h8 = u_ref[pl.ds(r0, SUBLANES), :] + a_ref[pl.ds(r0, SUBLANES), :] * hprev
            u_ref[pl.ds(r0, SUBLANES), :] = h8
            return jnp.broadcast_to(h8[SUBLANES - 1:SUBLANES, :], (SUBLANES, CB_RNN))

        hlast = lax.fori_loop(0, TS_RNN // SUBLANES, group, hc_ref[:, sl], unroll=4)
        hc_ref[:, sl] = hlast
        o_ref[:, sl] = u_ref[...] * _gelu(ry_ref[:, sl])

    out_ref[...] = _rms(o_ref[...], g_ref[...]).astype(BF16)


def _rglru(rnn_in, cw, cb, wa, ba, wx, bx, lam, g, batch, seq_len):
    n = rnn_in.shape[0]
    nc = seq_len // TS_RNN
    nb = RNN_WIDTH // CB_RNN
    vec = pl.BlockSpec((1, RNN_WIDTH), lambda b, c: (0, 0))
    wspec = pl.BlockSpec((nb, CB_RNN, CB_RNN), lambda b, c: (0, 0, 0))
    return pl.pallas_call(
        _rglru_kernel,
        grid=(batch, nc),
        in_specs=[
            pl.BlockSpec((TS_RNN, RNN_WIDTH), lambda b, c: (b * nc + c, 0)),
            pl.BlockSpec((TS_RNN, RNN_WIDTH), lambda b, c: (b * nc + c, 1)),
            pl.BlockSpec((RNN_CONV_WIDTH, RNN_WIDTH), lambda b, c: (0, 0)),
            vec, wspec, vec, wspec, vec, vec, vec,
        ],
        out_specs=pl.BlockSpec((TS_RNN, RNN_WIDTH), lambda b, c: (b * nc + c, 0)),
        out_shape=jax.ShapeDtypeStruct((n, RNN_WIDTH), BF16),
        scratch_shapes=[
            pltpu.VMEM((SUBLANES, RNN_WIDTH), F32),
            pltpu.VMEM((SUBLANES, RNN_WIDTH), F32),
            pltpu.VMEM((TS_RNN, CB_RNN), F32),
            pltpu.VMEM((TS_RNN, CB_RNN), F32),
            pltpu.VMEM((TS_RNN, RNN_WIDTH), F32),
        ],
        compiler_params=pltpu.CompilerParams(dimension_semantics=("arbitrary", "arbitrary"),
                                             vmem_limit_bytes=VMEM_LIMIT),
        name="rglru",
    )(rnn_in, rnn_in, cw, cb, wa, ba, wx, bx, lam, g)


def _block_diag(w):
    per = CB_RNN // (RNN_WIDTH // RNN_BLOCKS)
    bd = w.shape[-1]
    w4 = w.reshape(RNN_BLOCKS // per, per, bd, bd)
    eye = jnp.eye(per, dtype=w.dtype)
    return jnp.einsum('cpij,pq->cpiqj', w4, eye).reshape(RNN_BLOCKS // per, per * bd, per * bd)


CMP_ROWS = 1024
CHUNK_W = CMP_STRIDE * HEAD_DIM


def _cmp_tokens_kernel(x_ref, pe_ref, w1_ref, w2_ref, out_ref):
    x = x_ref[...]
    ya = _dot(x, w1_ref[0:CHUNK_W, :])
    yb = _dot(x, w1_ref[CHUNK_W:2 * CHUNK_W, :])
    pe = jnp.broadcast_to(pe_ref[...], (SUBLANES, 2 * CHUNK_W))
    pterm = _dot(pe, w1_ref[...])[0:1, :]
    hid = ya + pltpu.roll(yb, x.shape[0] - 1, 0) + pterm
    out_ref[...] = _dot(_gelu(hid).astype(BF16), w2_ref[...]).astype(BF16)


def _cmp_tokens(xkv, pe, w1, w2):
    rows = xkv.shape[1]
    tr = min(CMP_ROWS, rows)
    return pl.pallas_call(
        _cmp_tokens_kernel,
        grid=(2, rows // tr),
        in_specs=[
            pl.BlockSpec((None, tr, CHUNK_W), lambda s, i: (s, i, 0)),
            pl.BlockSpec((None, 1, 2 * CHUNK_W), lambda s, i: (s, 0, 0)),
            pl.BlockSpec((None, 2 * CHUNK_W, CMP_HIDDEN), lambda s, i: (s, 0, 0)),
            pl.BlockSpec((None, CMP_HIDDEN, LANES), lambda s, i: (s, 0, 0)),
        ],
        out_specs=pl.BlockSpec((None, tr, LANES), lambda s, i: (s, i, 0)),
        out_shape=jax.ShapeDtypeStruct((2, rows, LANES), BF16),
        compiler_params=pltpu.CompilerParams(dimension_semantics=("parallel", "parallel")),
        name="cmp_tokens",
    )(xkv, pe, w1, w2)


N_TILE_TYPES = 4
TILE_DIAG, TILE_NEAR, TILE_WIN_LAST, TILE_NONE = 0, 1, 2, 3


def _t5_bucket_np(dist):
    n = np.maximum(dist, 0)
    max_exact = N_BUCKETS // 2
    nf = np.maximum(n, 1).astype(np.float64)
    large = max_exact + (np.log(nf / max_exact) / math.log(MAX_DISTANCE / max_exact)
                         * (N_BUCKETS - max_exact)).astype(np.int32)
    large = np.minimum(large, N_BUCKETS - 1)
    return np.where(n < max_exact, n, large).astype(np.int32)


def _bucket_constants(seq_len):
    i = np.arange(TQ)[:, None]
    j = np.arange(TK)[None, :]
    d0 = i - j
    d1 = TK + i - j
    d2 = 2 * TK + i - j
    valid = [d0 >= 0, np.ones_like(d1, bool), d2 < WINDOW, np.zeros_like(d0, bool)]
    tiles = np.stack([np.where(v, _t5_bucket_np(d), -1) for d, v in zip([d0, d1, d2, d0], valid)])
    t = np.arange(seq_len)[:, None]
    c = np.arange(N_CMP_PAD)[None, :]
    dc = t - (c * CMP_STRIDE + CMP_BLOCK - 1)
    cmp = np.where(dc >= 0, _t5_bucket_np(dc), -1)
    return tiles.astype(np.int32), cmp.astype(np.int32)


def _bias_kernel(rb_ref, tb_ref, cbk_ref, tiles_ref, cmpb_ref):
    h = pl.program_id(0)

    def lookup(bkt):
        acc = jnp.full(bkt.shape, NEG_INF, F32)
        for b in range(N_BUCKETS):
            acc = jnp.where(bkt == b, rb_ref[b, h], acc)
        return acc

    for t in range(N_TILE_TYPES):
        tiles_ref[t] = lookup(tb_ref[t])
    cmpb_ref[...] = lookup(cbk_ref[...])


def _bias_tiles(rel_bias, seq_len):
    tb, cbk = _bucket_constants(seq_len)
    r = HEADS_PER_GROUP
    return pl.pallas_call(
        _bias_kernel,
        grid=(N_Q_HEADS,),
        in_specs=[
            pl.BlockSpec(memory_space=pltpu.SMEM),
            pl.BlockSpec((N_TILE_TYPES, TQ, TK), lambda h: (0, 0, 0)),
            pl.BlockSpec((seq_len, N_CMP_PAD), lambda h: (0, 0)),
        ],
        out_specs=[
            pl.BlockSpec((None, N_TILE_TYPES, None, TQ, TK), lambda h: (h // r, 0, h % r, 0, 0)),
            pl.BlockSpec((None, None, seq_len, N_CMP_PAD), lambda h: (h // r, h % r, 0, 0)),
        ],
        out_shape=[
            jax.ShapeDtypeStruct((N_KV_GROUPS, N_TILE_TYPES, r, TQ, TK), F32),
            jax.ShapeDtypeStruct((N_KV_GROUPS, r, seq_len, N_CMP_PAD), F32),
        ],
        compiler_params=pltpu.CompilerParams(dimension_semantics=("parallel",)),
        name="bias_tiles",
    )(rel_bias, jnp.asarray(tb), jnp.asarray(cbk))


SIDE_FAR_HI, SIDE_FAR_LO, SIDE_OFF = 32, 33, 34
KSIDE_NEAR, KSIDE_FAR, KSIDE_OFF = 0, 1, 2


def _key_side_constants(seq_len):
    s = np.arange(seq_len)
    side = np.zeros((3, seq_len, LANES), np.float32)
    side[:, s, s // SEL_BLOCK] = 1.0
    side[KSIDE_FAR, :, SIDE_FAR_HI] = 1.0
    side[KSIDE_FAR, :, SIDE_FAR_LO] = 1.0
    side[KSIDE_OFF, :, SIDE_OFF] = 1.0
    return side


def _overlap_constants(seq_len):
    n_sb = seq_len // SEL_BLOCK
    c = np.arange(N_CMP_PAD)[None, :]
    j = np.arange(n_sb)[:, None]
    lo = np.maximum(c * CMP_STRIDE, j * SEL_BLOCK)
    hi = np.minimum(c * CMP_STRIDE + CMP_BLOCK, (j + 1) * SEL_BLOCK)
    ov = np.maximum(hi - lo, 0) / CMP_BLOCK
    ov[:, N_CMP_PAD - 1] = 0.0
    return ov.astype(np.float32)


def _nsa_kernel(n_sb, rb_ref, q_ref, kc_ref, vc_ref, ks_ref, vs_ref, kw_ref, vw_ref, gate_ref, bg_ref,
                tiles_ref, cmpb_ref, ov_ref, kside_ref, out_ref,
                qs_ref, side_ref, m_ref, acc_ref):
    g = pl.program_id(0)
    qi = pl.program_id(1)
    par = g % 2
    r_heads = HEADS_PER_GROUP
    rows = r_heads * TQ
    lane = lax.broadcasted_iota(jnp.int32, (TQ, LANES), 1)
    in_half = (lane // HEAD_DIM) == par

    for blk in range(2):
        qb = q_ref[:, blk * LANES:(blk + 1) * LANES].astype(F32)
        qr = pltpu.roll(qb, HEAD_DIM, 1)
        for hh in range(2):
            r = blk * 2 + hh
            src = jnp.where(par == hh, qb, qr)
            qs_ref[r * TQ:(r + 1) * TQ, :] = (jnp.where(in_half, src, 0.0) * SCALE).astype(BF16)

    gs = jax.nn.sigmoid(gate_ref[...] + bg_ref[...])
    gates = [[jnp.sum(jnp.where(lane == N_BRANCH * (r_heads * g + r) + br, gs, 0.0), axis=-1, keepdims=True)
              for br in range(N_BRANCH)] for r in range(r_heads)]

    def tile_rows(kj):
        return pl.ds(pl.multiple_of(kj * TK, TK), TK)

    ones = jnp.ones((TK, LANES), BF16)

    def scores(q, key_tiles, tile_types):
        keys = jnp.concatenate(key_tiles, axis=0)
        s = jnp.concatenate([_dot_nt(q[c:c + ROW_SPLIT], keys) for c in range(0, rows, ROW_SPLIT)], axis=0)
        out = []
        for t, kind in enumerate(tile_types):
            st = s[:, t * TK:(t + 1) * TK]
            if kind is not None:
                st = (st.reshape(r_heads, TQ, TK) + tiles_ref[kind]).reshape(rows, TK)
            out.append(st)
        return out

    def weighted(ps, v_ref, kjs):
        p = jnp.concatenate([x.astype(BF16) for x in ps], axis=1)
        v = jnp.concatenate([jnp.concatenate([v_ref[tile_rows(kj), :], ones], axis=1) for kj in kjs], axis=0)
        return jnp.concatenate([_dot(p[c:c + ROW_SPLIT], v) for c in range(0, rows, ROW_SPLIT)], axis=0)

    def row_max(ss):
        return jnp.max(functools.reduce(jnp.maximum, ss), axis=-1, keepdims=True)

    def normalised(acc):
        return acc[:, 0:LANES] * (1.0 / acc[:, LANES:2 * LANES])

    kj1 = jnp.maximum(qi - 1, 0)
    near_type = jnp.where(qi >= 1, TILE_NEAR, TILE_NONE)

    kj2 = jnp.maximum(qi - 2, 0)
    last_type = jnp.where(qi >= 2, TILE_WIN_LAST, TILE_NONE)
    sw = scores(qs_ref[...], [kw_ref[tile_rows(kj), :] for kj in (qi, kj1, kj2)],
                (TILE_DIAG, near_type, last_type))
    mw = row_max(sw)
    o_win = normalised(weighted([jnp.exp(s - mw) for s in sw], vw_ref, (qi, kj1, kj2)))

    trow = qi * TQ + lax.broadcasted_iota(jnp.int32, (TQ, 1), 0)
    has_any = jnp.where(trow >= CMP_BLOCK - 1, 1.0, 0.0)
    kc = kc_ref[...]
    vc = vc_ref[...]
    psum = jnp.zeros((TQ, N_CMP_PAD), F32)
    o_cmp = []
    for r in range(r_heads):
        s = _dot_nt(qs_ref[r * TQ:(r + 1) * TQ, :], kc) + cmpb_ref[r]
        e = jnp.exp(s - jnp.max(s, axis=-1, keepdims=True))
        p = e * (has_any / jnp.sum(e, axis=-1, keepdims=True))
        psum = psum + p
        o_cmp.append(_dot(p.astype(BF16), vc))

    p_hi = psum.astype(BF16)
    p_lo = (psum - p_hi.astype(F32)).astype(BF16)
    ov = ov_ref[...]
    imp = _dot_nt(ov, p_hi) + _dot_nt(ov, p_lo)
    jrow = lax.broadcasted_iota(jnp.int32, (n_sb, TQ), 0)
    tq = qi * TQ + lax.broadcasted_iota(jnp.int32, (n_sb, TQ), 1)
    cur = tq // SEL_BLOCK
    valid = jrow <= cur
    forced = jnp.where(jrow == 0, 1.0, jnp.where(jrow == cur, 1.0, jnp.where(jrow == cur - 1, 1.0, 0.0)))
    score = jnp.where(valid, imp + FORCE_BONUS * forced, -1.0)
    rank = jnp.zeros((n_sb, TQ), F32)
    for jp in range(n_sb):
        other = jnp.broadcast_to(score[jp:jp + 1, :], (n_sb, TQ))
        before = jnp.where(jrow > jp, 1.0, 0.0)
        rank = rank + jnp.where(other > score, 1.0, jnp.where(other == score, before, 0.0))
    chosen = jnp.where(valid, jnp.where(rank < float(min(N_SELECT, n_sb)), 0.0, NEG_INF), NEG_INF)

    srow = lax.broadcasted_iota(jnp.int32, (LANES - n_sb, TQ), 0) + n_sb
    side_t = jnp.concatenate([chosen, jnp.where(srow == SIDE_OFF, NEG_INF, 0.0)], axis=0)
    side = side_t.T
    for r in range(r_heads):
        far = jnp.full((TQ, LANES), rb_ref[N_BUCKETS - 1, r_heads * g + r], F32)
        far_hi = far.astype(BF16).astype(F32)
        side_r = jnp.where(lane == SIDE_FAR_HI, far_hi, jnp.where(lane == SIDE_FAR_LO, far - far_hi, side))
        side_ref[r * TQ:(r + 1) * TQ, :] = side_r.astype(BF16)

    def key_side(kj, side_kind):
        return jnp.concatenate([ks_ref[tile_rows(kj), :], kside_ref[side_kind, tile_rows(kj), :]], axis=1)

    def q_side():
        return jnp.concatenate([qs_ref[...], side_ref[...]], axis=1)

    sn = scores(q_side(), [key_side(qi, KSIDE_NEAR), key_side(kj1, KSIDE_NEAR)], (TILE_DIAG, near_type))
    m0 = row_max(sn)
    m_ref[...] = jnp.broadcast_to(m0, (rows, LANES))
    acc_ref[...] = weighted([jnp.exp(s - m0) for s in sn], vs_ref, (qi, kj1))

    def far_pair(idx, carry):
        ka = qi - 2 - 2 * idx
        kb_raw = ka - 1
        kb = jnp.maximum(kb_raw, 0)
        sf = scores(q_side(), [key_side(ka, KSIDE_FAR),
                               key_side(kb, jnp.where(kb_raw >= 0, KSIDE_FAR, KSIDE_OFF))], (None, None))
        m_prev = m_ref[...]
        m_new = jnp.maximum(m_prev, row_max(sf))
        alpha = jnp.exp(m_prev - m_new)
        m_wide = jnp.tile(m_new, (1, TK // LANES))
        acc_ref[...] = (jnp.tile(alpha, (1, 2)) * acc_ref[...]
                        + weighted([jnp.exp(s - m_wide) for s in sf], vs_ref, (ka, kb)))
        m_ref[...] = m_new
        return carry

    lax.fori_loop(0, jnp.maximum(qi, 1) // 2, far_pair, 0)
    o_slc = normalised(acc_ref[...])

    heads = []
    for r in range(r_heads):
        w = gates[r]
        heads.append(w[0] * o_cmp[r] + w[1] * o_slc[r * TQ:(r + 1) * TQ, :] + w[2] * o_win[r * TQ:(r + 1) * TQ, :])
    for blk in range(2):
        lo, hi = heads[2 * blk], heads[2 * blk + 1]
        lo = jnp.where(par == 0, lo, pltpu.roll(lo, HEAD_DIM, 1))
        hi = jnp.where(par == 1, hi, pltpu.roll(hi, HEAD_DIM, 1))
        out_ref[:, blk * LANES:(blk + 1) * LANES] = jnp.where(lane < HEAD_DIM, lo, hi)


def _nsa_attn(rel_bias, att, rnn_in, kvc, bgate, tiles, cmpb, batch, seq_len):
    n = att.shape[0]
    nq = seq_len // TQ
    n_sb = seq_len // SEL_BLOCK
    ng = N_KV_GROUPS
    r = HEADS_PER_GROUP
    ov = _overlap_constants(seq_len)
    kside = _key_side_constants(seq_len)
    qw = r * HEAD_DIM
    kcol = ATTN_WIDTH + 2 * KV_WIDTH
    kvb = lambda off: (lambda g, qi, b: (b, (kcol + off * KV_WIDTH) // LANES + g // 2))
    return pl.pallas_call(
        functools.partial(_nsa_kernel, n_sb),
        grid=(ng, nq, batch),
        in_specs=[
            pl.BlockSpec(memory_space=pltpu.SMEM),
            pl.BlockSpec((TQ, qw), lambda g, qi, b: (b * nq + qi, g)),
            pl.BlockSpec((None, None, N_CMP_PAD, LANES), lambda g, qi, b: (0, b * ng + g, 0, 0)),
            pl.BlockSpec((None, None, N_CMP_PAD, LANES), lambda g, qi, b: (1, b * ng + g, 0, 0)),
            pl.BlockSpec((seq_len, LANES), kvb(0)),
            pl.BlockSpec((seq_len, LANES), kvb(1)),
            pl.BlockSpec((seq_len, LANES), kvb(2)),
            pl.BlockSpec((seq_len, LANES), kvb(3)),
            pl.BlockSpec((TQ, GATE_PAD), lambda g, qi, b: (b * nq + qi, 2 * RNN_WIDTH // GATE_PAD)),
            pl.BlockSpec((1, GATE_PAD), lambda g, qi, b: (0, 0)),
            pl.BlockSpec((None, N_TILE_TYPES, r, TQ, TK), lambda g, qi, b: (g, 0, 0, 0, 0)),
            pl.BlockSpec((None, r, TQ, N_CMP_PAD), lambda g, qi, b: (g, 0, qi, 0)),
            pl.BlockSpec((n_sb, N_CMP_PAD), lambda g, qi, b: (0, 0)),
            pl.BlockSpec((3, seq_len, LANES), lambda g, qi, b: (0, 0, 0)),
        ],
        out_specs=pl.BlockSpec((TQ, qw), lambda g, qi, b: (b * nq + qi, g)),
        out_shape=jax.ShapeDtypeStruct((n, ATTN_WIDTH), F32),
        scratch_shapes=[
            pltpu.VMEM((r * TQ, LANES), BF16),
            pltpu.VMEM((r * TQ, LANES), BF16),
            pltpu.VMEM((r * TQ, LANES), F32),
            pltpu.VMEM((r * TQ, 2 * LANES), F32),
        ],
        compiler_params=pltpu.CompilerParams(dimension_semantics=("arbitrary", "arbitrary", "arbitrary"),
                                             vmem_limit_bytes=VMEM_LIMIT),
        name="nsa_attn",
    )(rel_bias, att, kvc, kvc, att, att, att, att, rnn_in, bgate, tiles, cmpb,
      jnp.asarray(ov, BF16), jnp.asarray(kside, BF16))


def kernel(x, mix_norm_g, w_in, b_gate, cmp_pe_k, cmp_pe_v, cmp_k_w1, cmp_k_w2, cmp_v_w1, cmp_v_w2, rel_bias, rnn_conv_w, rnn_conv_b, rg_a_w, rg_a_b, rg_x_w, rg_x_b, rg_lambda, attn_out_g, rnn_out_g, w_out, ffn_norm_g, w_ffn_gate, w_ffn_up, ffn_conv_w, ffn_conv_b, w_ffn_down, final_norm_g):
    batch, seq_len, d_model = x.shape
    assert d_model == D_MODEL and w_in.shape[0] == 1, "single-layer model of the stated width"
    assert seq_len // CMP_STRIDE == N_CMP_PAD and seq_len % TM_FFN == 0
    n = batch * seq_len
    ng, hd = N_KV_GROUPS, HEAD_DIM
    x2 = x.reshape(n, D_MODEL)

    w0 = w_in[0]
    gate0 = ATT_COLS
    rx0 = gate0 + N_BRANCH * N_Q_HEADS
    w_perm = jnp.concatenate(
        [w0[:, :ATT_COLS], w0[:, rx0:rx0 + 2 * RNN_WIDTH], w0[:, gate0:rx0],
         jnp.zeros((D_MODEL, GATE_PAD - N_BRANCH * N_Q_HEADS), w0.dtype)], axis=1).astype(BF16)
    att, rnn_in = _in_proj(x2, mix_norm_g[0][None], w_perm)

    nch = seq_len // CMP_STRIDE
    chunks = att[:, ATTN_WIDTH:ATTN_WIDTH + 2 * KV_WIDTH].reshape(batch, nch, CMP_STRIDE, 2, ng, hd)
    chunks = chunks.transpose(3, 0, 4, 1, 2, 5).reshape(2, batch * ng * nch, CHUNK_W)
    pe = jnp.stack([cmp_pe_k[0].reshape(1, -1), cmp_pe_v[0].reshape(1, -1)]).astype(BF16)
    w1 = jnp.stack([cmp_k_w1[0], cmp_v_w1[0]]).astype(BF16)
    w2 = jnp.stack([jnp.tile(cmp_k_w2[0], (1, 2)), jnp.tile(cmp_v_w2[0], (1, 2))]).astype(BF16)
    kvc = _cmp_tokens(chunks, pe, w1, w2).reshape(2, batch * ng, N_CMP_PAD, LANES)

    tiles, cmpb = _bias_tiles(rel_bias, seq_len)
    bgate = jnp.pad(b_gate[0], (0, GATE_PAD - b_gate.shape[1]))[None]
    o_attn = _nsa_attn(rel_bias, att, rnn_in, kvc, bgate, tiles, cmpb, batch, seq_len)

    nr = _rglru(rnn_in, rnn_conv_w[0], rnn_conv_b[0][None],
                _block_diag(rg_a_w[0]).astype(BF16), rg_a_b[0][None],
                _block_diag(rg_x_w[0]).astype(BF16), rg_x_b[0][None],
                rg_lambda[0][None], rnn_out_g[0][None], batch, seq_len)

    h, y = _out_proj(o_attn, nr, x2, attn_out_g[0][None], ffn_norm_g[0][None], w_out[0].astype(BF16))
    out = _conv_ffn(y, h, w_ffn_gate[0].astype(BF16), w_ffn_up[0].astype(BF16), w_ffn_down[0].astype(BF16),
                    ffn_conv_w[0], ffn_conv_b[0][None], final_norm_g[None], seq_len)
    return out.reshape(batch, seq_len, D_MODEL)
```

```python
import functools
import math

import jax
import jax.numpy as jnp
import numpy as np
from jax import lax
from jax.experimental import pallas as pl
from jax.experimental.pallas import tpu as pltpu

F32 = jnp.float32
BF16 = jnp.bfloat16

D_MODEL = 2048
N_Q_HEADS = 16
N_KV_GROUPS = 4
HEADS_PER_GROUP = 4
HEAD_DIM = 64
ATTN_WIDTH = 1024
KV_WIDTH = 256
SCALE = HEAD_DIM ** -0.5
CMP_BLOCK = 32
CMP_STRIDE = 16
CMP_HIDDEN = 256
SEL_BLOCK = 64
N_SELECT = 16
WINDOW = 512
N_BRANCH = 3
RNN_WIDTH = 1024
RNN_BLOCKS = 16
RNN_CONV_WIDTH = 4
RG_LRU_C = 8.0
N_BUCKETS = 32
MAX_DISTANCE = 128
D_FF = 5632
NORM_EPS = 1e-6
NEG_INF = -1e30
FORCE_BONUS = 1e3
GELU_C0 = math.sqrt(2.0 / math.pi)
GELU_C1 = GELU_C0 * 0.044715

LANES = 128
SUBLANES = 8
VMEM_LIMIT = 56 * 1024 * 1024

TM_PROJ = 512
COL_CHUNK = 512
TM_FFN = 1024
TF_FFN = 512
FFN_ROW_BLOCKS = 2
H_PIECE = 256
TS_RNN = 256
CB_RNN = 256
TQ = 256
TK = 256
ROW_SPLIT = 512
N_CMP_PAD = 128
GATE_PAD = 128

ATT_COLS = ATTN_WIDTH + 6 * KV_WIDTH
RNN_COLS = 2 * RNN_WIDTH + GATE_PAD
CMP_COL0 = ATTN_WIDTH
N_CHUNK_SLABS = 2 * KV_WIDTH // LANES
CHUNK2_W = CMP_STRIDE * LANES


def _rms(x, g):
    return x * lax.rsqrt(jnp.mean(x * x, axis=-1, keepdims=True) + NORM_EPS) * g


def _gelu(x):
    return jax.nn.gelu(x, approximate=True)


def _dot(a, b):
    return jnp.dot(a, b, preferred_element_type=F32)


def _dot_nt(a, b):
    return lax.dot_general(a, b, (((1,), (1,)), ((), ())), preferred_element_type=F32)


def _dot_tn(a, b):
    return lax.dot_general(a, b, (((0,), (0,)), ((), ())), preferred_element_type=F32)


def _in_proj_kernel(x_ref, g_ref, w_ref, att_ref, rnn_ref, chunk_ref, stage_ref):
    xn = _rms(x_ref[...], g_ref[...]).astype(BF16)
    for c0 in range(0, ATT_COLS, COL_CHUNK):
        res = _dot(xn, w_ref[:, c0:c0 + COL_CHUNK])
        att_ref[:, c0:c0 + COL_CHUNK] = res.astype(BF16)
        if c0 == CMP_COL0:
            for q in range(N_CHUNK_SLABS):
                stage_ref[q] = res[:, q * LANES:(q + 1) * LANES]
            for l in range(CMP_STRIDE):
                for q in range(N_CHUNK_SLABS):
                    tok = stage_ref[q, pl.ds(l, TM_PROJ // CMP_STRIDE, stride=CMP_STRIDE), :]
                    chunk_ref[q, :, l * LANES:(l + 1) * LANES] = tok.astype(BF16)
    for c0 in range(0, RNN_COLS, COL_CHUNK):
        c1 = min(c0 + COL_CHUNK, RNN_COLS)
        rnn_ref[:, c0:c1] = _dot(xn, w_ref[:, ATT_COLS + c0:ATT_COLS + c1])


def _in_proj(x2, g, w):
    n = x2.shape[0]
    assert CMP_COL0 % COL_CHUNK == 0 and N_CHUNK_SLABS * LANES == COL_CHUNK
    return pl.pallas_call(
        _in_proj_kernel,
        grid=(n // TM_PROJ,),
        in_specs=[
            pl.BlockSpec((TM_PROJ, D_MODEL), lambda i: (i, 0)),
            pl.BlockSpec((1, D_MODEL), lambda i: (0, 0)),
            pl.BlockSpec((D_MODEL, ATT_COLS + RNN_COLS), lambda i: (0, 0), pipeline_mode=pl.Buffered(1)),
        ],
        out_specs=[
            pl.BlockSpec((TM_PROJ, ATT_COLS), lambda i: (i, 0)),
            pl.BlockSpec((TM_PROJ, RNN_COLS), lambda i: (i, 0)),
            pl.BlockSpec((N_CHUNK_SLABS, TM_PROJ // CMP_STRIDE, CHUNK2_W), lambda i: (0, i, 0)),
        ],
        out_shape=[
            jax.ShapeDtypeStruct((n, ATT_COLS), BF16),
            jax.ShapeDtypeStruct((n, RNN_COLS), F32),
            jax.ShapeDtypeStruct((N_CHUNK_SLABS, n // CMP_STRIDE, CHUNK2_W), BF16),
        ],
        scratch_shapes=[pltpu.VMEM((N_CHUNK_SLABS, TM_PROJ, LANES), F32)],
        compiler_params=pltpu.CompilerParams(dimension_semantics=("parallel",), vmem_limit_bytes=VMEM_LIMIT),
        name="in_proj",
    )(x2, g, w)


def _out_proj_kernel(oa_ref, nr_ref, x_ref, ga_ref, gf_ref, w_ref, h_ref, y_ref):
    na = _rms(oa_ref[...], ga_ref[...]).astype(BF16)
    nr = nr_ref[...]
    for c0 in range(0, D_MODEL, COL_CHUNK):
        sl = slice(c0, c0 + COL_CHUNK)
        h_ref[:, sl] = (x_ref[:, sl] + _dot(na, w_ref[0:ATTN_WIDTH, sl])
                        + _dot(nr, w_ref[ATTN_WIDTH:D_MODEL, sl]))
    y_ref[...] = _rms(h_ref[...], gf_ref[...]).astype(BF16)


def _out_proj(o_attn, nr, x2, ga, gf, w):
    n = x2.shape[0]
    return pl.pallas_call(
        _out_proj_kernel,
        grid=(n // TM_PROJ,),
        in_specs=[
            pl.BlockSpec((TM_PROJ, ATTN_WIDTH), lambda i: (i, 0)),
            pl.BlockSpec((TM_PROJ, RNN_WIDTH), lambda i: (i, 0)),
            pl.BlockSpec((TM_PROJ, D_MODEL), lambda i: (i, 0)),
            pl.BlockSpec((1, ATTN_WIDTH), lambda i: (0, 0)),
            pl.BlockSpec((1, D_MODEL), lambda i: (0, 0)),
            pl.BlockSpec((D_MODEL, D_MODEL), lambda i: (0, 0), pipeline_mode=pl.Buffered(1)),
        ],
        out_specs=[
            pl.BlockSpec((TM_PROJ, D_MODEL), lambda i: (i, 0)),
            pl.BlockSpec((TM_PROJ, D_MODEL), lambda i: (i, 0)),
        ],
        out_shape=[
            jax.ShapeDtypeStruct((n, D_MODEL), F32),
            jax.ShapeDtypeStruct((n, D_MODEL), BF16),
        ],
        compiler_params=pltpu.CompilerParams(dimension_semantics=("parallel",), vmem_limit_bytes=VMEM_LIMIT),
        name="out_proj",
    )(o_attn, nr, x2, ga, gf, w)


def _ffn_kernel(tiles_per_seq, y_ref, h_ref, wg_ref, wu_ref, wd_ref, cw_ref, cb_ref, gf_ref, out_ref, carry_ref):
    i = pl.program_id(0)
    j = pl.program_id(1)
    nj = pl.num_programs(1)

    @pl.when((i % tiles_per_seq) == 0)
    def _():
        carry_ref[j] = jnp.zeros((SUBLANES, TF_FFN), F32)

    @pl.when(j == 0)
    def _():
        out_ref[...] = jnp.zeros(out_ref.shape, F32)

    @pl.when(j < D_MODEL // H_PIECE)
    def _():
        cols = pl.ds(pl.multiple_of(j * H_PIECE, H_PIECE), H_PIECE)
        out_ref[:, cols] += h_ref[...]

    row = lax.broadcasted_iota(jnp.int32, (SUBLANES, TF_FFN), 0)
    cw = cw_ref[...]
    cb = cb_ref[...]
    prev = carry_ref[j]
    rows_blk = TM_FFN // FFN_ROW_BLOCKS
    for blk in range(FFN_ROW_BLOCKS):
        sl = slice(blk * rows_blk, (blk + 1) * rows_blk)
        y = y_ref[sl, :]
        gt = _dot(y, wg_ref[...])
        p1 = prev[SUBLANES - 1:SUBLANES, :]
        p2 = prev[SUBLANES - 2:SUBLANES - 1, :]
        r1 = pltpu.roll(gt, 1, 0)
        r2 = pltpu.roll(gt, 2, 0)
        h1 = jnp.where(row == 0, p1, r1[0:SUBLANES])
        h2 = jnp.where(row == 0, p2, jnp.where(row == 1, p1, r2[0:SUBLANES]))
        s1 = jnp.concatenate([h1, r1[SUBLANES:]], axis=0)
        s2 = jnp.concatenate([h2, r2[SUBLANES:]], axis=0)
        conv = cw[2:3, :] * gt + cw[1:2, :] * s1 + cw[0:1, :] * s2 + cb
        inner = conv * (GELU_C0 + GELU_C1 * (conv * conv))
        hx = 0.5 * conv
        u = hx + hx * jnp.tanh(inner)
        z = (u * _dot(y, wu_ref[...])).astype(BF16)
        out_ref[sl, :] += _dot(z, wd_ref[...])
        prev = gt[rows_blk - SUBLANES:rows_blk, :]
    carry_ref[j] = prev

    @pl.when(j == nj - 1)
    def _():
        out_ref[...] = _rms(out_ref[...], gf_ref[...])


def _conv_ffn(y, h, wg, wu, wd, cw, cb, gf, seq_len):
    n = y.shape[0]
    nj = D_FF // TF_FFN
    return pl.pallas_call(
        functools.partial(_ffn_kernel, seq_len // TM_FFN),
        grid=(n // TM_FFN, nj),
        in_specs=[
            pl.BlockSpec((TM_FFN, D_MODEL), lambda i, j: (i, 0)),
            pl.BlockSpec((TM_FFN, H_PIECE), lambda i, j: (i, jnp.minimum(j, D_MODEL // H_PIECE - 1))),
            pl.BlockSpec((D_MODEL, TF_FFN), lambda i, j: (0, j)),
            pl.BlockSpec((D_MODEL, TF_FFN), lambda i, j: (0, j)),
            pl.BlockSpec((TF_FFN, D_MODEL), lambda i, j: (j, 0)),
            pl.BlockSpec((3, TF_FFN), lambda i, j: (0, j)),
            pl.BlockSpec((1, TF_FFN), lambda i, j: (0, j)),
            pl.BlockSpec((1, D_MODEL), lambda i, j: (0, 0)),
        ],
        out_specs=pl.BlockSpec((TM_FFN, D_MODEL), lambda i, j: (i, 0)),
        out_shape=jax.ShapeDtypeStruct((n, D_MODEL), F32),
        scratch_shapes=[pltpu.VMEM((nj, SUBLANES, TF_FFN), F32)],
        compiler_params=pltpu.CompilerParams(dimension_semantics=("arbitrary", "arbitrary"),
                                             vmem_limit_bytes=VMEM_LIMIT),
        name="conv_ffn",
    )(y, h, wg, wu, wd, cw, cb, gf)


def _rglru_kernel(rx_ref, ry_ref, cw_ref, cb_ref, wa_ref, ba_ref, wx_ref, bx_ref, lam_ref, g_ref,
                  out_ref, tail_ref, hc_ref, a_ref, u_ref, o_ref):
    c = pl.program_id(1)

    @pl.when(c == 0)
    def _():
        tail_ref[...] = jnp.zeros(tail_ref.shape, F32)
        hc_ref[...] = jnp.zeros(hc_ref.shape, F32)

    row8 = lax.broadcasted_iota(jnp.int32, (SUBLANES, CB_RNN), 0)
    rmod = lax.broadcasted_iota(jnp.int32, (TS_RNN // SUBLANES, SUBLANES, CB_RNN), 1)
    lam = lam_ref[...]
    nlam = -lam
    softplus = jnp.maximum(nlam, 0.0) + jnp.log1p(jnp.exp(-jnp.abs(nlam)))
    log_a_unit = -RG_LRU_C * softplus

    for cb in range(RNN_WIDTH // CB_RNN):
        sl = slice(cb * CB_RNN, (cb + 1) * CB_RNN)
        rx = rx_ref[:, sl]
        prev = tail_ref[:, sl]
        tail_ref[:, sl] = rx[TS_RNN - SUBLANES:TS_RNN, :]
        cw = cw_ref[:, sl]
        xr = cw[3:4, :] * rx + cb_ref[:, sl]
        for d in (1, 2, 3):
            sh = pltpu.roll(rx, d, 0)
            head = sh[0:SUBLANES]
            for k in range(d):
                head = jnp.where(row8 == k, prev[SUBLANES - d + k:SUBLANES - d + k + 1, :], head)
            xr = xr + cw[3 - d:4 - d, :] * jnp.concatenate([head, sh[SUBLANES:]], axis=0)
        xb = xr.astype(BF16)
        r = jax.nn.sigmoid(_dot(xb, wa_ref[cb]) + ba_ref[:, sl])
        gi = jax.nn.sigmoid(_dot(xb, wx_ref[cb]) + bx_ref[:, sl])
        log_a = log_a_unit[:, sl] * r
        a = jnp.exp(log_a)
        th = jnp.tanh(log_a)
        u = jnp.sqrt(-2.0 * th / (1.0 - th)) * (gi * xr)
        a = a.reshape(TS_RNN // SUBLANES, SUBLANES, CB_RNN)
        u = u.reshape(TS_RNN // SUBLANES, SUBLANES, CB_RNN)
        for d in (1, 2, 4):
            keep = rmod >= d
            a_s = pltpu.roll(a, d, 1)
            u_s = pltpu.roll(u, d, 1)
            u = jnp.where(keep, a * u_s + u, u)
            a = jnp.where(keep, a * a_s, a)
        a_ref[...] = a.reshape(TS_RNN, CB_RNN)
        u_ref[...] = u.reshape(TS_RNN, CB_RNN)

        def group(k, hprev):
            r0 = pl.multiple_of(k * SUBLANES, SUBLANES)
            h8 = u_ref[pl.ds(r0, SUBLANES), :] + a_ref[pl.ds(r0, SUBLANES), :] * hprev
            u_ref[pl.ds(r0, SUBLANES), :] = h8
            return jnp.broadcast_to(h8[SUBLANES - 1:SUBLANES, :], (SUBLANES, CB_RNN))

        hlast = lax.fori_loop(0, TS_RNN // SUBLANES, group, hc_ref[:, sl], unroll=4)
        hc_ref[:, sl] = hlast
        o_ref[:, sl] = u_ref[...] * _gelu(ry_ref[:, sl])

    out_ref[...] = _rms(o_ref[...], g_ref[...]).astype(BF16)


def _rglru(rnn_in, cw, cb, wa, ba, wx, bx, lam, g, batch, seq_len):
    n = rnn_in.shape[0]
    nc = seq_len // TS_RNN
    nb = RNN_WIDTH // CB_RNN
    vec = pl.BlockSpec((1, RNN_WIDTH), lambda b, c: (0, 0))
    wspec = pl.BlockSpec((nb, CB_RNN, CB_RNN), lambda b, c: (0, 0, 0))
    return pl.pallas_call(
        _rglru_kernel,
        grid=(batch, nc),
        in_specs=[
            pl.BlockSpec((TS_RNN, RNN_WIDTH), lambda b, c: (b * nc + c, 0)),
            pl.BlockSpec((TS_RNN, RNN_WIDTH), lambda b, c: (b * nc + c, 1)),
            pl.BlockSpec((RNN_CONV_WIDTH, RNN_WIDTH), lambda b, c: (0, 0)),
            vec, wspec, vec, wspec, vec, vec, vec,
        ],
        out_specs=pl.BlockSpec((TS_RNN, RNN_WIDTH), lambda b, c: (b * nc + c, 0)),
        out_shape=jax.ShapeDtypeStruct((n, RNN_WIDTH), BF16),
        scratch_shapes=[
            pltpu.VMEM((SUBLANES, RNN_WIDTH), F32),
            pltpu.VMEM((SUBLANES, RNN_WIDTH), F32),
            pltpu.VMEM((TS_RNN, CB_RNN), F32),
            pltpu.VMEM((TS_RNN, CB_RNN), F32),
            pltpu.VMEM((TS_RNN, RNN_WIDTH), F32),
        ],
        compiler_params=pltpu.CompilerParams(dimension_semantics=("arbitrary", "arbitrary"),
                                             vmem_limit_bytes=VMEM_LIMIT),
        name="rglru",
    )(rnn_in, rnn_in, cw, cb, wa, ba, wx, bx, lam, g)


def _block_diag(w):
    per = CB_RNN // (RNN_WIDTH // RNN_BLOCKS)
    bd = w.shape[-1]
    w4 = w.reshape(RNN_BLOCKS // per, per, bd, bd)
    eye = jnp.eye(per, dtype=w.dtype)
    return jnp.einsum('cpij,pq->cpiqj', w4, eye).reshape(RNN_BLOCKS // per, per * bd, per * bd)


CMP_BATCHES = 8


def _cmp_tokens_kernel(x_ref, pe_ref, w1_ref, w2_ref, out_ref):
    x = x_ref[...]
    rows = x.shape[0]
    ya = _dot(x, w1_ref[0])
    yb = _dot(x, w1_ref[1])
    pterm = (_dot(jnp.broadcast_to(pe_ref[0], (SUBLANES, CHUNK2_W)), w1_ref[0])
             + _dot(jnp.broadcast_to(pe_ref[1], (SUBLANES, CHUNK2_W)), w1_ref[1]))[0:1, :]
    act = _gelu(ya + pltpu.roll(yb, rows - 1, 0) + pterm).astype(BF16)
    for e in range(2):
        tok = _dot(act[:, e * CMP_HIDDEN:(e + 1) * CMP_HIDDEN], w2_ref[...]).astype(BF16)
        out_ref[:, e] = tok.reshape(rows // N_CMP_PAD, N_CMP_PAD, LANES)


def _cmp_tokens(chunks, pe, w1p, w2, batch):
    nb = min(CMP_BATCHES, batch)
    tr = nb * N_CMP_PAD
    return pl.pallas_call(
        _cmp_tokens_kernel,
        grid=(N_CHUNK_SLABS, batch // nb),
        in_specs=[
            pl.BlockSpec((None, tr, CHUNK2_W), lambda s, i: (s, i, 0)),
            pl.BlockSpec((None, 2, 1, CHUNK2_W), lambda s, i: (s // 2, 0, 0, 0)),
            pl.BlockSpec((None, 2, CHUNK2_W, 2 * CMP_HIDDEN), lambda s, i: (s // 2, 0, 0, 0)),
            pl.BlockSpec((None, CMP_HIDDEN, LANES), lambda s, i: (s // 2, 0, 0)),
        ],
        out_specs=pl.BlockSpec((None, nb, None, 2, N_CMP_PAD, LANES), lambda s, i: (s // 2, i, s % 2, 0, 0, 0)),
        out_shape=jax.ShapeDtypeStruct((2, batch, 2, 2, N_CMP_PAD, LANES), BF16),
        compiler_params=pltpu.CompilerParams(dimension_semantics=("parallel", "parallel")),
        name="cmp_tokens",
    )(chunks, pe, w1p, w2)


def _pair_weights(w1):
    half = CMP_STRIDE * HEAD_DIM
    w = w1.reshape(2, CMP_STRIDE, HEAD_DIM, CMP_HIDDEN)
    eye = jnp.eye(2, dtype=w1.dtype)
    return jnp.einsum('hldj,ef->hledfj', w, eye).reshape(2, 2 * half, 2 * CMP_HIDDEN)


def _pair_pe(pe):
    p = pe.reshape(2, CMP_STRIDE, 1, HEAD_DIM)
    return jnp.broadcast_to(p, (2, CMP_STRIDE, 2, HEAD_DIM)).reshape(2, 1, CHUNK2_W)


N_TILE_TYPES = 4
TILE_DIAG, TILE_NEAR, TILE_WIN_LAST, TILE_NONE = 0, 1, 2, 3


def _t5_bucket_np(dist):
    n = np.maximum(dist, 0)
    max_exact = N_BUCKETS // 2
    nf = np.maximum(n, 1).astype(np.float64)
    large = max_exact + (np.log(nf / max_exact) / math.log(MAX_DISTANCE / max_exact)
                         * (N_BUCKETS - max_exact)).astype(np.int32)
    large = np.minimum(large, N_BUCKETS - 1)
    return np.where(n < max_exact, n, large).astype(np.int32)


def _bucket_constants(seq_len):
    i = np.arange(TQ)[:, None]
    j = np.arange(TK)[None, :]
    d0 = i - j
    d1 = TK + i - j
    d2 = 2 * TK + i - j
    valid = [d0 >= 0, np.ones_like(d1, bool), d2 < WINDOW, np.zeros_like(d0, bool)]
    tiles = np.stack([np.where(v, _t5_bucket_np(d), -1) for d, v in zip([d0, d1, d2, d0], valid)])
    t = np.arange(seq_len)[:, None]
    c = np.arange(N_CMP_PAD)[None, :]
    dc = t - (c * CMP_STRIDE + CMP_BLOCK - 1)
    cmp = np.where(dc >= 0, _t5_bucket_np(dc), -1)
    return tiles.astype(np.int32), cmp.astype(np.int32)


def _bias_kernel(tile_buckets, rb_ref, tb_ref, cbk_ref, tiles_ref, cmpb_ref):
    h = pl.program_id(0)

    def lookup(bkt, buckets):
        acc = jnp.full(bkt.shape, NEG_INF, F32)
        for b in buckets:
            acc = jnp.where(bkt == b, rb_ref[b, h], acc)
        return acc

    for t in range(N_TILE_TYPES):
        tiles_ref[t] = lookup(tb_ref[t], tile_buckets[t])
    cmpb_ref[...] = lookup(cbk_ref[...], range(N_BUCKETS))


def _bias_tiles(rel_bias, seq_len):
    tb, cbk = _bucket_constants(seq_len)
    tile_buckets = tuple(tuple(int(b) for b in np.unique(t[t >= 0])) for t in tb)
    r = HEADS_PER_GROUP
    return pl.pallas_call(
        functools.partial(_bias_kernel, tile_buckets),
        grid=(N_Q_HEADS,),
        in_specs=[
            pl.BlockSpec(memory_space=pltpu.SMEM),
            pl.BlockSpec((N_TILE_TYPES, TQ, TK), lambda h: (0, 0, 0)),
            pl.BlockSpec((seq_len, N_CMP_PAD), lambda h: (0, 0)),
        ],
        out_specs=[
            pl.BlockSpec((None, N_TILE_TYPES, None, TQ, TK), lambda h: (h // r, 0, h % r, 0, 0)),
            pl.BlockSpec((None, None, seq_len, N_CMP_PAD), lambda h: (h // r, h % r, 0, 0)),
        ],
        out_shape=[
            jax.ShapeDtypeStruct((N_KV_GROUPS, N_TILE_TYPES, r, TQ, TK), F32),
            jax.ShapeDtypeStruct((N_KV_GROUPS, r, seq_len, N_CMP_PAD), F32),
        ],
        compiler_params=pltpu.CompilerParams(dimension_semantics=("parallel",)),
        name="bias_tiles",
    )(rel_bias, jnp.asarray(tb), jnp.asarray(cbk))


SIDE_FAR_HI, SIDE_FAR_LO, SIDE_OFF = 32, 33, 34
KSIDE_NEAR, KSIDE_FAR, KSIDE_OFF = 0, 1, 2


def _key_side_constants(seq_len):
    s = np.arange(seq_len)
    side = np.zeros((3, seq_len, LANES), np.float32)
    side[:, s, s // SEL_BLOCK] = 1.0
    side[KSIDE_FAR, :, SIDE_FAR_HI] = 1.0
    side[KSIDE_FAR, :, SIDE_FAR_LO] = 1.0
    side[KSIDE_OFF, :, SIDE_OFF] = 1.0
    return side


def _overlap_constants(seq_len):
    n_sb = seq_len // SEL_BLOCK
    c = np.arange(N_CMP_PAD)[None, :]
    j = np.arange(n_sb)[:, None]
    lo = np.maximum(c * CMP_STRIDE, j * SEL_BLOCK)
    hi = np.minimum(c * CMP_STRIDE + CMP_BLOCK, (j + 1) * SEL_BLOCK)
    ov = np.maximum(hi - lo, 0) / CMP_BLOCK
    ov[:, N_CMP_PAD - 1] = 0.0
    return ov.astype(np.float32)


def _nsa_kernel(n_sb, rb_ref, q_ref, kc_ref, vc_ref, ks_ref, vs_ref, kw_ref, vw_ref, gate_ref, bg_ref,
                tiles_ref, cmpb_ref, ov_ref, kside_ref, out_ref,
                qs_ref, side_ref, m_ref, acc_ref):
    g = pl.program_id(0)
    qi = pl.program_id(1)
    par = g % 2
    r_heads = HEADS_PER_GROUP
    rows = r_heads * TQ
    lane = lax.broadcasted_iota(jnp.int32, (TQ, LANES), 1)
    in_half = (lane // HEAD_DIM) == par

    for blk in range(2):
        qb = q_ref[:, blk * LANES:(blk + 1) * LANES].astype(F32)
        qr = pltpu.roll(qb, HEAD_DIM, 1)
        for hh in range(2):
            r = blk * 2 + hh
            src = jnp.where(par == hh, qb, qr)
            qs_ref[r * TQ:(r + 1) * TQ, :] = (jnp.where(in_half, src, 0.0) * SCALE).astype(BF16)

    gs = jax.nn.sigmoid(gate_ref[...] + bg_ref[...])
    gates = [[jnp.sum(jnp.where(lane == N_BRANCH * (r_heads * g + r) + br, gs, 0.0), axis=-1, keepdims=True)
              for br in range(N_BRANCH)] for r in range(r_heads)]

    def tile_rows(kj):
        return pl.ds(pl.multiple_of(kj * TK, TK), TK)

    ones = jnp.ones((TK, LANES), BF16)

    def scores(q, key_tiles, tile_types):
        keys = jnp.concatenate(key_tiles, axis=0)
        s = jnp.concatenate([_dot_nt(q[c:c + ROW_SPLIT], keys) for c in range(0, rows, ROW_SPLIT)], axis=0)
        out = []
        for t, kind in enumerate(tile_types):
            st = s[:, t * TK:(t + 1) * TK]
            if kind is not None:
                st = (st.reshape(r_heads, TQ, TK) + tiles_ref[kind]).reshape(rows, TK)
            out.append(st)
        return out

    def weighted(ps, v_ref, kjs):
        p = jnp.concatenate([x.astype(BF16) for x in ps], axis=1)
        v = jnp.concatenate([jnp.concatenate([v_ref[tile_rows(kj), :], ones], axis=1) for kj in kjs], axis=0)
        return jnp.concatenate([_dot(p[c:c + ROW_SPLIT], v) for c in range(0, rows, ROW_SPLIT)], axis=0)

    def row_max(ss):
        return jnp.max(functools.reduce(jnp.maximum, ss), axis=-1, keepdims=True)

    def normalised(acc):
        return acc[:, 0:LANES] * (1.0 / acc[:, LANES:2 * LANES])

    kj1 = jnp.maximum(qi - 1, 0)
    near_type = jnp.where(qi >= 1, TILE_NEAR, TILE_NONE)

    kj2 = jnp.maximum(qi - 2, 0)
    last_type = jnp.where(qi >= 2, TILE_WIN_LAST, TILE_NONE)
    sw = scores(qs_ref[...], [kw_ref[tile_rows(kj), :] for kj in (qi, kj1, kj2)],
                (TILE_DIAG, near_type, last_type))
    mw = row_max(sw)
    o_win = normalised(weighted([jnp.exp(s - mw) for s in sw], vw_ref, (qi, kj1, kj2)))

    trow = qi * TQ + lax.broadcasted_iota(jnp.int32, (TQ, 1), 0)
    has_any = jnp.where(trow >= CMP_BLOCK - 1, 1.0, 0.0)
    kc = kc_ref[...]
    vc = vc_ref[...]
    psum = jnp.zeros((TQ, N_CMP_PAD), F32)
    o_cmp = []
    for r in range(r_heads):
        s = _dot_nt(qs_ref[r * TQ:(r + 1) * TQ, :], kc) + cmpb_ref[r]
        e = jnp.exp(s - jnp.max(s, axis=-1, keepdims=True))
        p = e * (has_any / jnp.sum(e, axis=-1, keepdims=True))
        psum = psum + p
        o_cmp.append(_dot(p.astype(BF16), vc))

    p_hi = psum.astype(BF16)
    p_lo = (psum - p_hi.astype(F32)).astype(BF16)
    ov = ov_ref[...]
    imp = _dot_nt(ov, p_hi) + _dot_nt(ov, p_lo)
    jrow = lax.broadcasted_iota(jnp.int32, (n_sb, TQ), 0)
    tq = qi * TQ + lax.broadcasted_iota(jnp.int32, (n_sb, TQ), 1)
    cur = tq // SEL_BLOCK
    valid = jrow <= cur
    forced = jnp.where(jrow == 0, 1.0, jnp.where(jrow == cur, 1.0, jnp.where(jrow == cur - 1, 1.0, 0.0)))
    score = jnp.where(valid, imp + FORCE_BONUS * forced, -1.0)
    rank = jnp.zeros((n_sb, TQ), F32)
    for jp in range(n_sb):
        other = jnp.broadcast_to(score[jp:jp + 1, :], (n_sb, TQ))
        before = jnp.where(jrow > jp, 1.0, 0.0)
        rank = rank + jnp.where(other > score, 1.0, jnp.where(other == score, before, 0.0))
    chosen = jnp.where(valid, jnp.where(rank < float(min(N_SELECT, n_sb)), 0.0, NEG_INF), NEG_INF)

    srow = lax.broadcasted_iota(jnp.int32, (LANES - n_sb, TQ), 0) + n_sb
    side_t = jnp.concatenate([chosen, jnp.where(srow == SIDE_OFF, NEG_INF, 0.0)], axis=0)
    side = side_t.T
    for r in range(r_heads):
        far = jnp.full((TQ, LANES), rb_ref[N_BUCKETS - 1, r_heads * g + r], F32)
        far_hi = far.astype(BF16).astype(F32)
        side_r = jnp.where(lane == SIDE_FAR_HI, far_hi, jnp.where(lane == SIDE_FAR_LO, far - far_hi, side))
        side_ref[r * TQ:(r + 1) * TQ, :] = side_r.astype(BF16)

    def key_side(kj, side_kind):
        return jnp.concatenate([ks_ref[tile_rows(kj), :], kside_ref[side_kind, tile_rows(kj), :]], axis=1)

    def q_side():
        return jnp.concatenate([qs_ref[...], side_ref[...]], axis=1)

    sn = scores(q_side(), [key_side(qi, KSIDE_NEAR), key_side(kj1, KSIDE_NEAR)], (TILE_DIAG, near_type))
    m0 = row_max(sn)
    m_ref[...] = jnp.broadcast_to(m0, (rows, LANES))
    acc_ref[...] = weighted([jnp.exp(s - m0) for s in sn], vs_ref, (qi, kj1))

    def far_pair(idx, carry):
        ka = qi - 2 - 2 * idx
        kb_raw = ka - 1
        kb = jnp.maximum(kb_raw, 0)
        sf = scores(q_side(), [key_side(ka, KSIDE_FAR),
                               key_side(kb, jnp.where(kb_raw >= 0, KSIDE_FAR, KSIDE_OFF))], (None, None))
        m_prev = m_ref[...]
        m_new = jnp.maximum(m_prev, row_max(sf))
        alpha = jnp.exp(m_prev - m_new)
        m_wide = jnp.tile(m_new, (1, TK // LANES))
        acc_ref[...] = (jnp.tile(alpha, (1, 2)) * acc_ref[...]
                        + weighted([jnp.exp(s - m_wide) for s in sf], vs_ref, (ka, kb)))
        m_ref[...] = m_new
        return carry

    lax.fori_loop(0, jnp.maximum(qi, 1) // 2, far_pair, 0)
    o_slc = normalised(acc_ref[...])

    heads = []
    for r in range(r_heads):
        w = gates[r]
        heads.append(w[0] * o_cmp[r] + w[1] * o_slc[r * TQ:(r + 1) * TQ, :] + w[2] * o_win[r * TQ:(r + 1) * TQ, :])
    for blk in range(2):
        lo, hi = heads[2 * blk], heads[2 * blk + 1]
        lo = jnp.where(par == 0, lo, pltpu.roll(lo, HEAD_DIM, 1))
        hi = jnp.where(par == 1, hi, pltpu.roll(hi, HEAD_DIM, 1))
        out_ref[:, blk * LANES:(blk + 1) * LANES] = jnp.where(lane < HEAD_DIM, lo, hi)


def _nsa_attn(rel_bias, att, rnn_in, kvc, bgate, tiles, cmpb, batch, seq_len):
    n = att.shape[0]
    nq = seq_len // TQ
    n_sb = seq_len // SEL_BLOCK
    ng = N_KV_GROUPS
    r = HEADS_PER_GROUP
    ov = _overlap_constants(seq_len)
    kside = _key_side_constants(seq_len)
    qw = r * HEAD_DIM
    kcol = ATTN_WIDTH + 2 * KV_WIDTH
    kvb = lambda off: (lambda g, qi, b: (b, (kcol + off * KV_WIDTH) // LANES + g // 2))
    return pl.pallas_call(
        functools.partial(_nsa_kernel, n_sb),
        grid=(ng, nq, batch),
        in_specs=[
            pl.BlockSpec(memory_space=pltpu.SMEM),
            pl.BlockSpec((TQ, qw), lambda g, qi, b: (b * nq + qi, g)),
            pl.BlockSpec((None, None, N_CMP_PAD, LANES), lambda g, qi, b: (0, b * ng + g, 0, 0)),
            pl.BlockSpec((None, None, N_CMP_PAD, LANES), lambda g, qi, b: (1, b * ng + g, 0, 0)),
            pl.BlockSpec((seq_len, LANES), kvb(0)),
            pl.BlockSpec((seq_len, LANES), kvb(1)),
            pl.BlockSpec((seq_len, LANES), kvb(2)),
            pl.BlockSpec((seq_len, LANES), kvb(3)),
            pl.BlockSpec((TQ, GATE_PAD), lambda g, qi, b: (b * nq + qi, 2 * RNN_WIDTH // GATE_PAD)),
            pl.BlockSpec((1, GATE_PAD), lambda g, qi, b: (0, 0)),
            pl.BlockSpec((None, N_TILE_TYPES, r, TQ, TK), lambda g, qi, b: (g, 0, 0, 0, 0)),
            pl.BlockSpec((None, r, TQ, N_CMP_PAD), lambda g, qi, b: (g, 0, qi, 0)),
            pl.BlockSpec((n_sb, N_CMP_PAD), lambda g, qi, b: (0, 0)),
            pl.BlockSpec((3, seq_len, LANES), lambda g, qi, b: (0, 0, 0)),
        ],
        out_specs=pl.BlockSpec((TQ, qw), lambda g, qi, b: (b * nq + qi, g)),
        out_shape=jax.ShapeDtypeStruct((n, ATTN_WIDTH), F32),
        scratch_shapes=[
            pltpu.VMEM((r * TQ, LANES), BF16),
            pltpu.VMEM((r * TQ, LANES), BF16),
            pltpu.VMEM((r * TQ, LANES), F32),
            pltpu.VMEM((r * TQ, 2 * LANES), F32),
        ],
        compiler_params=pltpu.CompilerParams(dimension_semantics=("arbitrary", "arbitrary", "arbitrary"),
                                             vmem_limit_bytes=VMEM_LIMIT),
        name="nsa_attn",
    )(rel_bias, att, kvc, kvc, att, att, att, att, rnn_in, bgate, tiles, cmpb,
      jnp.asarray(ov, BF16), jnp.asarray(kside, BF16))


def kernel(x, mix_norm_g, w_in, b_gate, cmp_pe_k, cmp_pe_v, cmp_k_w1, cmp_k_w2, cmp_v_w1, cmp_v_w2, rel_bias, rnn_conv_w, rnn_conv_b, rg_a_w, rg_a_b, rg_x_w, rg_x_b, rg_lambda, attn_out_g, rnn_out_g, w_out, ffn_norm_g, w_ffn_gate, w_ffn_up, ffn_conv_w, ffn_conv_b, w_ffn_down, final_norm_g):
    batch, seq_len, d_model = x.shape
    assert d_model == D_MODEL and w_in.shape[0] == 1, "single-layer model of the stated width"
    assert seq_len // CMP_STRIDE == N_CMP_PAD and seq_len % TM_FFN == 0
    n = batch * seq_len
    ng = N_KV_GROUPS
    x2 = x.reshape(n, D_MODEL)

    w0 = w_in[0]
    gate0 = ATT_COLS
    rx0 = gate0 + N_BRANCH * N_Q_HEADS
    w_perm = jnp.concatenate(
        [w0[:, :ATT_COLS], w0[:, rx0:rx0 + 2 * RNN_WIDTH], w0[:, gate0:rx0],
         jnp.zeros((D_MODEL, GATE_PAD - N_BRANCH * N_Q_HEADS), w0.dtype)], axis=1).astype(BF16)
    att, rnn_in, chunks = _in_proj(x2, mix_norm_g[0][None], w_perm)

    pe = jnp.stack([_pair_pe(cmp_pe_k[0]), _pair_pe(cmp_pe_v[0])]).astype(BF16)
    w1p = jnp.stack([_pair_weights(cmp_k_w1[0]), _pair_weights(cmp_v_w1[0])]).astype(BF16)
    w2 = jnp.stack([jnp.tile(cmp_k_w2[0], (1, 2)), jnp.tile(cmp_v_w2[0], (1, 2))]).astype(BF16)
    kvc = _cmp_tokens(chunks, pe, w1p, w2, batch).reshape(2, batch * ng, N_CMP_PAD, LANES)

    tiles, cmpb = _bias_tiles(rel_bias, seq_len)
    bgate = jnp.pad(b_gate[0], (0, GATE_PAD - b_gate.shape[1]))[None]
    o_attn = _nsa_attn(rel_bias, att, rnn_in, kvc, bgate, tiles, cmpb, batch, seq_len)

    nr = _rglru(rnn_in, rnn_conv_w[0], rnn_conv_b[0][None],
                _block_diag(rg_a_w[0]).astype(BF16), rg_a_b[0][None],
                _block_diag(rg_x_w[0]).astype(BF16), rg_x_b[0][None],
                rg_lambda[0][None], rnn_out_g[0][None], batch, seq_len)

    h, y = _out_proj(o_attn, nr, x2, attn_out_g[0][None], ffn_norm_g[0][None], w_out[0].astype(BF16))
    out = _conv_ffn(y, h, w_ffn_gate[0].astype(BF16), w_ffn_up[0].astype(BF16), w_ffn_down[0].astype(BF16),
                    ffn_conv_w[0], ffn_conv_b[0][None], final_norm_g[None], seq_len)
    return out.reshape(batch, seq_len, D_MODEL)
```

```python
import functools
import math

import jax
import jax.numpy as jnp
import numpy as np
from jax import lax
from jax.experimental import pallas as pl
from jax.experimental.pallas import tpu as pltpu

F32 = jnp.float32
BF16 = jnp.bfloat16

D_MODEL = 2048
N_Q_HEADS = 16
N_KV_GROUPS = 4
HEADS_PER_GROUP = 4
HEAD_DIM = 64
ATTN_WIDTH = 1024
KV_WIDTH = 256
SCALE = HEAD_DIM ** -0.5
CMP_BLOCK = 32
CMP_STRIDE = 16
CMP_HIDDEN = 256
SEL_BLOCK = 64
N_SELECT = 16
WINDOW = 512
N_BRANCH = 3
RNN_WIDTH = 1024
RNN_BLOCKS = 16
RNN_CONV_WIDTH = 4
RG_LRU_C = 8.0
N_BUCKETS = 32
MAX_DISTANCE = 128
D_FF = 5632
NORM_EPS = 1e-6
NEG_INF = -1e30
FORCE_BONUS = 1e3
GELU_C0 = math.sqrt(2.0 / math.pi)
GELU_C1 = GELU_C0 * 0.044715

LANES = 128
SUBLANES = 8
VMEM_LIMIT = 56 * 1024 * 1024

TM_PROJ = 512
COL_CHUNK = 512
TM_FFN = 1024
TF_FFN = 512
FFN_ROW_BLOCKS = 2
H_PIECE = 256
TS_RNN = 512
CB_RNN = 256
TQ = 256
TK = 256
ATTN_SLOTS = 2
ROW_SPLIT = 512
N_CMP_PAD = 128
GATE_PAD = 128

ATT_COLS = ATTN_WIDTH + 6 * KV_WIDTH
RNN_COLS = 2 * RNN_WIDTH + GATE_PAD
CMP_COL0 = ATTN_WIDTH
N_CHUNK_SLABS = 2 * KV_WIDTH // LANES
CHUNK2_W = CMP_STRIDE * LANES


def _rms(x, g):
    return x * lax.rsqrt(jnp.mean(x * x, axis=-1, keepdims=True) + NORM_EPS) * g


def _gelu(x):
    return jax.nn.gelu(x, approximate=True)


def _dot(a, b):
    return jnp.dot(a, b, preferred_element_type=F32)


def _dot_nt(a, b):
    return lax.dot_general(a, b, (((1,), (1,)), ((), ())), preferred_element_type=F32)


def _dot_tn(a, b):
    return lax.dot_general(a, b, (((0,), (0,)), ((), ())), preferred_element_type=F32)


def _in_proj_kernel(x_ref, g_ref, w_ref, att_ref, rnn_ref, chunk_ref, stage_ref):
    xn = _rms(x_ref[...], g_ref[...]).astype(BF16)
    for c0 in range(0, ATT_COLS, COL_CHUNK):
        res = _dot(xn, w_ref[:, c0:c0 + COL_CHUNK])
        att_ref[:, c0:c0 + COL_CHUNK] = res.astype(BF16)
        if c0 == CMP_COL0:
            for q in range(N_CHUNK_SLABS):
                stage_ref[q] = res[:, q * LANES:(q + 1) * LANES]
            for l in range(CMP_STRIDE):
                for q in range(N_CHUNK_SLABS):
                    tok = stage_ref[q, pl.ds(l, TM_PROJ // CMP_STRIDE, stride=CMP_STRIDE), :]
                    chunk_ref[q, :, l * LANES:(l + 1) * LANES] = tok.astype(BF16)
    for c0 in range(0, RNN_COLS, COL_CHUNK):
        c1 = min(c0 + COL_CHUNK, RNN_COLS)
        rnn_ref[:, c0:c1] = _dot(xn, w_ref[:, ATT_COLS + c0:ATT_COLS + c1])


def _in_proj(x2, g, w):
    n = x2.shape[0]
    assert CMP_COL0 % COL_CHUNK == 0 and N_CHUNK_SLABS * LANES == COL_CHUNK
    return pl.pallas_call(
        _in_proj_kernel,
        grid=(n // TM_PROJ,),
        in_specs=[
            pl.BlockSpec((TM_PROJ, D_MODEL), lambda i: (i, 0)),
            pl.BlockSpec((1, D_MODEL), lambda i: (0, 0)),
            pl.BlockSpec((D_MODEL, ATT_COLS + RNN_COLS), lambda i: (0, 0), pipeline_mode=pl.Buffered(1)),
        ],
        out_specs=[
            pl.BlockSpec((TM_PROJ, ATT_COLS), lambda i: (i, 0)),
            pl.BlockSpec((TM_PROJ, RNN_COLS), lambda i: (i, 0)),
            pl.BlockSpec((N_CHUNK_SLABS, TM_PROJ // CMP_STRIDE, CHUNK2_W), lambda i: (0, i, 0)),
        ],
        out_shape=[
            jax.ShapeDtypeStruct((n, ATT_COLS), BF16),
            jax.ShapeDtypeStruct((n, RNN_COLS), F32),
            jax.ShapeDtypeStruct((N_CHUNK_SLABS, n // CMP_STRIDE, CHUNK2_W), BF16),
        ],
        scratch_shapes=[pltpu.VMEM((N_CHUNK_SLABS, TM_PROJ, LANES), F32)],
        compiler_params=pltpu.CompilerParams(dimension_semantics=("parallel",), vmem_limit_bytes=VMEM_LIMIT),
        name="in_proj",
    )(x2, g, w)


def _out_proj_kernel(oa_ref, nr_ref, x_ref, ga_ref, gf_ref, w_ref, h_ref, y_ref):
    na = _rms(oa_ref[...], ga_ref[...]).astype(BF16)
    nr = nr_ref[...]
    for c0 in range(0, D_MODEL, COL_CHUNK):
        sl = slice(c0, c0 + COL_CHUNK)
        h_ref[:, sl] = (x_ref[:, sl] + _dot(na, w_ref[0:ATTN_WIDTH, sl])
                        + _dot(nr, w_ref[ATTN_WIDTH:D_MODEL, sl]))
    y_ref[...] = _rms(h_ref[...], gf_ref[...]).astype(BF16)


def _out_proj(o_attn, nr, x2, ga, gf, w):
    n = x2.shape[0]
    return pl.pallas_call(
        _out_proj_kernel,
        grid=(n // TM_PROJ,),
        in_specs=[
            pl.BlockSpec((TM_PROJ, ATTN_WIDTH), lambda i: (i, 0)),
            pl.BlockSpec((TM_PROJ, RNN_WIDTH), lambda i: (i, 0)),
            pl.BlockSpec((TM_PROJ, D_MODEL), lambda i: (i, 0)),
            pl.BlockSpec((1, ATTN_WIDTH), lambda i: (0, 0)),
            pl.BlockSpec((1, D_MODEL), lambda i: (0, 0)),
            pl.BlockSpec((D_MODEL, D_MODEL), lambda i: (0, 0), pipeline_mode=pl.Buffered(1)),
        ],
        out_specs=[
            pl.BlockSpec((TM_PROJ, D_MODEL), lambda i: (i, 0)),
            pl.BlockSpec((TM_PROJ, D_MODEL), lambda i: (i, 0)),
        ],
        out_shape=[
            jax.ShapeDtypeStruct((n, D_MODEL), F32),
            jax.ShapeDtypeStruct((n, D_MODEL), BF16),
        ],
        compiler_params=pltpu.CompilerParams(dimension_semantics=("parallel",), vmem_limit_bytes=VMEM_LIMIT),
        name="out_proj",
    )(o_attn, nr, x2, ga, gf, w)


def _ffn_kernel(tiles_per_seq, y_ref, h_ref, wg_ref, wu_ref, wd_ref, cw_ref, cb_ref, gf_ref, out_ref, carry_ref):
    i = pl.program_id(0)
    j = pl.program_id(1)
    nj = pl.num_programs(1)

    @pl.when((i % tiles_per_seq) == 0)
    def _():
        carry_ref[j] = jnp.zeros((SUBLANES, TF_FFN), F32)

    @pl.when(j == 0)
    def _():
        out_ref[...] = jnp.zeros(out_ref.shape, F32)

    @pl.when(j < D_MODEL // H_PIECE)
    def _():
        cols = pl.ds(pl.multiple_of(j * H_PIECE, H_PIECE), H_PIECE)
        out_ref[:, cols] += h_ref[...]

    row = lax.broadcasted_iota(jnp.int32, (SUBLANES, TF_FFN), 0)
    cw = cw_ref[...]
    cb = cb_ref[...]
    prev = carry_ref[j]
    rows_blk = TM_FFN // FFN_ROW_BLOCKS
    for blk in range(FFN_ROW_BLOCKS):
        sl = slice(blk * rows_blk, (blk + 1) * rows_blk)
        y = y_ref[sl, :]
        gt = _dot(y, wg_ref[...])
        p1 = prev[SUBLANES - 1:SUBLANES, :]
        p2 = prev[SUBLANES - 2:SUBLANES - 1, :]
        r1 = pltpu.roll(gt, 1, 0)
        r2 = pltpu.roll(gt, 2, 0)
        h1 = jnp.where(row == 0, p1, r1[0:SUBLANES])
        h2 = jnp.where(row == 0, p2, jnp.where(row == 1, p1, r2[0:SUBLANES]))
        s1 = jnp.concatenate([h1, r1[SUBLANES:]], axis=0)
        s2 = jnp.concatenate([h2, r2[SUBLANES:]], axis=0)
        conv = cw[2:3, :] * gt + cw[1:2, :] * s1 + cw[0:1, :] * s2 + cb
        inner = conv * (GELU_C0 + GELU_C1 * (conv * conv))
        hx = 0.5 * conv
        u = hx + hx * jnp.tanh(inner)
        z = (u * _dot(y, wu_ref[...])).astype(BF16)
        out_ref[sl, :] += _dot(z, wd_ref[...])
        prev = gt[rows_blk - SUBLANES:rows_blk, :]
    carry_ref[j] = prev

    @pl.when(j == nj - 1)
    def _():
        out_ref[...] = _rms(out_ref[...], gf_ref[...])


def _conv_ffn(y, h, wg, wu, wd, cw, cb, gf, seq_len):
    n = y.shape[0]
    nj = D_FF // TF_FFN
    return pl.pallas_call(
        functools.partial(_ffn_kernel, seq_len // TM_FFN),
        grid=(n // TM_FFN, nj),
        in_specs=[
            pl.BlockSpec((TM_FFN, D_MODEL), lambda i, j: (i, 0)),
            pl.BlockSpec((TM_FFN, H_PIECE), lambda i, j: (i, jnp.minimum(j, D_MODEL // H_PIECE - 1))),
            pl.BlockSpec((D_MODEL, TF_FFN), lambda i, j: (0, j)),
            pl.BlockSpec((D_MODEL, TF_FFN), lambda i, j: (0, j)),
            pl.BlockSpec((TF_FFN, D_MODEL), lambda i, j: (j, 0)),
            pl.BlockSpec((3, TF_FFN), lambda i, j: (0, j)),
            pl.BlockSpec((1, TF_FFN), lambda i, j: (0, j)),
            pl.BlockSpec((1, D_MODEL), lambda i, j: (0, 0)),
        ],
        out_specs=pl.BlockSpec((TM_FFN, D_MODEL), lambda i, j: (i, 0)),
        out_shape=jax.ShapeDtypeStruct((n, D_MODEL), F32),
        scratch_shapes=[pltpu.VMEM((nj, SUBLANES, TF_FFN), F32)],
        compiler_params=pltpu.CompilerParams(dimension_semantics=("arbitrary", "arbitrary"),
                                             vmem_limit_bytes=VMEM_LIMIT),
        name="conv_ffn",
    )(y, h, wg, wu, wd, cw, cb, gf)


def _rglru_kernel(rx_ref, ry_ref, cw_ref, cb_ref, wa_ref, ba_ref, wx_ref, bx_ref, lam_ref, g_ref,
                  out_ref, tail_ref, hc_ref, a_ref, u_ref, o_ref, xbuf_ref):
    c = pl.program_id(1)

    @pl.when(c == 0)
    def _():
        tail_ref[...] = jnp.zeros(tail_ref.shape, F32)
        hc_ref[...] = jnp.zeros(hc_ref.shape, F32)

    rmod = lax.broadcasted_iota(jnp.int32, (TS_RNN // SUBLANES, SUBLANES, CB_RNN), 1)
    lam = lam_ref[...]
    nlam = -lam
    softplus = jnp.maximum(nlam, 0.0) + jnp.log1p(jnp.exp(-jnp.abs(nlam)))
    log_a_unit = -RG_LRU_C * softplus

    for cb in range(RNN_WIDTH // CB_RNN):
        sl = slice(cb * CB_RNN, (cb + 1) * CB_RNN)
        rx = rx_ref[:, sl]
        xbuf_ref[0:SUBLANES, :] = tail_ref[:, sl]
        xbuf_ref[SUBLANES:SUBLANES + TS_RNN, :] = rx
        tail_ref[:, sl] = rx[TS_RNN - SUBLANES:TS_RNN, :]
        cw = cw_ref[:, sl]
        xr = cw[3:4, :] * rx + cb_ref[:, sl]
        for d in (1, 2, 3):
            xr = xr + cw[3 - d:4 - d, :] * xbuf_ref[SUBLANES - d:SUBLANES - d + TS_RNN, :]
        xb = xr.astype(BF16)
        r = jax.nn.sigmoid(_dot(xb, wa_ref[cb]) + ba_ref[:, sl])
        gi = jax.nn.sigmoid(_dot(xb, wx_ref[cb]) + bx_ref[:, sl])
        log_a = log_a_unit[:, sl] * r
        a = jnp.exp(log_a)
        th = jnp.tanh(log_a)
        u = jnp.sqrt(-2.0 * th / (1.0 - th)) * (gi * xr)
        a = a.reshape(TS_RNN // SUBLANES, SUBLANES, CB_RNN)
        u = u.reshape(TS_RNN // SUBLANES, SUBLANES, CB_RNN)
        for d in (1, 2, 4):
            keep = rmod >= d
            a_s = pltpu.roll(a, d, 1)
            u_s = pltpu.roll(u, d, 1)
            u = jnp.where(keep, a * u_s + u, u)
            a = jnp.where(keep, a * a_s, a)
        a_ref[...] = a.reshape(TS_RNN, CB_RNN)
        u_ref[...] = u.reshape(TS_RNN, CB_RNN)

        def group(k, hprev):
            r0 = pl.multiple_of(k * SUBLANES, SUBLANES)
            h8 = u_ref[pl.ds(r0, SUBLANES), :] + a_ref[pl.ds(r0, SUBLANES), :] * hprev
            u_ref[pl.ds(r0, SUBLANES), :] = h8
            return jnp.broadcast_to(h8[SUBLANES - 1:SUBLANES, :], (SUBLANES, CB_RNN))

        hlast = lax.fori_loop(0, TS_RNN // SUBLANES, group, hc_ref[:, sl], unroll=4)
        hc_ref[:, sl] = hlast
        o_ref[:, sl] = u_ref[...] * _gelu(ry_ref[:, sl])

    out_ref[...] = _rms(o_ref[...], g_ref[...]).astype(BF16)


def _rglru(rnn_in, cw, cb, wa, ba, wx, bx, lam, g, batch, seq_len):
    n = rnn_in.shape[0]
    nc = seq_len // TS_RNN
    nb = RNN_WIDTH // CB_RNN
    vec = pl.BlockSpec((1, RNN_WIDTH), lambda b, c: (0, 0))
    wspec = pl.BlockSpec((nb, CB_RNN, CB_RNN), lambda b, c: (0, 0, 0))
    return pl.pallas_call(
        _rglru_kernel,
        grid=(batch, nc),
        in_specs=[
            pl.BlockSpec((TS_RNN, RNN_WIDTH), lambda b, c: (b * nc + c, 0)),
            pl.BlockSpec((TS_RNN, RNN_WIDTH), lambda b, c: (b * nc + c, 1)),
            pl.BlockSpec((RNN_CONV_WIDTH, RNN_WIDTH), lambda b, c: (0, 0)),
            vec, wspec, vec, wspec, vec, vec, vec,
        ],
        out_specs=pl.BlockSpec((TS_RNN, RNN_WIDTH), lambda b, c: (b * nc + c, 0)),
        out_shape=jax.ShapeDtypeStruct((n, RNN_WIDTH), BF16),
        scratch_shapes=[
            pltpu.VMEM((SUBLANES, RNN_WIDTH), F32),
            pltpu.VMEM((SUBLANES, RNN_WIDTH), F32),
            pltpu.VMEM((TS_RNN, CB_RNN), F32),
            pltpu.VMEM((TS_RNN, CB_RNN), F32),
            pltpu.VMEM((TS_RNN, RNN_WIDTH), F32),
            pltpu.VMEM((SUBLANES + TS_RNN, CB_RNN), F32),
        ],
        compiler_params=pltpu.CompilerParams(dimension_semantics=("arbitrary", "arbitrary"),
                                             vmem_limit_bytes=VMEM_LIMIT),
        name="rglru",
    )(rnn_in, rnn_in, cw, cb, wa, ba, wx, bx, lam, g)


def _block_diag(w):
    per = CB_RNN // (RNN_WIDTH // RNN_BLOCKS)
    bd = w.shape[-1]
    w4 = w.reshape(RNN_BLOCKS // per, per, bd, bd)
    eye = jnp.eye(per, dtype=w.dtype)
    return jnp.einsum('cpij,pq->cpiqj', w4, eye).reshape(RNN_BLOCKS // per, per * bd, per * bd)


CMP_BATCHES = 8


def _cmp_tokens_kernel(x_ref, pe_ref, w1_ref, w2_ref, out_ref):
    x = x_ref[...]
    rows = x.shape[0]
    ya = _dot(x, w1_ref[0])
    yb = _dot(x, w1_ref[1])
    pterm = (_dot(jnp.broadcast_to(pe_ref[0], (SUBLANES, CHUNK2_W)), w1_ref[0])
             + _dot(jnp.broadcast_to(pe_ref[1], (SUBLANES, CHUNK2_W)), w1_ref[1]))[0:1, :]
    act = _gelu(ya + pltpu.roll(yb, rows - 1, 0) + pterm).astype(BF16)
    for e in range(2):
        tok = _dot(act[:, e * CMP_HIDDEN:(e + 1) * CMP_HIDDEN], w2_ref[...]).astype(BF16)
        out_ref[:, e] = tok.reshape(rows // N_CMP_PAD, N_CMP_PAD, LANES)


def _cmp_tokens(chunks, pe, w1p, w2, batch):
    nb = min(CMP_BATCHES, batch)
    tr = nb * N_CMP_PAD
    return pl.pallas_call(
        _cmp_tokens_kernel,
        grid=(N_CHUNK_SLABS, batch // nb),
        in_specs=[
            pl.BlockSpec((None, tr, CHUNK2_W), lambda s, i: (s, i, 0)),
            pl.BlockSpec((None, 2, 1, CHUNK2_W), lambda s, i: (s // 2, 0, 0, 0)),
            pl.BlockSpec((None, 2, CHUNK2_W, 2 * CMP_HIDDEN), lambda s, i: (s // 2, 0, 0, 0)),
            pl.BlockSpec((None, CMP_HIDDEN, LANES), lambda s, i: (s // 2, 0, 0)),
        ],
        out_specs=pl.BlockSpec((None, nb, None, 2, N_CMP_PAD, LANES), lambda s, i: (s // 2, i, s % 2, 0, 0, 0)),
        out_shape=jax.ShapeDtypeStruct((2, batch, 2, 2, N_CMP_PAD, LANES), BF16),
        compiler_params=pltpu.CompilerParams(dimension_semantics=("parallel", "parallel")),
        name="cmp_tokens",
    )(chunks, pe, w1p, w2)


def _pair_weights(w1):
    half = CMP_STRIDE * HEAD_DIM
    w = w1.reshape(2, CMP_STRIDE, HEAD_DIM, CMP_HIDDEN)
    eye = jnp.eye(2, dtype=w1.dtype)
    return jnp.einsum('hldj,ef->hledfj', w, eye).reshape(2, 2 * half, 2 * CMP_HIDDEN)


def _pair_pe(pe):
    p = pe.reshape(2, CMP_STRIDE, 1, HEAD_DIM)
    return jnp.broadcast_to(p, (2, CMP_STRIDE, 2, HEAD_DIM)).reshape(2, 1, CHUNK2_W)


N_TILE_TYPES = 4
TILE_DIAG, TILE_NEAR, TILE_WIN_LAST, TILE_NONE = 0, 1, 2, 3


def _t5_bucket_np(dist):
    n = np.maximum(dist, 0)
    max_exact = N_BUCKETS // 2
    nf = np.maximum(n, 1).astype(np.float64)
    large = max_exact + (np.log(nf / max_exact) / math.log(MAX_DISTANCE / max_exact)
                         * (N_BUCKETS - max_exact)).astype(np.int32)
    large = np.minimum(large, N_BUCKETS - 1)
    return np.where(n < max_exact, n, large).astype(np.int32)


def _bucket_constants(seq_len):
    i = np.arange(TQ)[:, None]
    j = np.arange(TK)[None, :]
    d0 = i - j
    d1 = TK + i - j
    d2 = 2 * TK + i - j
    valid = [d0 >= 0, np.ones_like(d1, bool), d2 < WINDOW, np.zeros_like(d0, bool)]
    tiles = np.stack([np.where(v, _t5_bucket_np(d), -1) for d, v in zip([d0, d1, d2, d0], valid)])
    t = np.arange(seq_len)[:, None]
    c = np.arange(N_CMP_PAD)[None, :]
    dc = t - (c * CMP_STRIDE + CMP_BLOCK - 1)
    cmp = np.where(dc >= 0, _t5_bucket_np(dc), -1)
    return tiles.astype(np.int32), cmp.astype(np.int32)


def _bias_kernel(tile_buckets, rb_ref, tb_ref, cbk_ref, tiles_ref, cmpb_ref):
    h = pl.program_id(0)

    def lookup(bkt, buckets):
        acc = jnp.full(bkt.shape, NEG_INF, F32)
        for b in buckets:
            acc = jnp.where(bkt == b, rb_ref[b, h], acc)
        return acc

    for t in range(N_TILE_TYPES):
        tiles_ref[t] = lookup(tb_ref[t], tile_buckets[t])
    cmpb_ref[...] = lookup(cbk_ref[...], range(N_BUCKETS))


def _bias_tiles(rel_bias, seq_len):
    tb, cbk = _bucket_constants(seq_len)
    tile_buckets = tuple(tuple(int(b) for b in np.unique(t[t >= 0])) for t in tb)
    r = HEADS_PER_GROUP
    return pl.pallas_call(
        functools.partial(_bias_kernel, tile_buckets),
        grid=(N_Q_HEADS,),
        in_specs=[
            pl.BlockSpec(memory_space=pltpu.SMEM),
            pl.BlockSpec((N_TILE_TYPES, TQ, TK), lambda h: (0, 0, 0)),
            pl.BlockSpec((seq_len, N_CMP_PAD), lambda h: (0, 0)),
        ],
        out_specs=[
            pl.BlockSpec((None, N_TILE_TYPES, None, TQ, TK), lambda h: (h // r, 0, h % r, 0, 0)),
            pl.BlockSpec((None, None, seq_len, N_CMP_PAD), lambda h: (h // r, h % r, 0, 0)),
        ],
        out_shape=[
            jax.ShapeDtypeStruct((N_KV_GROUPS, N_TILE_TYPES, r, TQ, TK), F32),
            jax.ShapeDtypeStruct((N_KV_GROUPS, r, seq_len, N_CMP_PAD), F32),
        ],
        compiler_params=pltpu.CompilerParams(dimension_semantics=("parallel",)),
        name="bias_tiles",
    )(rel_bias, jnp.asarray(tb), jnp.asarray(cbk))


SIDE_FAR_HI, SIDE_FAR_LO, SIDE_OFF = 32, 33, 34
KSIDE_NEAR, KSIDE_FAR, KSIDE_OFF = 0, 1, 2


def _key_side_constants(seq_len):
    s = np.arange(seq_len)
    side = np.zeros((3, seq_len, LANES), np.float32)
    side[:, s, s // SEL_BLOCK] = 1.0
    side[KSIDE_FAR, :, SIDE_FAR_HI] = 1.0
    side[KSIDE_FAR, :, SIDE_FAR_LO] = 1.0
    side[KSIDE_OFF, :, SIDE_OFF] = 1.0
    return side


def _overlap_constants(seq_len):
    n_sb = seq_len // SEL_BLOCK
    c = np.arange(N_CMP_PAD)[None, :]
    j = np.arange(n_sb)[:, None]
    lo = np.maximum(c * CMP_STRIDE, j * SEL_BLOCK)
    hi = np.minimum(c * CMP_STRIDE + CMP_BLOCK, (j + 1) * SEL_BLOCK)
    ov = np.maximum(hi - lo, 0) / CMP_BLOCK
    ov[:, N_CMP_PAD - 1] = 0.0
    return ov.astype(np.float32)


def _nsa_kernel(n_sb, n_slots, rb_ref, *refs):
    per_slot_in = 8
    slot_in = [refs[s * per_slot_in:(s + 1) * per_slot_in] for s in range(n_slots)]
    shared = refs[n_slots * per_slot_in:n_slots * per_slot_in + 5]
    out_ref = refs[n_slots * per_slot_in + 5]
    scratch = refs[n_slots * per_slot_in + 6:]
    g = pl.program_id(0)
    qi = pl.program_id(1)
    slots = [_nsa_slot(n_sb, g, qi, rb_ref, *slot_in[s], *shared, out_ref.at[s], *scratch[4 * s:4 * s + 4])
             for s in range(n_slots)]

    def far_pairs(idx, carry):
        for far_pair, _ in slots:
            far_pair(idx)
        return carry

    lax.fori_loop(0, jnp.maximum(qi, 1) // 2, far_pairs, 0)
    for _, finish in slots:
        finish()


def _nsa_slot(n_sb, g, qi, rb_ref, q_ref, kc_ref, vc_ref, ks_ref, vs_ref, kw_ref, vw_ref, gate_ref, bg_ref,
              tiles_ref, cmpb_ref, ov_ref, kside_ref, out_ref,
              qs_ref, side_ref, m_ref, acc_ref):
    par = g % 2
    r_heads = HEADS_PER_GROUP
    rows = r_heads * TQ
    lane = lax.broadcasted_iota(jnp.int32, (TQ, LANES), 1)
    in_half = (lane // HEAD_DIM) == par

    for blk in range(2):
        qb = q_ref[:, blk * LANES:(blk + 1) * LANES].astype(F32)
        qr = pltpu.roll(qb, HEAD_DIM, 1)
        for hh in range(2):
            r = blk * 2 + hh
            src = jnp.where(par == hh, qb, qr)
            qs_ref[r * TQ:(r + 1) * TQ, :] = (jnp.where(in_half, src, 0.0) * SCALE).astype(BF16)

    gs = jax.nn.sigmoid(gate_ref[...] + bg_ref[...])
    gates = [[jnp.sum(jnp.where(lane == N_BRANCH * (r_heads * g + r) + br, gs, 0.0), axis=-1, keepdims=True)
              for br in range(N_BRANCH)] for r in range(r_heads)]

    def tile_rows(kj):
        return pl.ds(pl.multiple_of(kj * TK, TK), TK)

    ones = jnp.ones((TK, LANES), BF16)

    def scores(q, key_tiles, tile_types):
        keys = jnp.concatenate(key_tiles, axis=0)
        s = jnp.concatenate([_dot_nt(q[c:c + ROW_SPLIT], keys) for c in range(0, rows, ROW_SPLIT)], axis=0)
        out = []
        for t, kind in enumerate(tile_types):
            st = s[:, t * TK:(t + 1) * TK]
            if kind is not None:
                st = (st.reshape(r_heads, TQ, TK) + tiles_ref[kind]).reshape(rows, TK)
            out.append(st)
        return out

    def weighted(ps, v_ref, kjs):
        p = jnp.concatenate([x.astype(BF16) for x in ps], axis=1)
        v = jnp.concatenate([jnp.concatenate([v_ref[tile_rows(kj), :], ones], axis=1) for kj in kjs], axis=0)
        return jnp.concatenate([_dot(p[c:c + ROW_SPLIT], v) for c in range(0, rows, ROW_SPLIT)], axis=0)

    def row_max(ss):
        return jnp.max(functools.reduce(jnp.maximum, ss), axis=-1, keepdims=True)

    def normalised(acc):
        return acc[:, 0:LANES] * (1.0 / acc[:, LANES:2 * LANES])

    kj1 = jnp.maximum(qi - 1, 0)
    near_type = jnp.where(qi >= 1, TILE_NEAR, TILE_NONE)

    kj2 = jnp.maximum(qi - 2, 0)
    last_type = jnp.where(qi >= 2, TILE_WIN_LAST, TILE_NONE)
    sw = scores(qs_ref[...], [kw_ref[tile_rows(kj), :] for kj in (qi, kj1, kj2)],
                (TILE_DIAG, near_type, last_type))
    mw = row_max(sw)
    o_win = normalised(weighted([jnp.exp(s - mw) for s in sw], vw_ref, (qi, kj1, kj2)))

    trow = qi * TQ + lax.broadcasted_iota(jnp.int32, (TQ, 1), 0)
    has_any = jnp.where(trow >= CMP_BLOCK - 1, 1.0, 0.0)
    kc = kc_ref[...]
    vc = vc_ref[...]
    psum = jnp.zeros((TQ, N_CMP_PAD), F32)
    o_cmp = []
    for r in range(r_heads):
        s = _dot_nt(qs_ref[r * TQ:(r + 1) * TQ, :], kc) + cmpb_ref[r]
        e = jnp.exp(s - jnp.max(s, axis=-1, keepdims=True))
        p = e * (has_any / jnp.sum(e, axis=-1, keepdims=True))
        psum = psum + p
        o_cmp.append(_dot(p.astype(BF16), vc))

    p_hi = psum.astype(BF16)
    p_lo = (psum - p_hi.astype(F32)).astype(BF16)
    ov = ov_ref[...]
    imp = _dot_nt(ov, p_hi) + _dot_nt(ov, p_lo)
    jrow = lax.broadcasted_iota(jnp.int32, (n_sb, TQ), 0)
    tq = qi * TQ + lax.broadcasted_iota(jnp.int32, (n_sb, TQ), 1)
    cur = tq // SEL_BLOCK
    valid = jrow <= cur
    forced = jnp.where(jrow == 0, 1.0, jnp.where(jrow == cur, 1.0, jnp.where(jrow == cur - 1, 1.0, 0.0)))
    score = jnp.where(valid, imp + FORCE_BONUS * forced, -1.0)
    rank = jnp.zeros((n_sb, TQ), F32)
    for jp in range(n_sb):
        other = jnp.broadcast_to(score[jp:jp + 1, :], (n_sb, TQ))
        before = jnp.where(jrow > jp, 1.0, 0.0)
        rank = rank + jnp.where(other > score, 1.0, jnp.where(other == score, before, 0.0))
    chosen = jnp.where(valid, jnp.where(rank < float(min(N_SELECT, n_sb)), 0.0, NEG_INF), NEG_INF)

    srow = lax.broadcasted_iota(jnp.int32, (LANES - n_sb, TQ), 0) + n_sb
    side_t = jnp.concatenate([chosen, jnp.where(srow == SIDE_OFF, NEG_INF, 0.0)], axis=0)
    side = side_t.T
    for r in range(r_heads):
        far = jnp.full((TQ, LANES), rb_ref[N_BUCKETS - 1, r_heads * g + r], F32)
        far_hi = far.astype(BF16).astype(F32)
        side_r = jnp.where(lane == SIDE_FAR_HI, far_hi, jnp.where(lane == SIDE_FAR_LO, far - far_hi, side))
        side_ref[r * TQ:(r + 1) * TQ, :] = side_r.astype(BF16)

    def key_side(kj, side_kind):
        return jnp.concatenate([ks_ref[tile_rows(kj), :], kside_ref[side_kind, tile_rows(kj), :]], axis=1)

    def q_side():
        return jnp.concatenate([qs_ref[...], side_ref[...]], axis=1)

    sn = scores(q_side(), [key_side(qi, KSIDE_NEAR), key_side(kj1, KSIDE_NEAR)], (TILE_DIAG, near_type))
    m0 = row_max(sn)
    m_ref[...] = jnp.broadcast_to(m0, (rows, LANES))
    acc_ref[...] = weighted([jnp.exp(s - m0) for s in sn], vs_ref, (qi, kj1))

    def far_pair(idx):
        ka = qi - 2 - 2 * idx
        kb_raw = ka - 1
        kb = jnp.maximum(kb_raw, 0)
        sf = scores(q_side(), [key_side(ka, KSIDE_FAR),
                               key_side(kb, jnp.where(kb_raw >= 0, KSIDE_FAR, KSIDE_OFF))], (None, None))
        m_prev = m_ref[...]
        m_new = jnp.maximum(m_prev, row_max(sf))
        alpha = jnp.exp(m_prev - m_new)
        m_wide = jnp.tile(m_new, (1, TK // LANES))
        acc_ref[...] = (jnp.tile(alpha, (1, 2)) * acc_ref[...]
                        + weighted([jnp.exp(s - m_wide) for s in sf], vs_ref, (ka, kb)))
        m_ref[...] = m_new

    def placed(lo, hi):
        lo = jnp.where(par == 0, lo, pltpu.roll(lo, HEAD_DIM, 1))
        hi = jnp.where(par == 1, hi, pltpu.roll(hi, HEAD_DIM, 1))
        return jnp.where(lane < HEAD_DIM, lo, hi)

    for blk in range(2):
        part = [gates[r][0] * o_cmp[r] + gates[r][2] * o_win[r * TQ:(r + 1) * TQ, :] for r in (2 * blk, 2 * blk + 1)]
        out_ref[:, blk * LANES:(blk + 1) * LANES] = placed(*part)

    def finish():
        o_slc = normalised(acc_ref[...])
        for blk in range(2):
            part = [gates[r][1] * o_slc[r * TQ:(r + 1) * TQ, :] for r in (2 * blk, 2 * blk + 1)]
            out_ref[:, blk * LANES:(blk + 1) * LANES] += placed(*part)

    return far_pair, finish


def _nsa_attn(rel_bias, att, rnn_in, kvc, bgate, tiles, cmpb, batch, seq_len):
    n = att.shape[0]
    nq = seq_len // TQ
    n_sb = seq_len // SEL_BLOCK
    ng = N_KV_GROUPS
    r = HEADS_PER_GROUP
    ov = _overlap_constants(seq_len)
    kside = _key_side_constants(seq_len)
    qw = r * HEAD_DIM
    kcol = ATTN_WIDTH + 2 * KV_WIDTH
    ns = ATTN_SLOTS if batch % ATTN_SLOTS == 0 else 1
    gate_blk = 2 * RNN_WIDTH // GATE_PAD

    def slot_specs(s):
        seq = lambda bb: bb * ns + s
        kvb = lambda off: (lambda g, qi, bb: (seq(bb), (kcol + off * KV_WIDTH) // LANES + g // 2))
        return [
            pl.BlockSpec((TQ, qw), lambda g, qi, bb: (seq(bb) * nq + qi, g)),
            pl.BlockSpec((None, None, N_CMP_PAD, LANES), lambda g, qi, bb: (0, seq(bb) * ng + g, 0, 0)),
            pl.BlockSpec((None, None, N_CMP_PAD, LANES), lambda g, qi, bb: (1, seq(bb) * ng + g, 0, 0)),
            pl.BlockSpec((seq_len, LANES), kvb(0)),
            pl.BlockSpec((seq_len, LANES), kvb(1)),
            pl.BlockSpec((seq_len, LANES), kvb(2)),
            pl.BlockSpec((seq_len, LANES), kvb(3)),
            pl.BlockSpec((TQ, GATE_PAD), lambda g, qi, bb: (seq(bb) * nq + qi, gate_blk)),
        ]

    shared_specs = [
        pl.BlockSpec((1, GATE_PAD), lambda g, qi, bb: (0, 0)),
        pl.BlockSpec((None, N_TILE_TYPES, r, TQ, TK), lambda g, qi, bb: (g, 0, 0, 0, 0)),
        pl.BlockSpec((None, r, TQ, N_CMP_PAD), lambda g, qi, bb: (g, 0, qi, 0)),
        pl.BlockSpec((n_sb, N_CMP_PAD), lambda g, qi, bb: (0, 0)),
        pl.BlockSpec((3, seq_len, LANES), lambda g, qi, bb: (0, 0, 0)),
    ]
    slot_scratch = [
        pltpu.VMEM((r * TQ, LANES), BF16),
        pltpu.VMEM((r * TQ, LANES), BF16),
        pltpu.VMEM((r * TQ, LANES), F32),
        pltpu.VMEM((r * TQ, 2 * LANES), F32),
    ]
    outs = pl.pallas_call(
        functools.partial(_nsa_kernel, n_sb, ns),
        grid=(ng, nq, batch // ns),
        in_specs=([pl.BlockSpec(memory_space=pltpu.SMEM)]
                  + [spec for s in range(ns) for spec in slot_specs(s)] + shared_specs),
        out_specs=pl.BlockSpec((None, ns, TQ, qw), lambda g, qi, bb: (bb, 0, qi, g)),
        out_shape=jax.ShapeDtypeStruct((batch // ns, ns, seq_len, ATTN_WIDTH), F32),
        scratch_shapes=slot_scratch * ns,
        compiler_params=pltpu.CompilerParams(dimension_semantics=("arbitrary", "arbitrary", "arbitrary"),
                                             vmem_limit_bytes=VMEM_LIMIT),
        name="nsa_attn",
    )(rel_bias, *([att, kvc, kvc, att, att, att, att, rnn_in] * ns), bgate, tiles, cmpb,
      jnp.asarray(ov, BF16), jnp.asarray(kside, BF16))
    return outs.reshape(n, ATTN_WIDTH)


def kernel(x, mix_norm_g, w_in, b_gate, cmp_pe_k, cmp_pe_v, cmp_k_w1, cmp_k_w2, cmp_v_w1, cmp_v_w2, rel_bias, rnn_conv_w, rnn_conv_b, rg_a_w, rg_a_b, rg_x_w, rg_x_b, rg_lambda, attn_out_g, rnn_out_g, w_out, ffn_norm_g, w_ffn_gate, w_ffn_up, ffn_conv_w, ffn_conv_b, w_ffn_down, final_norm_g):
    batch, seq_len, d_model = x.shape
    assert d_model == D_MODEL and w_in.shape[0] == 1, "single-layer model of the stated width"
    assert seq_len // CMP_STRIDE == N_CMP_PAD and seq_len % TM_FFN == 0
    n = batch * seq_len
    ng = N_KV_GROUPS
    x2 = x.reshape(n, D_MODEL)

    w0 = w_in[0]
    gate0 = ATT_COLS
    rx0 = gate0 + N_BRANCH * N_Q_HEADS
    w_perm = jnp.concatenate(
        [w0[:, :ATT_COLS], w0[:, rx0:rx0 + 2 * RNN_WIDTH], w0[:, gate0:rx0],
         jnp.zeros((D_MODEL, GATE_PAD - N_BRANCH * N_Q_HEADS), w0.dtype)], axis=1).astype(BF16)
    att, rnn_in, chunks = _in_proj(x2, mix_norm_g[0][None], w_perm)

    pe = jnp.stack([_pair_pe(cmp_pe_k[0]), _pair_pe(cmp_pe_v[0])]).astype(BF16)
    w1p = jnp.stack([_pair_weights(cmp_k_w1[0]), _pair_weights(cmp_v_w1[0])]).astype(BF16)
    w2 = jnp.stack([jnp.tile(cmp_k_w2[0], (1, 2)), jnp.tile(cmp_v_w2[0], (1, 2))]).astype(BF16)
    kvc = _cmp_tokens(chunks, pe, w1p, w2, batch).reshape(2, batch * ng, N_CMP_PAD, LANES)

    tiles, cmpb = _bias_tiles(rel_bias, seq_len)
    bgate = jnp.pad(b_gate[0], (0, GATE_PAD - b_gate.shape[1]))[None]
    o_attn = _nsa_attn(rel_bias, att, rnn_in, kvc, bgate, tiles, cmpb, batch, seq_len)

    nr = _rglru(rnn_in, rnn_conv_w[0], rnn_conv_b[0][None],
                _block_diag(rg_a_w[0]).astype(BF16), rg_a_b[0][None],
                _block_diag(rg_x_w[0]).astype(BF16), rg_x_b[0][None],
                rg_lambda[0][None], rnn_out_g[0][None], batch, seq_len)

    h, y = _out_proj(o_attn, nr, x2, attn_out_g[0][None], ffn_norm_g[0][None], w_out[0].astype(BF16))
    out = _conv_ffn(y, h, w_ffn_gate[0].astype(BF16), w_ffn_up[0].astype(BF16), w_ffn_down[0].astype(BF16),
                    ffn_conv_w[0], ffn_conv_b[0][None], final_norm_g[None], seq_len)
    return out.reshape(batch, seq_len, D_MODEL)
```

```python
import functools
import math

import jax
import jax.numpy as jnp
import numpy as np
from jax import lax
from jax.experimental import pallas as pl
from jax.experimental.pallas import tpu as pltpu

F32 = jnp.float32
BF16 = jnp.bfloat16

D_MODEL = 2048
N_Q_HEADS = 16
N_KV_GROUPS = 4
HEADS_PER_GROUP = 4
HEAD_DIM = 64
ATTN_WIDTH = 1024
KV_WIDTH = 256
SCALE = HEAD_DIM ** -0.5
CMP_BLOCK = 32
CMP_STRIDE = 16
CMP_HIDDEN = 256
SEL_BLOCK = 64
N_SELECT = 16
WINDOW = 512
N_BRANCH = 3
RNN_WIDTH = 1024
RNN_BLOCKS = 16
RNN_CONV_WIDTH = 4
RG_LRU_C = 8.0
N_BUCKETS = 32
MAX_DISTANCE = 128
D_FF = 5632
NORM_EPS = 1e-6
NEG_INF = -1e30
FORCE_BONUS = 1e3
GELU_C0 = math.sqrt(2.0 / math.pi)
GELU_C1 = GELU_C0 * 0.044715

LANES = 128
SUBLANES = 8
VMEM_LIMIT = 56 * 1024 * 1024

TM_PROJ = 512
COL_CHUNK = 512
TM_FFN = 1024
TF_FFN = 512
FFN_ROW_BLOCKS = 2
H_PIECE = 256
TS_RNN = 512
CB_RNN = 256
TQ = 256
TK = 256
ATTN_SLOTS = 2
ROW_SPLIT = 512
N_CMP_PAD = 128
GATE_PAD = 128

ATT_COLS = ATTN_WIDTH + 6 * KV_WIDTH
RNN_COLS = 2 * RNN_WIDTH + GATE_PAD
CMP_COL0 = ATTN_WIDTH
N_CHUNK_SLABS = 2 * KV_WIDTH // LANES
CHUNK2_W = CMP_STRIDE * LANES


def _rms(x, g):
    return x * lax.rsqrt(jnp.mean(x * x, axis=-1, keepdims=True) + NORM_EPS) * g


def _gelu(x):
    return jax.nn.gelu(x, approximate=True)


def _dot(a, b):
    return jnp.dot(a, b, preferred_element_type=F32)


def _dot_nt(a, b):
    return lax.dot_general(a, b, (((1,), (1,)), ((), ())), preferred_element_type=F32)


def _dot_tn(a, b):
    return lax.dot_general(a, b, (((0,), (0,)), ((), ())), preferred_element_type=F32)


def _in_proj_kernel(x_ref, g_ref, w_ref, wr_ref, att_ref, rnn_ref, chunk_ref, stage_ref):
    xn = _rms(x_ref[...], g_ref[...]).astype(BF16)
    for c0 in range(0, ATT_COLS, COL_CHUNK):
        res = _dot(xn, w_ref[:, c0:c0 + COL_CHUNK])
        att_ref[:, c0:c0 + COL_CHUNK] = res.astype(BF16)
        if c0 == CMP_COL0:
            for q in range(N_CHUNK_SLABS):
                stage_ref[q] = res[:, q * LANES:(q + 1) * LANES]
            for l in range(CMP_STRIDE):
                for q in range(N_CHUNK_SLABS):
                    tok = stage_ref[q, pl.ds(l, TM_PROJ // CMP_STRIDE, stride=CMP_STRIDE), :]
                    chunk_ref[q, :, l * LANES:(l + 1) * LANES] = tok.astype(BF16)
    for c0 in range(0, RNN_COLS, COL_CHUNK):
        c1 = min(c0 + COL_CHUNK, RNN_COLS)
        rnn_ref[:, c0:c1] = _dot(xn, wr_ref[:, c0:c1])


def _in_proj(x2, g, w, w_rnn):
    n = x2.shape[0]
    assert CMP_COL0 % COL_CHUNK == 0 and N_CHUNK_SLABS * LANES == COL_CHUNK
    return pl.pallas_call(
        _in_proj_kernel,
        grid=(n // TM_PROJ,),
        in_specs=[
            pl.BlockSpec((TM_PROJ, D_MODEL), lambda i: (i, 0)),
            pl.BlockSpec((1, D_MODEL), lambda i: (0, 0)),
            pl.BlockSpec((D_MODEL, ATT_COLS), lambda i: (0, 0), pipeline_mode=pl.Buffered(1)),
            pl.BlockSpec((D_MODEL, RNN_COLS), lambda i: (0, 0), pipeline_mode=pl.Buffered(1)),
        ],
        out_specs=[
            pl.BlockSpec((TM_PROJ, ATT_COLS), lambda i: (i, 0)),
            pl.BlockSpec((TM_PROJ, RNN_COLS), lambda i: (i, 0)),
            pl.BlockSpec((N_CHUNK_SLABS, TM_PROJ // CMP_STRIDE, CHUNK2_W), lambda i: (0, i, 0)),
        ],
        out_shape=[
            jax.ShapeDtypeStruct((n, ATT_COLS), BF16),
            jax.ShapeDtypeStruct((n, RNN_COLS), F32),
            jax.ShapeDtypeStruct((N_CHUNK_SLABS, n // CMP_STRIDE, CHUNK2_W), BF16),
        ],
        scratch_shapes=[pltpu.VMEM((N_CHUNK_SLABS, TM_PROJ, LANES), F32)],
        compiler_params=pltpu.CompilerParams(dimension_semantics=("parallel",), vmem_limit_bytes=VMEM_LIMIT),
        name="in_proj",
    )(x2, g, w, w_rnn)


def _out_proj_kernel(oa_ref, nr_ref, x_ref, ga_ref, gf_ref, w_ref, h_ref, y_ref):
    na = _rms(oa_ref[...], ga_ref[...]).astype(BF16)
    nr = nr_ref[...]
    for c0 in range(0, D_MODEL, COL_CHUNK):
        sl = slice(c0, c0 + COL_CHUNK)
        h_ref[:, sl] = (x_ref[:, sl] + _dot(na, w_ref[0:ATTN_WIDTH, sl])
                        + _dot(nr, w_ref[ATTN_WIDTH:D_MODEL, sl]))
    y_ref[...] = _rms(h_ref[...], gf_ref[...]).astype(BF16)


def _out_proj(o_attn, nr, x2, ga, gf, w):
    n = x2.shape[0]
    return pl.pallas_call(
        _out_proj_kernel,
        grid=(n // TM_PROJ,),
        in_specs=[
            pl.BlockSpec((TM_PROJ, ATTN_WIDTH), lambda i: (i, 0)),
            pl.BlockSpec((TM_PROJ, RNN_WIDTH), lambda i: (i, 0)),
            pl.BlockSpec((TM_PROJ, D_MODEL), lambda i: (i, 0)),
            pl.BlockSpec((1, ATTN_WIDTH), lambda i: (0, 0)),
            pl.BlockSpec((1, D_MODEL), lambda i: (0, 0)),
            pl.BlockSpec((D_MODEL, D_MODEL), lambda i: (0, 0), pipeline_mode=pl.Buffered(1)),
        ],
        out_specs=[
            pl.BlockSpec((TM_PROJ, D_MODEL), lambda i: (i, 0)),
            pl.BlockSpec((TM_PROJ, D_MODEL), lambda i: (i, 0)),
        ],
        out_shape=[
            jax.ShapeDtypeStruct((n, D_MODEL), F32),
            jax.ShapeDtypeStruct((n, D_MODEL), BF16),
        ],
        compiler_params=pltpu.CompilerParams(dimension_semantics=("parallel",), vmem_limit_bytes=VMEM_LIMIT),
        name="out_proj",
    )(o_attn, nr, x2, ga, gf, w)


def _ffn_kernel(tiles_per_seq, y_ref, h_ref, wg_ref, wu_ref, wd_ref, cw_ref, cb_ref, gf_ref, out_ref, carry_ref):
    i = pl.program_id(0)
    j = pl.program_id(1)
    nj = pl.num_programs(1)

    @pl.when((i % tiles_per_seq) == 0)
    def _():
        carry_ref[j] = jnp.zeros((SUBLANES, TF_FFN), F32)

    @pl.when(j == 0)
    def _():
        out_ref[...] = jnp.zeros(out_ref.shape, F32)

    @pl.when(j < D_MODEL // H_PIECE)
    def _():
        cols = pl.ds(pl.multiple_of(j * H_PIECE, H_PIECE), H_PIECE)
        out_ref[:, cols] += h_ref[...]

    row = lax.broadcasted_iota(jnp.int32, (SUBLANES, TF_FFN), 0)
    cw = cw_ref[...]
    cb = cb_ref[...]
    prev = carry_ref[j]
    rows_blk = TM_FFN // FFN_ROW_BLOCKS
    for blk in range(FFN_ROW_BLOCKS):
        sl = slice(blk * rows_blk, (blk + 1) * rows_blk)
        y = y_ref[sl, :]
        gt = _dot(y, wg_ref[...])
        p1 = prev[SUBLANES - 1:SUBLANES, :]
        p2 = prev[SUBLANES - 2:SUBLANES - 1, :]
        r1 = pltpu.roll(gt, 1, 0)
        r2 = pltpu.roll(gt, 2, 0)
        h1 = jnp.where(row == 0, p1, r1[0:SUBLANES])
        h2 = jnp.where(row == 0, p2, jnp.where(row == 1, p1, r2[0:SUBLANES]))
        s1 = jnp.concatenate([h1, r1[SUBLANES:]], axis=0)
        s2 = jnp.concatenate([h2, r2[SUBLANES:]], axis=0)
        conv = cw[2:3, :] * gt + cw[1:2, :] * s1 + cw[0:1, :] * s2 + cb
        inner = conv * (GELU_C0 + GELU_C1 * (conv * conv))
        hx = 0.5 * conv
        u = hx + hx * jnp.tanh(inner)
        z = (u * _dot(y, wu_ref[...])).astype(BF16)
        out_ref[sl, :] += _dot(z, wd_ref[...])
        prev = gt[rows_blk - SUBLANES:rows_blk, :]
    carry_ref[j] = prev

    @pl.when(j == nj - 1)
    def _():
        out_ref[...] = _rms(out_ref[...], gf_ref[...])


def _conv_ffn(y, h, wg, wu, wd, cw, cb, gf, seq_len):
    n = y.shape[0]
    nj = D_FF // TF_FFN
    return pl.pallas_call(
        functools.partial(_ffn_kernel, seq_len // TM_FFN),
        grid=(n // TM_FFN, nj),
        in_specs=[
            pl.BlockSpec((TM_FFN, D_MODEL), lambda i, j: (i, 0)),
            pl.BlockSpec((TM_FFN, H_PIECE), lambda i, j: (i, jnp.minimum(j, D_MODEL // H_PIECE - 1))),
            pl.BlockSpec((D_MODEL, TF_FFN), lambda i, j: (0, j)),
            pl.BlockSpec((D_MODEL, TF_FFN), lambda i, j: (0, j)),
            pl.BlockSpec((TF_FFN, D_MODEL), lambda i, j: (j, 0)),
            pl.BlockSpec((3, TF_FFN), lambda i, j: (0, j)),
            pl.BlockSpec((1, TF_FFN), lambda i, j: (0, j)),
            pl.BlockSpec((1, D_MODEL), lambda i, j: (0, 0)),
        ],
        out_specs=pl.BlockSpec((TM_FFN, D_MODEL), lambda i, j: (i, 0)),
        out_shape=jax.ShapeDtypeStruct((n, D_MODEL), F32),
        scratch_shapes=[pltpu.VMEM((nj, SUBLANES, TF_FFN), F32)],
        compiler_params=pltpu.CompilerParams(dimension_semantics=("arbitrary", "arbitrary"),
                                             vmem_limit_bytes=VMEM_LIMIT),
        name="conv_ffn",
    )(y, h, wg, wu, wd, cw, cb, gf)


def _rglru_kernel(rx_ref, ry_ref, cw_ref, cb_ref, wa_ref, ba_ref, wx_ref, bx_ref, lam_ref, g_ref,
                  out_ref, tail_ref, hc_ref, a_ref, u_ref, xbuf_ref):
    c = pl.program_id(1)

    @pl.when(c == 0)
    def _():
        tail_ref[...] = jnp.zeros(tail_ref.shape, F32)
        hc_ref[...] = jnp.zeros(hc_ref.shape, F32)

    rmod = lax.broadcasted_iota(jnp.int32, (TS_RNN // SUBLANES, SUBLANES, CB_RNN), 1)
    lam = lam_ref[...]
    nlam = -lam
    softplus = jnp.maximum(nlam, 0.0) + jnp.log1p(jnp.exp(-jnp.abs(nlam)))
    log_a_unit = -RG_LRU_C * softplus

    for cb in range(RNN_WIDTH // CB_RNN):
        sl = slice(cb * CB_RNN, (cb + 1) * CB_RNN)
        rx = rx_ref[:, sl]
        xbuf_ref[0:SUBLANES, :] = tail_ref[:, sl]
        xbuf_ref[SUBLANES:SUBLANES + TS_RNN, :] = rx
        tail_ref[:, sl] = rx[TS_RNN - SUBLANES:TS_RNN, :]
        cw = cw_ref[:, sl]
        xr = cw[3:4, :] * rx + cb_ref[:, sl]
        for d in (1, 2, 3):
            xr = xr + cw[3 - d:4 - d, :] * xbuf_ref[SUBLANES - d:SUBLANES - d + TS_RNN, :]
        xb = xr.astype(BF16)
        r = jax.nn.sigmoid(_dot(xb, wa_ref[cb]) + ba_ref[:, sl])
        gi = jax.nn.sigmoid(_dot(xb, wx_ref[cb]) + bx_ref[:, sl])
        log_a = log_a_unit[:, sl] * r
        a = jnp.exp(log_a)
        th = jnp.tanh(log_a)
        w = -2.0 * th / (1.0 - th)
        u = jnp.where(w > 0.0, w * lax.rsqrt(w), 0.0) * (gi * xr)
        a = a.reshape(TS_RNN // SUBLANES, SUBLANES, CB_RNN)
        u = u.reshape(TS_RNN // SUBLANES, SUBLANES, CB_RNN)
        for d in (1, 2, 4):
            keep = rmod >= d
            a_s = pltpu.roll(a, d, 1)
            u_s = pltpu.roll(u, d, 1)
            u = jnp.where(keep, a * u_s + u, u)
            a = jnp.where(keep, a * a_s, a)
        a_ref[:, sl] = a.reshape(TS_RNN, CB_RNN)
        u_ref[:, sl] = u.reshape(TS_RNN, CB_RNN)

    def group(k, hprev):
        r0 = pl.multiple_of(k * SUBLANES, SUBLANES)
        h8 = u_ref[pl.ds(r0, SUBLANES), :] + a_ref[pl.ds(r0, SUBLANES), :] * hprev
        u_ref[pl.ds(r0, SUBLANES), :] = h8
        return jnp.broadcast_to(h8[SUBLANES - 1:SUBLANES, :], (SUBLANES, RNN_WIDTH))

    hc_ref[...] = lax.fori_loop(0, TS_RNN // SUBLANES, group, hc_ref[...], unroll=4)
    out_ref[...] = _rms(u_ref[...] * _gelu(ry_ref[...]), g_ref[...]).astype(BF16)


def _rglru(rnn_in, cw, cb, wa, ba, wx, bx, lam, g, batch, seq_len):
    n = rnn_in.shape[0]
    nc = seq_len // TS_RNN
    nb = RNN_WIDTH // CB_RNN
    vec = pl.BlockSpec((1, RNN_WIDTH), lambda b, c: (0, 0))
    wspec = pl.BlockSpec((nb, CB_RNN, CB_RNN), lambda b, c: (0, 0, 0))
    return pl.pallas_call(
        _rglru_kernel,
        grid=(batch, nc),
        in_specs=[
            pl.BlockSpec((TS_RNN, RNN_WIDTH), lambda b, c: (b * nc + c, 0)),
            pl.BlockSpec((TS_RNN, RNN_WIDTH), lambda b, c: (b * nc + c, 1)),
            pl.BlockSpec((RNN_CONV_WIDTH, RNN_WIDTH), lambda b, c: (0, 0)),
            vec, wspec, vec, wspec, vec, vec, vec,
        ],
        out_specs=pl.BlockSpec((TS_RNN, RNN_WIDTH), lambda b, c: (b * nc + c, 0)),
        out_shape=jax.ShapeDtypeStruct((n, RNN_WIDTH), BF16),
        scratch_shapes=[
            pltpu.VMEM((SUBLANES, RNN_WIDTH), F32),
            pltpu.VMEM((SUBLANES, RNN_WIDTH), F32),
            pltpu.VMEM((TS_RNN, RNN_WIDTH), F32),
            pltpu.VMEM((TS_RNN, RNN_WIDTH), F32),
            pltpu.VMEM((SUBLANES + TS_RNN, CB_RNN), F32),
        ],
        compiler_params=pltpu.CompilerParams(dimension_semantics=("arbitrary", "arbitrary"),
                                             vmem_limit_bytes=VMEM_LIMIT),
        name="rglru",
    )(rnn_in, rnn_in, cw, cb, wa, ba, wx, bx, lam, g)


def _block_diag(w):
    per = CB_RNN // (RNN_WIDTH // RNN_BLOCKS)
    bd = w.shape[-1]
    w4 = w.reshape(RNN_BLOCKS // per, per, bd, bd)
    eye = jnp.eye(per, dtype=w.dtype)
    return jnp.einsum('cpij,pq->cpiqj', w4, eye).reshape(RNN_BLOCKS // per, per * bd, per * bd)


CMP_BATCHES = 8


def _cmp_tokens_kernel(x_ref, pe_ref, w1_ref, w2_ref, out_ref):
    x = x_ref[...]
    rows = x.shape[0]
    ya = _dot(x, w1_ref[0])
    yb = _dot(x, w1_ref[1])
    pterm = (_dot(jnp.broadcast_to(pe_ref[0], (SUBLANES, CHUNK2_W)), w1_ref[0])
             + _dot(jnp.broadcast_to(pe_ref[1], (SUBLANES, CHUNK2_W)), w1_ref[1]))[0:1, :]
    act = _gelu(ya + pltpu.roll(yb, rows - 1, 0) + pterm).astype(BF16)
    for e in range(2):
        tok = _dot(act[:, e * CMP_HIDDEN:(e + 1) * CMP_HIDDEN], w2_ref[...]).astype(BF16)
        out_ref[:, e] = tok.reshape(rows // N_CMP_PAD, N_CMP_PAD, LANES)


def _cmp_tokens(chunks, pe, w1p, w2, batch):
    nb = min(CMP_BATCHES, batch)
    tr = nb * N_CMP_PAD
    return pl.pallas_call(
        _cmp_tokens_kernel,
        grid=(N_CHUNK_SLABS, batch // nb),
        in_specs=[
            pl.BlockSpec((None, tr, CHUNK2_W), lambda s, i: (s, i, 0)),
            pl.BlockSpec((None, 2, 1, CHUNK2_W), lambda s, i: (s // 2, 0, 0, 0)),
            pl.BlockSpec((None, 2, CHUNK2_W, 2 * CMP_HIDDEN), lambda s, i: (s // 2, 0, 0, 0)),
            pl.BlockSpec((None, CMP_HIDDEN, LANES), lambda s, i: (s // 2, 0, 0)),
        ],
        out_specs=pl.BlockSpec((None, nb, None, 2, N_CMP_PAD, LANES), lambda s, i: (s // 2, i, s % 2, 0, 0, 0)),
        out_shape=jax.ShapeDtypeStruct((2, batch, 2, 2, N_CMP_PAD, LANES), BF16),
        compiler_params=pltpu.CompilerParams(dimension_semantics=("parallel", "parallel")),
        name="cmp_tokens",
    )(chunks, pe, w1p, w2)


def _pair_weights(w1):
    half = CMP_STRIDE * HEAD_DIM
    w = w1.reshape(2, CMP_STRIDE, HEAD_DIM, CMP_HIDDEN)
    eye = jnp.eye(2, dtype=w1.dtype)
    return jnp.einsum('hldj,ef->hledfj', w, eye).reshape(2, 2 * half, 2 * CMP_HIDDEN)


def _pair_pe(pe):
    p = pe.reshape(2, CMP_STRIDE, 1, HEAD_DIM)
    return jnp.broadcast_to(p, (2, CMP_STRIDE, 2, HEAD_DIM)).reshape(2, 1, CHUNK2_W)


N_TILE_TYPES = 4
TILE_DIAG, TILE_NEAR, TILE_WIN_LAST, TILE_NONE = 0, 1, 2, 3


def _t5_bucket_np(dist):
    n = np.maximum(dist, 0)
    max_exact = N_BUCKETS // 2
    nf = np.maximum(n, 1).astype(np.float64)
    large = max_exact + (np.log(nf / max_exact) / math.log(MAX_DISTANCE / max_exact)
                         * (N_BUCKETS - max_exact)).astype(np.int32)
    large = np.minimum(large, N_BUCKETS - 1)
    return np.where(n < max_exact, n, large).astype(np.int32)


def _bucket_constants(seq_len):
    i = np.arange(TQ)[:, None]
    j = np.arange(TK)[None, :]
    d0 = i - j
    d1 = TK + i - j
    d2 = 2 * TK + i - j
    valid = [d0 >= 0, np.ones_like(d1, bool), d2 < WINDOW, np.zeros_like(d0, bool)]
    tiles = np.stack([np.where(v, _t5_bucket_np(d), -1) for d, v in zip([d0, d1, d2, d0], valid)])
    t = np.arange(seq_len)[:, None]
    c = np.arange(N_CMP_PAD)[None, :]
    dc = t - (c * CMP_STRIDE + CMP_BLOCK - 1)
    cmp = np.where(dc >= 0, _t5_bucket_np(dc), -1)
    return tiles.astype(np.int32), cmp.astype(np.int32)


def _bias_kernel(tile_buckets, rb_ref, tb_ref, cbk_ref, tiles_ref, cmpb_ref):
    h = pl.program_id(0)

    def lookup(bkt, buckets):
        acc = jnp.full(bkt.shape, NEG_INF, F32)
        for b in buckets:
            acc = jnp.where(bkt == b, rb_ref[b, h], acc)
        return acc

    for t in range(N_TILE_TYPES):
        tiles_ref[t] = lookup(tb_ref[t], tile_buckets[t])
    cmpb_ref[...] = lookup(cbk_ref[...], range(N_BUCKETS))


def _bias_tiles(rel_bias, seq_len):
    tb, cbk = _bucket_constants(seq_len)
    tile_buckets = tuple(tuple(int(b) for b in np.unique(t[t >= 0])) for t in tb)
    r = HEADS_PER_GROUP
    return pl.pallas_call(
        functools.partial(_bias_kernel, tile_buckets),
        grid=(N_Q_HEADS,),
        in_specs=[
            pl.BlockSpec(memory_space=pltpu.SMEM),
            pl.BlockSpec((N_TILE_TYPES, TQ, TK), lambda h: (0, 0, 0)),
            pl.BlockSpec((seq_len, N_CMP_PAD), lambda h: (0, 0)),
        ],
        out_specs=[
            pl.BlockSpec((None, N_TILE_TYPES, None, TQ, TK), lambda h: (h // r, 0, h % r, 0, 0)),
            pl.BlockSpec((None, None, seq_len, N_CMP_PAD), lambda h: (h // r, h % r, 0, 0)),
        ],
        out_shape=[
            jax.ShapeDtypeStruct((N_KV_GROUPS, N_TILE_TYPES, r, TQ, TK), F32),
            jax.ShapeDtypeStruct((N_KV_GROUPS, r, seq_len, N_CMP_PAD), F32),
        ],
        compiler_params=pltpu.CompilerParams(dimension_semantics=("parallel",)),
        name="bias_tiles",
    )(rel_bias, jnp.asarray(tb), jnp.asarray(cbk))


SIDE_FAR_HI, SIDE_FAR_LO, SIDE_OFF = 32, 33, 34
KSIDE_NEAR, KSIDE_FAR, KSIDE_OFF = 0, 1, 2


def _key_side_constants(seq_len):
    s = np.arange(seq_len)
    side = np.zeros((3, seq_len, LANES), np.float32)
    side[:, s, s // SEL_BLOCK] = 1.0
    side[KSIDE_FAR, :, SIDE_FAR_HI] = 1.0
    side[KSIDE_FAR, :, SIDE_FAR_LO] = 1.0
    side[KSIDE_OFF, :, SIDE_OFF] = 1.0
    return side


def _overlap_constants(seq_len):
    n_sb = seq_len // SEL_BLOCK
    c = np.arange(N_CMP_PAD)[None, :]
    j = np.arange(n_sb)[:, None]
    lo = np.maximum(c * CMP_STRIDE, j * SEL_BLOCK)
    hi = np.minimum(c * CMP_STRIDE + CMP_BLOCK, (j + 1) * SEL_BLOCK)
    ov = np.maximum(hi - lo, 0) / CMP_BLOCK
    ov[:, N_CMP_PAD - 1] = 0.0
    return ov.astype(np.float32)


def _nsa_kernel(n_sb, n_slots, rb_ref, *refs):
    per_slot_in = 8
    slot_in = [refs[s * per_slot_in:(s + 1) * per_slot_in] for s in range(n_slots)]
    shared = refs[n_slots * per_slot_in:n_slots * per_slot_in + 5]
    out_ref = refs[n_slots * per_slot_in + 5]
    scratch = refs[n_slots * per_slot_in + 6:]
    g = pl.program_id(0)
    qi = pl.program_id(1)
    slots = [_nsa_slot(n_sb, g, qi, rb_ref, *slot_in[s], *shared, out_ref.at[s], *scratch[4 * s:4 * s + 4])
             for s in range(n_slots)]

    def far_pairs(idx, carry):
        for far_pair, _ in slots:
            far_pair(idx)
        return carry

    lax.fori_loop(0, jnp.maximum(qi, 1) // 2, far_pairs, 0)
    for _, finish in slots:
        finish()


def _nsa_slot(n_sb, g, qi, rb_ref, q_ref, kc_ref, vc_ref, ks_ref, vs_ref, kw_ref, vw_ref, gate_ref, bg_ref,
              tiles_ref, cmpb_ref, ov_ref, kside_ref, out_ref,
              qs_ref, side_ref, m_ref, acc_ref):
    par = g % 2
    r_heads = HEADS_PER_GROUP
    rows = r_heads * TQ
    lane = lax.broadcasted_iota(jnp.int32, (TQ, LANES), 1)
    in_half = (lane // HEAD_DIM) == par

    for blk in range(2):
        qb = q_ref[:, blk * LANES:(blk + 1) * LANES].astype(F32)
        qr = pltpu.roll(qb, HEAD_DIM, 1)
        for hh in range(2):
            r = blk * 2 + hh
            src = jnp.where(par == hh, qb, qr)
            qs_ref[r * TQ:(r + 1) * TQ, :] = (jnp.where(in_half, src, 0.0) * SCALE).astype(BF16)

    gs = jax.nn.sigmoid(gate_ref[...] + bg_ref[...])
    gates = [[jnp.sum(jnp.where(lane == N_BRANCH * (r_heads * g + r) + br, gs, 0.0), axis=-1, keepdims=True)
              for br in range(N_BRANCH)] for r in range(r_heads)]

    def tile_rows(kj):
        return pl.ds(pl.multiple_of(kj * TK, TK), TK)

    ones = jnp.ones((TK, LANES), BF16)

    def scores(q, key_tiles, tile_types):
        keys = jnp.concatenate(key_tiles, axis=0)
        s = jnp.concatenate([_dot_nt(q[c:c + ROW_SPLIT], keys) for c in range(0, rows, ROW_SPLIT)], axis=0)
        out = []
        for t, kind in enumerate(tile_types):
            st = s[:, t * TK:(t + 1) * TK]
            if kind is not None:
                st = (st.reshape(r_heads, TQ, TK) + tiles_ref[kind]).reshape(rows, TK)
            out.append(st)
        return out

    def weighted(ps, v_ref, kjs):
        p = jnp.concatenate([x.astype(BF16) for x in ps], axis=1)
        v = jnp.concatenate([jnp.concatenate([v_ref[tile_rows(kj), :], ones], axis=1) for kj in kjs], axis=0)
        return jnp.concatenate([_dot(p[c:c + ROW_SPLIT], v) for c in range(0, rows, ROW_SPLIT)], axis=0)

    def row_max(ss):
        return jnp.max(functools.reduce(jnp.maximum, ss), axis=-1, keepdims=True)

    def normalised(acc):
        return acc[:, 0:LANES] * (1.0 / acc[:, LANES:2 * LANES])

    kj1 = jnp.maximum(qi - 1, 0)
    near_type = jnp.where(qi >= 1, TILE_NEAR, TILE_NONE)

    kj2 = jnp.maximum(qi - 2, 0)
    last_type = jnp.where(qi >= 2, TILE_WIN_LAST, TILE_NONE)
    sw = scores(qs_ref[...], [kw_ref[tile_rows(kj), :] for kj in (qi, kj1, kj2)],
                (TILE_DIAG, near_type, last_type))
    mw = row_max(sw)
    o_win = normalised(weighted([jnp.exp(s - mw) for s in sw], vw_ref, (qi, kj1, kj2)))

    trow = qi * TQ + lax.broadcasted_iota(jnp.int32, (TQ, 1), 0)
    has_any = jnp.where(trow >= CMP_BLOCK - 1, 1.0, 0.0)
    kc = kc_ref[...]
    vc = vc_ref[...]
    psum = jnp.zeros((TQ, N_CMP_PAD), F32)
    o_cmp = []
    for r in range(r_heads):
        s = _dot_nt(qs_ref[r * TQ:(r + 1) * TQ, :], kc) + cmpb_ref[r]
        e = jnp.exp(s - jnp.max(s, axis=-1, keepdims=True))
        p = e * (has_any / jnp.sum(e, axis=-1, keepdims=True))
        psum = psum + p
        o_cmp.append(_dot(p.astype(BF16), vc))

    p_hi = psum.astype(BF16)
    p_lo = (psum - p_hi.astype(F32)).astype(BF16)
    ov = ov_ref[...]
    imp = _dot_nt(ov, p_hi) + _dot_nt(ov, p_lo)
    jrow = lax.broadcasted_iota(jnp.int32, (n_sb, TQ), 0)
    tq = qi * TQ + lax.broadcasted_iota(jnp.int32, (n_sb, TQ), 1)
    cur = tq // SEL_BLOCK
    valid = jrow <= cur
    forced = jnp.where(jrow == 0, 1.0, jnp.where(jrow == cur, 1.0, jnp.where(jrow == cur - 1, 1.0, 0.0)))
    score = jnp.where(valid, imp + FORCE_BONUS * forced, -1.0)
    rank = jnp.zeros((n_sb, TQ), F32)
    for jp in range(n_sb):
        other = jnp.broadcast_to(score[jp:jp + 1, :], (n_sb, TQ))
        before = jnp.where(jrow > jp, 1.0, 0.0)
        rank = rank + jnp.where(other > score, 1.0, jnp.where(other == score, before, 0.0))
    chosen = jnp.where(valid, jnp.where(rank < float(min(N_SELECT, n_sb)), 0.0, NEG_INF), NEG_INF)

    srow = lax.broadcasted_iota(jnp.int32, (LANES - n_sb, TQ), 0) + n_sb
    side_t = jnp.concatenate([chosen, jnp.where(srow == SIDE_OFF, NEG_INF, 0.0)], axis=0)
    side = side_t.T
    for r in range(r_heads):
        far = jnp.full((TQ, LANES), rb_ref[N_BUCKETS - 1, r_heads * g + r], F32)
        far_hi = far.astype(BF16).astype(F32)
        side_r = jnp.where(lane == SIDE_FAR_HI, far_hi, jnp.where(lane == SIDE_FAR_LO, far - far_hi, side))
        side_ref[r * TQ:(r + 1) * TQ, :] = side_r.astype(BF16)

    def key_side(kj, side_kind):
        return jnp.concatenate([ks_ref[tile_rows(kj), :], kside_ref[side_kind, tile_rows(kj), :]], axis=1)

    def q_side():
        return jnp.concatenate([qs_ref[...], side_ref[...]], axis=1)

    sn = scores(q_side(), [key_side(qi, KSIDE_NEAR), key_side(kj1, KSIDE_NEAR)], (TILE_DIAG, near_type))
    m0 = row_max(sn)
    m_ref[...] = jnp.broadcast_to(m0, (rows, LANES))
    acc_ref[...] = weighted([jnp.exp(s - m0) for s in sn], vs_ref, (qi, kj1))

    def far_pair(idx):
        ka = qi - 2 - 2 * idx
        kb_raw = ka - 1
        kb = jnp.maximum(kb_raw, 0)
        sf = scores(q_side(), [key_side(ka, KSIDE_FAR),
                               key_side(kb, jnp.where(kb_raw >= 0, KSIDE_FAR, KSIDE_OFF))], (None, None))
        m_prev = m_ref[...]
        m_new = jnp.maximum(m_prev, row_max(sf))
        alpha = jnp.exp(m_prev - m_new)
        m_wide = jnp.tile(m_new, (1, TK // LANES))
        acc_ref[...] = (jnp.tile(alpha, (1, 2)) * acc_ref[...]
                        + weighted([jnp.exp(s - m_wide) for s in sf], vs_ref, (ka, kb)))
        m_ref[...] = m_new

    def placed(lo, hi):
        lo = jnp.where(par == 0, lo, pltpu.roll(lo, HEAD_DIM, 1))
        hi = jnp.where(par == 1, hi, pltpu.roll(hi, HEAD_DIM, 1))
        return jnp.where(lane < HEAD_DIM, lo, hi)

    for blk in range(2):
        part = [gates[r][0] * o_cmp[r] + gates[r][2] * o_win[r * TQ:(r + 1) * TQ, :] for r in (2 * blk, 2 * blk + 1)]
        out_ref[:, blk * LANES:(blk + 1) * LANES] = placed(*part)

    def finish():
        o_slc = normalised(acc_ref[...])
        for blk in range(2):
            part = [gates[r][1] * o_slc[r * TQ:(r + 1) * TQ, :] for r in (2 * blk, 2 * blk + 1)]
            out_ref[:, blk * LANES:(blk + 1) * LANES] += placed(*part)

    return far_pair, finish


def _nsa_attn(rel_bias, att, rnn_in, kvc, bgate, tiles, cmpb, batch, seq_len):
    n = att.shape[0]
    nq = seq_len // TQ
    n_sb = seq_len // SEL_BLOCK
    ng = N_KV_GROUPS
    r = HEADS_PER_GROUP
    ov = _overlap_constants(seq_len)
    kside = _key_side_constants(seq_len)
    qw = r * HEAD_DIM
    kcol = ATTN_WIDTH + 2 * KV_WIDTH
    ns = ATTN_SLOTS if batch % ATTN_SLOTS == 0 else 1
    gate_blk = 2 * RNN_WIDTH // GATE_PAD

    def slot_specs(s):
        seq = lambda bb: bb * ns + s
        kvb = lambda off: (lambda g, qi, bb: (seq(bb), (kcol + off * KV_WIDTH) // LANES + g // 2))
        return [
            pl.BlockSpec((TQ, qw), lambda g, qi, bb: (seq(bb) * nq + qi, g)),
            pl.BlockSpec((None, None, N_CMP_PAD, LANES), lambda g, qi, bb: (0, seq(bb) * ng + g, 0, 0)),
            pl.BlockSpec((None, None, N_CMP_PAD, LANES), lambda g, qi, bb: (1, seq(bb) * ng + g, 0, 0)),
            pl.BlockSpec((seq_len, LANES), kvb(0)),
            pl.BlockSpec((seq_len, LANES), kvb(1)),
            pl.BlockSpec((seq_len, LANES), kvb(2)),
            pl.BlockSpec((seq_len, LANES), kvb(3)),
            pl.BlockSpec((TQ, GATE_PAD), lambda g, qi, bb: (seq(bb) * nq + qi, gate_blk)),
        ]

    shared_specs = [
        pl.BlockSpec((1, GATE_PAD), lambda g, qi, bb: (0, 0)),
        pl.BlockSpec((None, N_TILE_TYPES, r, TQ, TK), lambda g, qi, bb: (g, 0, 0, 0, 0)),
        pl.BlockSpec((None, r, TQ, N_CMP_PAD), lambda g, qi, bb: (g, 0, qi, 0)),
        pl.BlockSpec((n_sb, N_CMP_PAD), lambda g, qi, bb: (0, 0)),
        pl.BlockSpec((3, seq_len, LANES), lambda g, qi, bb: (0, 0, 0)),
    ]
    slot_scratch = [
        pltpu.VMEM((r * TQ, LANES), BF16),
        pltpu.VMEM((r * TQ, LANES), BF16),
        pltpu.VMEM((r * TQ, LANES), F32),
        pltpu.VMEM((r * TQ, 2 * LANES), F32),
    ]
    outs = pl.pallas_call(
        functools.partial(_nsa_kernel, n_sb, ns),
        grid=(ng, nq, batch // ns),
        in_specs=([pl.BlockSpec(memory_space=pltpu.SMEM)]
                  + [spec for s in range(ns) for spec in slot_specs(s)] + shared_specs),
        out_specs=pl.BlockSpec((None, ns, TQ, qw), lambda g, qi, bb: (bb, 0, qi, g)),
        out_shape=jax.ShapeDtypeStruct((batch // ns, ns, seq_len, ATTN_WIDTH), F32),
        scratch_shapes=slot_scratch * ns,
        compiler_params=pltpu.CompilerParams(dimension_semantics=("arbitrary", "arbitrary", "arbitrary"),
                                             vmem_limit_bytes=VMEM_LIMIT),
        name="nsa_attn",
    )(rel_bias, *([att, kvc, kvc, att, att, att, att, rnn_in] * ns), bgate, tiles, cmpb,
      jnp.asarray(ov, BF16), jnp.asarray(kside, BF16))
    return outs.reshape(n, ATTN_WIDTH)


def kernel(x, mix_norm_g, w_in, b_gate, cmp_pe_k, cmp_pe_v, cmp_k_w1, cmp_k_w2, cmp_v_w1, cmp_v_w2, rel_bias, rnn_conv_w, rnn_conv_b, rg_a_w, rg_a_b, rg_x_w, rg_x_b, rg_lambda, attn_out_g, rnn_out_g, w_out, ffn_norm_g, w_ffn_gate, w_ffn_up, ffn_conv_w, ffn_conv_b, w_ffn_down, final_norm_g):
    batch, seq_len, d_model = x.shape
    assert d_model == D_MODEL and w_in.shape[0] == 1, "single-layer model of the stated width"
    assert seq_len // CMP_STRIDE == N_CMP_PAD and seq_len % TM_FFN == 0
    n = batch * seq_len
    ng = N_KV_GROUPS
    x2 = x.reshape(n, D_MODEL)

    w0 = w_in[0].astype(BF16)
    gate0 = ATT_COLS
    rx0 = gate0 + N_BRANCH * N_Q_HEADS
    w_rnn = jnp.concatenate(
        [w0[:, rx0:rx0 + 2 * RNN_WIDTH], w0[:, gate0:rx0],
         jnp.zeros((D_MODEL, GATE_PAD - N_BRANCH * N_Q_HEADS), BF16)], axis=1)
    att, rnn_in, chunks = _in_proj(x2, mix_norm_g[0][None], w0, w_rnn)

    pe = jnp.stack([_pair_pe(cmp_pe_k[0]), _pair_pe(cmp_pe_v[0])]).astype(BF16)
    w1p = jnp.stack([_pair_weights(cmp_k_w1[0]), _pair_weights(cmp_v_w1[0])]).astype(BF16)
    w2 = jnp.stack([jnp.tile(cmp_k_w2[0], (1, 2)), jnp.tile(cmp_v_w2[0], (1, 2))]).astype(BF16)
    kvc = _cmp_tokens(chunks, pe, w1p, w2, batch).reshape(2, batch * ng, N_CMP_PAD, LANES)

    tiles, cmpb = _bias_tiles(rel_bias, seq_len)
    bgate = jnp.pad(b_gate[0], (0, GATE_PAD - b_gate.shape[1]))[None]
    o_attn = _nsa_attn(rel_bias, att, rnn_in, kvc, bgate, tiles, cmpb, batch, seq_len)

    nr = _rglru(rnn_in, rnn_conv_w[0], rnn_conv_b[0][None],
                _block_diag(rg_a_w[0]).astype(BF16), rg_a_b[0][None],
                _block_diag(rg_x_w[0]).astype(BF16), rg_x_b[0][None],
                rg_lambda[0][None], rnn_out_g[0][None], batch, seq_len)

    h, y = _out_proj(o_attn, nr, x2, attn_out_g[0][None], ffn_norm_g[0][None], w_out[0].astype(BF16))
    out = _conv_ffn(y, h, w_ffn_gate[0].astype(BF16), w_ffn_up[0].astype(BF16), w_ffn_down[0].astype(BF16),
                    ffn_conv_w[0], ffn_conv_b[0][None], final_norm_g[None], seq_len)
    return out.reshape(batch, seq_len, D_MODEL)
```

```python
import functools
import math

import jax
import jax.numpy as jnp
import numpy as np
from jax import lax
from jax.experimental import pallas as pl
from jax.experimental.pallas import tpu as pltpu

F32 = jnp.float32
BF16 = jnp.bfloat16

D_MODEL = 2048
N_Q_HEADS = 16
N_KV_GROUPS = 4
HEADS_PER_GROUP = 4
HEAD_DIM = 64
ATTN_WIDTH = 1024
KV_WIDTH = 256
SCALE = HEAD_DIM ** -0.5
CMP_BLOCK = 32
CMP_STRIDE = 16
CMP_HIDDEN = 256
SEL_BLOCK = 64
N_SELECT = 16
WINDOW = 512
N_BRANCH = 3
RNN_WIDTH = 1024
RNN_BLOCKS = 16
RNN_CONV_WIDTH = 4
RG_LRU_C = 8.0
N_BUCKETS = 32
MAX_DISTANCE = 128
D_FF = 5632
NORM_EPS = 1e-6
NEG_INF = -1e30
FORCE_BONUS = 1e3
GELU_C0 = math.sqrt(2.0 / math.pi)
GELU_C1 = GELU_C0 * 0.044715

LANES = 128
SUBLANES = 8
VMEM_LIMIT = 56 * 1024 * 1024

TM_PROJ = 512
COL_CHUNK = 512
TM_FFN = 1024
TF_FFN = 512
FFN_ROW_BLOCKS = 2
H_PIECE = 256
TS_RNN = 512
CB_RNN = 256
RNN_ROWS = 256
PROJ_CHUNK = 256
TQ = 256
TK = 256
ATTN_SLOTS = 2
ROW_SPLIT = 512
N_CMP_PAD = 128
GATE_PAD = 128

ATT_COLS = ATTN_WIDTH + 6 * KV_WIDTH
RNN_COLS = 2 * RNN_WIDTH + GATE_PAD
CMP_COL0 = ATTN_WIDTH
N_CHUNK_SLABS = 2 * KV_WIDTH // LANES
CHUNK2_W = CMP_STRIDE * LANES


def _rms(x, g):
    return x * lax.rsqrt(jnp.mean(x * x, axis=-1, keepdims=True) + NORM_EPS) * g


def _gelu(x):
    return jax.nn.gelu(x, approximate=True)


def _dot(a, b):
    return jnp.dot(a, b, preferred_element_type=F32)


def _dot_nt(a, b):
    return lax.dot_general(a, b, (((1,), (1,)), ((), ())), preferred_element_type=F32)


def _dot_tn(a, b):
    return lax.dot_general(a, b, (((0,), (0,)), ((), ())), preferred_element_type=F32)


def _out_proj_kernel(oa_ref, nr_ref, x_ref, ga_ref, gf_ref, w_ref, h_ref, y_ref):
    na = _rms(oa_ref[...], ga_ref[...]).astype(BF16)
    nr = nr_ref[...]
    for c0 in range(0, D_MODEL, COL_CHUNK):
        sl = slice(c0, c0 + COL_CHUNK)
        h_ref[:, sl] = (x_ref[:, sl] + _dot(na, w_ref[0:ATTN_WIDTH, sl])
                        + _dot(nr, w_ref[ATTN_WIDTH:D_MODEL, sl]))
    y_ref[...] = _rms(h_ref[...], gf_ref[...]).astype(BF16)


def _out_proj(o_attn, nr, x2, ga, gf, w):
    n = x2.shape[0]
    return pl.pallas_call(
        _out_proj_kernel,
        grid=(n // TM_PROJ,),
        in_specs=[
            pl.BlockSpec((TM_PROJ, ATTN_WIDTH), lambda i: (i, 0)),
            pl.BlockSpec((TM_PROJ, RNN_WIDTH), lambda i: (i, 0)),
            pl.BlockSpec((TM_PROJ, D_MODEL), lambda i: (i, 0)),
            pl.BlockSpec((1, ATTN_WIDTH), lambda i: (0, 0)),
            pl.BlockSpec((1, D_MODEL), lambda i: (0, 0)),
            pl.BlockSpec((D_MODEL, D_MODEL), lambda i: (0, 0), pipeline_mode=pl.Buffered(1)),
        ],
        out_specs=[
            pl.BlockSpec((TM_PROJ, D_MODEL), lambda i: (i, 0)),
            pl.BlockSpec((TM_PROJ, D_MODEL), lambda i: (i, 0)),
        ],
        out_shape=[
            jax.ShapeDtypeStruct((n, D_MODEL), F32),
            jax.ShapeDtypeStruct((n, D_MODEL), BF16),
        ],
        compiler_params=pltpu.CompilerParams(dimension_semantics=("parallel",), vmem_limit_bytes=VMEM_LIMIT),
        name="out_proj",
    )(o_attn, nr, x2, ga, gf, w)


def _ffn_kernel(tiles_per_seq, y_ref, h_ref, wg_ref, wu_ref, wd_ref, cw_ref, cb_ref, gf_ref, out_ref, carry_ref):
    i = pl.program_id(0)
    j = pl.program_id(1)
    nj = pl.num_programs(1)

    @pl.when((i % tiles_per_seq) == 0)
    def _():
        carry_ref[j] = jnp.zeros((SUBLANES, TF_FFN), F32)

    @pl.when(j == 0)
    def _():
        out_ref[...] = jnp.zeros(out_ref.shape, F32)

    @pl.when(j < D_MODEL // H_PIECE)
    def _():
        cols = pl.ds(pl.multiple_of(j * H_PIECE, H_PIECE), H_PIECE)
        out_ref[:, cols] += h_ref[...]

    row = lax.broadcasted_iota(jnp.int32, (SUBLANES, TF_FFN), 0)
    cw = cw_ref[...]
    cb = cb_ref[...]
    prev = carry_ref[j]
    rows_blk = TM_FFN // FFN_ROW_BLOCKS
    for blk in range(FFN_ROW_BLOCKS):
        sl = slice(blk * rows_blk, (blk + 1) * rows_blk)
        y = y_ref[sl, :]
        gt = _dot(y, wg_ref[...])
        p1 = prev[SUBLANES - 1:SUBLANES, :]
        p2 = prev[SUBLANES - 2:SUBLANES - 1, :]
        r1 = pltpu.roll(gt, 1, 0)
        r2 = pltpu.roll(gt, 2, 0)
        h1 = jnp.where(row == 0, p1, r1[0:SUBLANES])
        h2 = jnp.where(row == 0, p2, jnp.where(row == 1, p1, r2[0:SUBLANES]))
        s1 = jnp.concatenate([h1, r1[SUBLANES:]], axis=0)
        s2 = jnp.concatenate([h2, r2[SUBLANES:]], axis=0)
        conv = cw[2:3, :] * gt + cw[1:2, :] * s1 + cw[0:1, :] * s2 + cb
        inner = conv * (GELU_C0 + GELU_C1 * (conv * conv))
        hx = 0.5 * conv
        u = hx + hx * jnp.tanh(inner)
        z = (u * _dot(y, wu_ref[...])).astype(BF16)
        out_ref[sl, :] += _dot(z, wd_ref[...])
        prev = gt[rows_blk - SUBLANES:rows_blk, :]
    carry_ref[j] = prev

    @pl.when(j == nj - 1)
    def _():
        out_ref[...] = _rms(out_ref[...], gf_ref[...])


def _conv_ffn(y, h, wg, wu, wd, cw, cb, gf, seq_len):
    n = y.shape[0]
    nj = D_FF // TF_FFN
    return pl.pallas_call(
        functools.partial(_ffn_kernel, seq_len // TM_FFN),
        grid=(n // TM_FFN, nj),
        in_specs=[
            pl.BlockSpec((TM_FFN, D_MODEL), lambda i, j: (i, 0)),
            pl.BlockSpec((TM_FFN, H_PIECE), lambda i, j: (i, jnp.minimum(j, D_MODEL // H_PIECE - 1))),
            pl.BlockSpec((D_MODEL, TF_FFN), lambda i, j: (0, j)),
            pl.BlockSpec((D_MODEL, TF_FFN), lambda i, j: (0, j)),
            pl.BlockSpec((TF_FFN, D_MODEL), lambda i, j: (j, 0)),
            pl.BlockSpec((3, TF_FFN), lambda i, j: (0, j)),
            pl.BlockSpec((1, TF_FFN), lambda i, j: (0, j)),
            pl.BlockSpec((1, D_MODEL), lambda i, j: (0, 0)),
        ],
        out_specs=pl.BlockSpec((TM_FFN, D_MODEL), lambda i, j: (i, 0)),
        out_shape=jax.ShapeDtypeStruct((n, D_MODEL), F32),
        scratch_shapes=[pltpu.VMEM((nj, SUBLANES, TF_FFN), F32)],
        compiler_params=pltpu.CompilerParams(dimension_semantics=("arbitrary", "arbitrary"),
                                             vmem_limit_bytes=VMEM_LIMIT),
        name="conv_ffn",
    )(y, h, wg, wu, wd, cw, cb, gf)


def _rglru_pieces(rx_ref, ry_ref, cw_ref, cb_ref, wa_ref, ba_ref, wx_ref, bx_ref, lam_ref, g_ref,
                  out_ref, tail_ref, hc_ref, a_ref, u_ref, xbuf_ref):
    rmod = lax.broadcasted_iota(jnp.int32, (RNN_ROWS // SUBLANES, SUBLANES, CB_RNN), 1)
    lam = lam_ref[...]
    nlam = -lam
    softplus = jnp.maximum(nlam, 0.0) + jnp.log1p(jnp.exp(-jnp.abs(nlam)))
    log_a_unit = -RG_LRU_C * softplus

    def stage():
        xbuf_ref[0:SUBLANES, :] = tail_ref[...]
        xbuf_ref[SUBLANES:SUBLANES + TS_RNN, :] = rx_ref[...]
        tail_ref[...] = rx_ref[TS_RNN - SUBLANES:TS_RNN, :]

    def channel_block(cb, lo):
        sl = slice(cb * CB_RNN, (cb + 1) * CB_RNN)
        hi = lo + RNN_ROWS
        cw = cw_ref[:, sl]
        xr = cw[3:4, :] * rx_ref[lo:hi, sl] + cb_ref[:, sl]
        for d in (1, 2, 3):
            xr = xr + cw[3 - d:4 - d, :] * xbuf_ref[SUBLANES - d + lo:SUBLANES - d + hi, sl]
        xb = xr.astype(BF16)
        r = jax.nn.sigmoid(_dot(xb, wa_ref[cb]) + ba_ref[:, sl])
        gi = jax.nn.sigmoid(_dot(xb, wx_ref[cb]) + bx_ref[:, sl])
        log_a = log_a_unit[:, sl] * r
        a = jnp.exp(log_a)
        th = jnp.tanh(log_a)
        w = -2.0 * th / (1.0 - th)
        u = jnp.where(w > 0.0, w * lax.rsqrt(w), 0.0) * (gi * xr)
        a = a.reshape(RNN_ROWS // SUBLANES, SUBLANES, CB_RNN)
        u = u.reshape(RNN_ROWS // SUBLANES, SUBLANES, CB_RNN)
        for d in (1, 2, 4):
            keep = rmod >= d
            a_s = pltpu.roll(a, d, 1)
            u_s = pltpu.roll(u, d, 1)
            u = jnp.where(keep, a * u_s + u, u)
            a = jnp.where(keep, a * a_s, a)
        a_ref[lo:hi, sl] = a.reshape(RNN_ROWS, CB_RNN)
        u_ref[lo:hi, sl] = u.reshape(RNN_ROWS, CB_RNN)

    def carry(lo, hi):
        hprev = hc_ref[...]
        for r0 in range(lo, hi, SUBLANES):
            h8 = u_ref[r0:r0 + SUBLANES, :] + a_ref[r0:r0 + SUBLANES, :] * hprev
            u_ref[r0:r0 + SUBLANES, :] = h8
            hprev = jnp.broadcast_to(h8[SUBLANES - 1:SUBLANES, :], (SUBLANES, RNN_WIDTH))
        hc_ref[...] = hprev

    def finish(lo, hi):
        out_ref[lo:hi, :] = _rms(u_ref[lo:hi, :] * _gelu(ry_ref[lo:hi, :]), g_ref[...]).astype(BF16)

    quarter, half = TS_RNN // 4, TS_RNN // 2
    rx_readers = [stage] + [functools.partial(channel_block, cb, lo)
                            for cb in range(RNN_WIDTH // CB_RNN) for lo in range(0, TS_RNN, RNN_ROWS)]
    others = ([functools.partial(carry, k * quarter, (k + 1) * quarter) for k in range(4)]
              + [functools.partial(finish, k * half, (k + 1) * half) for k in range(2)])
    return rx_readers, others


def _in_proj_kernel(tiles_per_seq, x_ref, g_ref, w_ref, wr_ref,
                    cw_ref, cb_ref, wa_ref, ba_ref, wx_ref, bx_ref, lam_ref, gr_ref,
                    att_ref, gate_ref, chunk_ref, nr_ref,
                    stage_ref, buf_ref, tail_ref, hc_ref, a_ref, u_ref, xbuf_ref):
    i = pl.program_id(0)

    @pl.when(i == 0)
    def _():
        buf_ref[...] = jnp.zeros(buf_ref.shape, F32)

    @pl.when(jnp.maximum(i - 1, 0) % tiles_per_seq == 0)
    def _():
        tail_ref[...] = jnp.zeros(tail_ref.shape, F32)
        hc_ref[...] = jnp.zeros(hc_ref.shape, F32)

    rx_readers, other_pieces = _rglru_pieces(
        buf_ref.at[:, 0:RNN_WIDTH], buf_ref.at[:, RNN_WIDTH:2 * RNN_WIDTH],
        cw_ref, cb_ref, wa_ref, ba_ref, wx_ref, bx_ref, lam_ref, gr_ref,
        nr_ref, tail_ref, hc_ref, a_ref, u_ref, xbuf_ref)
    xn = _rms(x_ref[...], g_ref[...]).astype(BF16)

    def att_chunk(c0):
        res = _dot(xn, w_ref[:, c0:c0 + PROJ_CHUNK])
        att_ref[:, c0:c0 + PROJ_CHUNK] = res.astype(BF16)
        if CMP_COL0 <= c0 < CMP_COL0 + N_CHUNK_SLABS * LANES:
            q0 = (c0 - CMP_COL0) // LANES
            slabs = range(q0, q0 + PROJ_CHUNK // LANES)
            for q in slabs:
                stage_ref[q] = res[:, (q - q0) * LANES:(q - q0 + 1) * LANES]
            for l in range(CMP_STRIDE):
                for q in slabs:
                    tok = stage_ref[q, pl.ds(l, TM_PROJ // CMP_STRIDE, stride=CMP_STRIDE), :]
                    chunk_ref[q, :, l * LANES:(l + 1) * LANES] = tok.astype(BF16)

    def rnn_chunk(c0):
        buf_ref[:, c0:c0 + PROJ_CHUNK] = _dot(xn, wr_ref[:, c0:c0 + PROJ_CHUNK])

    def gate_chunk():
        gate_ref[...] = _dot(xn, wr_ref[:, 2 * RNN_WIDTH:RNN_COLS])

    def interleave(matmuls, others):
        done = 0
        for k, piece in enumerate(matmuls):
            piece()
            upto = (k + 1) * len(others) // len(matmuls)
            for other in others[done:upto]:
                other()
            done = upto

    interleave([functools.partial(att_chunk, c0) for c0 in range(0, ATT_COLS, PROJ_CHUNK)] + [gate_chunk], rx_readers)
    interleave([functools.partial(rnn_chunk, c0) for c0 in range(0, RNN_WIDTH, PROJ_CHUNK)], other_pieces)
    for c0 in range(RNN_WIDTH, 2 * RNN_WIDTH, PROJ_CHUNK):
        rnn_chunk(c0)


def _in_proj(x2, g, w, w_rnn, cw, cb, wa, ba, wx, bx, lam, gr, seq_len):
    n = x2.shape[0]
    nt = n // TM_PROJ
    assert CMP_COL0 % PROJ_CHUNK == 0 and (N_CHUNK_SLABS * LANES) % PROJ_CHUNK == 0 and TS_RNN == TM_PROJ
    nb = RNN_WIDTH // CB_RNN
    cur = lambda i: jnp.minimum(i, nt - 1)
    vec = pl.BlockSpec((1, RNN_WIDTH), lambda i: (0, 0))
    wspec = pl.BlockSpec((nb, CB_RNN, CB_RNN), lambda i: (0, 0, 0))
    return pl.pallas_call(
        functools.partial(_in_proj_kernel, seq_len // TM_PROJ),
        grid=(nt + 1,),
        in_specs=[
            pl.BlockSpec((TM_PROJ, D_MODEL), lambda i: (cur(i), 0)),
            pl.BlockSpec((1, D_MODEL), lambda i: (0, 0)),
            pl.BlockSpec((D_MODEL, ATT_COLS), lambda i: (0, 0), pipeline_mode=pl.Buffered(1)),
            pl.BlockSpec((D_MODEL, RNN_COLS), lambda i: (0, 0), pipeline_mode=pl.Buffered(1)),
            pl.BlockSpec((RNN_CONV_WIDTH, RNN_WIDTH), lambda i: (0, 0)),
            vec, wspec, vec, wspec, vec, vec, vec,
        ],
        out_specs=[
            pl.BlockSpec((TM_PROJ, ATT_COLS), lambda i: (cur(i), 0)),
            pl.BlockSpec((TM_PROJ, GATE_PAD), lambda i: (cur(i), 0)),
            pl.BlockSpec((N_CHUNK_SLABS, TM_PROJ // CMP_STRIDE, CHUNK2_W), lambda i: (0, cur(i), 0)),
            pl.BlockSpec((TM_PROJ, RNN_WIDTH), lambda i: (jnp.maximum(i - 1, 0), 0)),
        ],
        out_shape=[
            jax.ShapeDtypeStruct((n, ATT_COLS), BF16),
            jax.ShapeDtypeStruct((n, GATE_PAD), F32),
            jax.ShapeDtypeStruct((N_CHUNK_SLABS, n // CMP_STRIDE, CHUNK2_W), BF16),
            jax.ShapeDtypeStruct((n, RNN_WIDTH), BF16),
        ],
        scratch_shapes=[
            pltpu.VMEM((N_CHUNK_SLABS, TM_PROJ, LANES), F32),
            pltpu.VMEM((TM_PROJ, 2 * RNN_WIDTH), F32),
            pltpu.VMEM((SUBLANES, RNN_WIDTH), F32),
            pltpu.VMEM((SUBLANES, RNN_WIDTH), F32),
            pltpu.VMEM((TS_RNN, RNN_WIDTH), F32),
            pltpu.VMEM((TS_RNN, RNN_WIDTH), F32),
            pltpu.VMEM((SUBLANES + TS_RNN, RNN_WIDTH), F32),
        ],
        compiler_params=pltpu.CompilerParams(dimension_semantics=("arbitrary",), vmem_limit_bytes=VMEM_LIMIT),
        name="in_proj",
    )(x2, g, w, w_rnn, cw, cb, wa, ba, wx, bx, lam, gr)


def _block_diag(w):
    per = CB_RNN // (RNN_WIDTH // RNN_BLOCKS)
    bd = w.shape[-1]
    w4 = w.reshape(RNN_BLOCKS // per, per, bd, bd)
    eye = jnp.eye(per, dtype=w.dtype)
    return jnp.einsum('cpij,pq->cpiqj', w4, eye).reshape(RNN_BLOCKS // per, per * bd, per * bd)


CMP_BATCHES = 8


def _cmp_tokens_kernel(x_ref, pe_ref, w1_ref, w2_ref, out_ref):
    x = x_ref[...]
    rows = x.shape[0]
    ya = _dot(x, w1_ref[0])
    yb = _dot(x, w1_ref[1])
    pterm = (_dot(jnp.broadcast_to(pe_ref[0], (SUBLANES, CHUNK2_W)), w1_ref[0])
             + _dot(jnp.broadcast_to(pe_ref[1], (SUBLANES, CHUNK2_W)), w1_ref[1]))[0:1, :]
    act = _gelu(ya + pltpu.roll(yb, rows - 1, 0) + pterm).astype(BF16)
    for e in range(2):
        tok = _dot(act[:, e * CMP_HIDDEN:(e + 1) * CMP_HIDDEN], w2_ref[...]).astype(BF16)
        out_ref[:, e] = tok.reshape(rows // N_CMP_PAD, N_CMP_PAD, LANES)


def _cmp_tokens(chunks, pe, w1p, w2, batch):
    nb = min(CMP_BATCHES, batch)
    tr = nb * N_CMP_PAD
    return pl.pallas_call(
        _cmp_tokens_kernel,
        grid=(N_CHUNK_SLABS, batch // nb),
        in_specs=[
            pl.BlockSpec((None, tr, CHUNK2_W), lambda s, i: (s, i, 0)),
            pl.BlockSpec((None, 2, 1, CHUNK2_W), lambda s, i: (s // 2, 0, 0, 0)),
            pl.BlockSpec((None, 2, CHUNK2_W, 2 * CMP_HIDDEN), lambda s, i: (s // 2, 0, 0, 0)),
            pl.BlockSpec((None, CMP_HIDDEN, LANES), lambda s, i: (s // 2, 0, 0)),
        ],
        out_specs=pl.BlockSpec((None, nb, None, 2, N_CMP_PAD, LANES), lambda s, i: (s // 2, i, s % 2, 0, 0, 0)),
        out_shape=jax.ShapeDtypeStruct((2, batch, 2, 2, N_CMP_PAD, LANES), BF16),
        compiler_params=pltpu.CompilerParams(dimension_semantics=("parallel", "parallel")),
        name="cmp_tokens",
    )(chunks, pe, w1p, w2)


def _pair_weights(w1):
    half = CMP_STRIDE * HEAD_DIM
    w = w1.reshape(2, CMP_STRIDE, HEAD_DIM, CMP_HIDDEN)
    eye = jnp.eye(2, dtype=w1.dtype)
    return jnp.einsum('hldj,ef->hledfj', w, eye).reshape(2, 2 * half, 2 * CMP_HIDDEN)


def _pair_pe(pe):
    p = pe.reshape(2, CMP_STRIDE, 1, HEAD_DIM)
    return jnp.broadcast_to(p, (2, CMP_STRIDE, 2, HEAD_DIM)).reshape(2, 1, CHUNK2_W)


N_TILE_TYPES = 4
TILE_DIAG, TILE_NEAR, TILE_WIN_LAST, TILE_NONE = 0, 1, 2, 3


def _t5_bucket_np(dist):
    n = np.maximum(dist, 0)
    max_exact = N_BUCKETS // 2
    nf = np.maximum(n, 1).astype(np.float64)
    large = max_exact + (np.log(nf / max_exact) / math.log(MAX_DISTANCE / max_exact)
                         * (N_BUCKETS - max_exact)).astype(np.int32)
    large = np.minimum(large, N_BUCKETS - 1)
    return np.where(n < max_exact, n, large).astype(np.int32)


def _bucket_constants(seq_len):
    i = np.arange(TQ)[:, None]
    j = np.arange(TK)[None, :]
    d0 = i - j
    d1 = TK + i - j
    d2 = 2 * TK + i - j
    valid = [d0 >= 0, np.ones_like(d1, bool), d2 < WINDOW, np.zeros_like(d0, bool)]
    tiles = np.stack([np.where(v, _t5_bucket_np(d), -1) for d, v in zip([d0, d1, d2, d0], valid)])
    t = np.arange(seq_len)[:, None]
    c = np.arange(N_CMP_PAD)[None, :]
    dc = t - (c * CMP_STRIDE + CMP_BLOCK - 1)
    cmp = np.where(dc >= 0, _t5_bucket_np(dc), -1)
    return tiles.astype(np.int32), cmp.astype(np.int32)


def _bias_kernel(tile_buckets, rb_ref, tb_ref, cbk_ref, tiles_ref, cmpb_ref):
    h = pl.program_id(0)

    def lookup(bkt, buckets):
        acc = jnp.full(bkt.shape, NEG_INF, F32)
        for b in buckets:
            acc = jnp.where(bkt == b, rb_ref[b, h], acc)
        return acc

    for t in range(N_TILE_TYPES):
        tiles_ref[t] = lookup(tb_ref[t], tile_buckets[t])
    cmpb_ref[...] = lookup(cbk_ref[...], range(N_BUCKETS))


def _bias_tiles(rel_bias, seq_len):
    tb, cbk = _bucket_constants(seq_len)
    tile_buckets = tuple(tuple(int(b) for b in np.unique(t[t >= 0])) for t in tb)
    r = HEADS_PER_GROUP
    return pl.pallas_call(
        functools.partial(_bias_kernel, tile_buckets),
        grid=(N_Q_HEADS,),
        in_specs=[
            pl.BlockSpec(memory_space=pltpu.SMEM),
            pl.BlockSpec((N_TILE_TYPES, TQ, TK), lambda h: (0, 0, 0)),
            pl.BlockSpec((seq_len, N_CMP_PAD), lambda h: (0, 0)),
        ],
        out_specs=[
            pl.BlockSpec((None, N_TILE_TYPES, None, TQ, TK), lambda h: (h // r, 0, h % r, 0, 0)),
            pl.BlockSpec((None, None, seq_len, N_CMP_PAD), lambda h: (h // r, h % r, 0, 0)),
        ],
        out_shape=[
            jax.ShapeDtypeStruct((N_KV_GROUPS, N_TILE_TYPES, r, TQ, TK), F32),
            jax.ShapeDtypeStruct((N_KV_GROUPS, r, seq_len, N_CMP_PAD), F32),
        ],
        compiler_params=pltpu.CompilerParams(dimension_semantics=("parallel",)),
        name="bias_tiles",
    )(rel_bias, jnp.asarray(tb), jnp.asarray(cbk))


SIDE_FAR_HI, SIDE_FAR_LO, SIDE_OFF = 32, 33, 34
KSIDE_NEAR, KSIDE_FAR, KSIDE_OFF = 0, 1, 2


def _key_side_constants(seq_len):
    s = np.arange(seq_len)
    side = np.zeros((3, seq_len, LANES), np.float32)
    side[:, s, s // SEL_BLOCK] = 1.0
    side[KSIDE_FAR, :, SIDE_FAR_HI] = 1.0
    side[KSIDE_FAR, :, SIDE_FAR_LO] = 1.0
    side[KSIDE_OFF, :, SIDE_OFF] = 1.0
    return side


def _overlap_constants(seq_len):
    n_sb = seq_len // SEL_BLOCK
    c = np.arange(N_CMP_PAD)[None, :]
    j = np.arange(n_sb)[:, None]
    lo = np.maximum(c * CMP_STRIDE, j * SEL_BLOCK)
    hi = np.minimum(c * CMP_STRIDE + CMP_BLOCK, (j + 1) * SEL_BLOCK)
    ov = np.maximum(hi - lo, 0) / CMP_BLOCK
    ov[:, N_CMP_PAD - 1] = 0.0
    return ov.astype(np.float32)


def _nsa_kernel(n_sb, n_slots, rb_ref, *refs):
    per_slot_in = 8
    slot_in = [refs[s * per_slot_in:(s + 1) * per_slot_in] for s in range(n_slots)]
    shared = refs[n_slots * per_slot_in:n_slots * per_slot_in + 5]
    out_ref = refs[n_slots * per_slot_in + 5]
    scratch = refs[n_slots * per_slot_in + 6:]
    g = pl.program_id(0)
    qi = pl.program_id(1)
    slots = [_nsa_slot(n_sb, g, qi, rb_ref, *slot_in[s], *shared, out_ref.at[s], *scratch[4 * s:4 * s + 4])
             for s in range(n_slots)]

    def far_pairs(idx, carry):
        for far_pair, _ in slots:
            far_pair(idx)
        return carry

    lax.fori_loop(0, jnp.maximum(qi, 1) // 2, far_pairs, 0)
    for _, finish in slots:
        finish()


def _nsa_slot(n_sb, g, qi, rb_ref, q_ref, kc_ref, vc_ref, ks_ref, vs_ref, kw_ref, vw_ref, gate_ref, bg_ref,
              tiles_ref, cmpb_ref, ov_ref, kside_ref, out_ref,
              qs_ref, side_ref, m_ref, acc_ref):
    par = g % 2
    r_heads = HEADS_PER_GROUP
    rows = r_heads * TQ
    lane = lax.broadcasted_iota(jnp.int32, (TQ, LANES), 1)
    in_half = (lane // HEAD_DIM) == par

    for blk in range(2):
        qb = q_ref[:, blk * LANES:(blk + 1) * LANES].astype(F32)
        qr = pltpu.roll(qb, HEAD_DIM, 1)
        for hh in range(2):
            r = blk * 2 + hh
            src = jnp.where(par == hh, qb, qr)
            qs_ref[r * TQ:(r + 1) * TQ, :] = (jnp.where(in_half, src, 0.0) * SCALE).astype(BF16)

    gs = jax.nn.sigmoid(gate_ref[...] + bg_ref[...])
    gates = [[jnp.sum(jnp.where(lane == N_BRANCH * (r_heads * g + r) + br, gs, 0.0), axis=-1, keepdims=True)
              for br in range(N_BRANCH)] for r in range(r_heads)]

    def tile_rows(kj):
        return pl.ds(pl.multiple_of(kj * TK, TK), TK)

    ones = jnp.ones((TK, LANES), BF16)

    def scores(q, key_tiles, tile_types):
        keys = jnp.concatenate(key_tiles, axis=0)
        s = jnp.concatenate([_dot_nt(q[c:c + ROW_SPLIT], keys) for c in range(0, rows, ROW_SPLIT)], axis=0)
        out = []
        for t, kind in enumerate(tile_types):
            st = s[:, t * TK:(t + 1) * TK]
            if kind is not None:
                st = (st.reshape(r_heads, TQ, TK) + tiles_ref[kind]).reshape(rows, TK)
            out.append(st)
        return out

    def weighted(ps, v_ref, kjs):
        p = jnp.concatenate([x.astype(BF16) for x in ps], axis=1)
        v = jnp.concatenate([jnp.concatenate([v_ref[tile_rows(kj), :], ones], axis=1) for kj in kjs], axis=0)
        return jnp.concatenate([_dot(p[c:c + ROW_SPLIT], v) for c in range(0, rows, ROW_SPLIT)], axis=0)

    def row_max(ss):
        return jnp.max(functools.reduce(jnp.maximum, ss), axis=-1, keepdims=True)

    def normalised(acc):
        return acc[:, 0:LANES] * (1.0 / acc[:, LANES:2 * LANES])

    kj1 = jnp.maximum(qi - 1, 0)
    near_type = jnp.where(qi >= 1, TILE_NEAR, TILE_NONE)

    kj2 = jnp.maximum(qi - 2, 0)
    last_type = jnp.where(qi >= 2, TILE_WIN_LAST, TILE_NONE)
    sw = scores(qs_ref[...], [kw_ref[tile_rows(kj), :] for kj in (qi, kj1, kj2)],
                (TILE_DIAG, near_type, last_type))
    mw = row_max(sw)
    o_win = normalised(weighted([jnp.exp(s - mw) for s in sw], vw_ref, (qi, kj1, kj2)))

    trow = qi * TQ + lax.broadcasted_iota(jnp.int32, (TQ, 1), 0)
    has_any = jnp.where(trow >= CMP_BLOCK - 1, 1.0, 0.0)
    kc = kc_ref[...]
    vc = vc_ref[...]
    psum = jnp.zeros((TQ, N_CMP_PAD), F32)
    o_cmp = []
    for r in range(r_heads):
        s = _dot_nt(qs_ref[r * TQ:(r + 1) * TQ, :], kc) + cmpb_ref[r]
        e = jnp.exp(s - jnp.max(s, axis=-1, keepdims=True))
        p = e * (has_any / jnp.sum(e, axis=-1, keepdims=True))
        psum = psum + p
        o_cmp.append(_dot(p.astype(BF16), vc))

    p_hi = psum.astype(BF16)
    p_lo = (psum - p_hi.astype(F32)).astype(BF16)
    ov = ov_ref[...]
    imp = _dot_nt(ov, p_hi) + _dot_nt(ov, p_lo)
    jrow = lax.broadcasted_iota(jnp.int32, (n_sb, TQ), 0)
    tq = qi * TQ + lax.broadcasted_iota(jnp.int32, (n_sb, TQ), 1)
    cur = tq // SEL_BLOCK
    valid = jrow <= cur
    forced = jnp.where(jrow == 0, 1.0, jnp.where(jrow == cur, 1.0, jnp.where(jrow == cur - 1, 1.0, 0.0)))
    score = jnp.where(valid, imp + FORCE_BONUS * forced, -1.0)
    rank = jnp.zeros((n_sb, TQ), F32)
    for jp in range(n_sb):
        other = jnp.broadcast_to(score[jp:jp + 1, :], (n_sb, TQ))
        before = jnp.where(jrow > jp, 1.0, 0.0)
        rank = rank + jnp.where(other > score, 1.0, jnp.where(other == score, before, 0.0))
    chosen = jnp.where(valid, jnp.where(rank < float(min(N_SELECT, n_sb)), 0.0, NEG_INF), NEG_INF)

    srow = lax.broadcasted_iota(jnp.int32, (LANES - n_sb, TQ), 0) + n_sb
    side_t = jnp.concatenate([chosen, jnp.where(srow == SIDE_OFF, NEG_INF, 0.0)], axis=0)
    side = side_t.T
    for r in range(r_heads):
        far = jnp.full((TQ, LANES), rb_ref[N_BUCKETS - 1, r_heads * g + r], F32)
        far_hi = far.astype(BF16).astype(F32)
        side_r = jnp.where(lane == SIDE_FAR_HI, far_hi, jnp.where(lane == SIDE_FAR_LO, far - far_hi, side))
        side_ref[r * TQ:(r + 1) * TQ, :] = side_r.astype(BF16)

    def key_side(kj, side_kind):
        return jnp.concatenate([ks_ref[tile_rows(kj), :], kside_ref[side_kind, tile_rows(kj), :]], axis=1)

    def q_side():
        return jnp.concatenate([qs_ref[...], side_ref[...]], axis=1)

    sn = scores(q_side(), [key_side(qi, KSIDE_NEAR), key_side(kj1, KSIDE_NEAR)], (TILE_DIAG, near_type))
    m0 = row_max(sn)
    m_ref[...] = jnp.broadcast_to(m0, (rows, LANES))
    acc_ref[...] = weighted([jnp.exp(s - m0) for s in sn], vs_ref, (qi, kj1))

    def far_pair(idx):
        ka = qi - 2 - 2 * idx
        kb_raw = ka - 1
        kb = jnp.maximum(kb_raw, 0)
        sf = scores(q_side(), [key_side(ka, KSIDE_FAR),
                               key_side(kb, jnp.where(kb_raw >= 0, KSIDE_FAR, KSIDE_OFF))], (None, None))
        m_prev = m_ref[...]
        m_new = jnp.maximum(m_prev, row_max(sf))
        alpha = jnp.exp(m_prev - m_new)
        m_wide = jnp.tile(m_new, (1, TK // LANES))
        acc_ref[...] = (jnp.tile(alpha, (1, 2)) * acc_ref[...]
                        + weighted([jnp.exp(s - m_wide) for s in sf], vs_ref, (ka, kb)))
        m_ref[...] = m_new

    def placed(lo, hi):
        lo = jnp.where(par == 0, lo, pltpu.roll(lo, HEAD_DIM, 1))
        hi = jnp.where(par == 1, hi, pltpu.roll(hi, HEAD_DIM, 1))
        return jnp.where(lane < HEAD_DIM, lo, hi)

    for blk in range(2):
        part = [gates[r][0] * o_cmp[r] + gates[r][2] * o_win[r * TQ:(r + 1) * TQ, :] for r in (2 * blk, 2 * blk + 1)]
        out_ref[:, blk * LANES:(blk + 1) * LANES] = placed(*part)

    def finish():
        o_slc = normalised(acc_ref[...])
        for blk in range(2):
            part = [gates[r][1] * o_slc[r * TQ:(r + 1) * TQ, :] for r in (2 * blk, 2 * blk + 1)]
            out_ref[:, blk * LANES:(blk + 1) * LANES] += placed(*part)

    return far_pair, finish


def _nsa_attn(rel_bias, att, gate, kvc, bgate, tiles, cmpb, batch, seq_len):
    n = att.shape[0]
    nq = seq_len // TQ
    n_sb = seq_len // SEL_BLOCK
    ng = N_KV_GROUPS
    r = HEADS_PER_GROUP
    ov = _overlap_constants(seq_len)
    kside = _key_side_constants(seq_len)
    qw = r * HEAD_DIM
    kcol = ATTN_WIDTH + 2 * KV_WIDTH
    ns = ATTN_SLOTS if batch % ATTN_SLOTS == 0 else 1

    def slot_specs(s):
        seq = lambda bb: bb * ns + s
        kvb = lambda off: (lambda g, qi, bb: (seq(bb), (kcol + off * KV_WIDTH) // LANES + g // 2))
        return [
            pl.BlockSpec((TQ, qw), lambda g, qi, bb: (seq(bb) * nq + qi, g)),
            pl.BlockSpec((None, None, N_CMP_PAD, LANES), lambda g, qi, bb: (0, seq(bb) * ng + g, 0, 0)),
            pl.BlockSpec((None, None, N_CMP_PAD, LANES), lambda g, qi, bb: (1, seq(bb) * ng + g, 0, 0)),
            pl.BlockSpec((seq_len, LANES), kvb(0)),
            pl.BlockSpec((seq_len, LANES), kvb(1)),
            pl.BlockSpec((seq_len, LANES), kvb(2)),
            pl.BlockSpec((seq_len, LANES), kvb(3)),
            pl.BlockSpec((TQ, GATE_PAD), lambda g, qi, bb: (seq(bb) * nq + qi, 0)),
        ]

    shared_specs = [
        pl.BlockSpec((1, GATE_PAD), lambda g, qi, bb: (0, 0)),
        pl.BlockSpec((None, N_TILE_TYPES, r, TQ, TK), lambda g, qi, bb: (g, 0, 0, 0, 0)),
        pl.BlockSpec((None, r, TQ, N_CMP_PAD), lambda g, qi, bb: (g, 0, qi, 0)),
        pl.BlockSpec((n_sb, N_CMP_PAD), lambda g, qi, bb: (0, 0)),
        pl.BlockSpec((3, seq_len, LANES), lambda g, qi, bb: (0, 0, 0)),
    ]
    slot_scratch = [
        pltpu.VMEM((r * TQ, LANES), BF16),
        pltpu.VMEM((r * TQ, LANES), BF16),
        pltpu.VMEM((r * TQ, LANES), F32),
        pltpu.VMEM((r * TQ, 2 * LANES), F32),
    ]
    outs = pl.pallas_call(
        functools.partial(_nsa_kernel, n_sb, ns),
        grid=(ng, nq, batch // ns),
        in_specs=([pl.BlockSpec(memory_space=pltpu.SMEM)]
                  + [spec for s in range(ns) for spec in slot_specs(s)] + shared_specs),
        out_specs=pl.BlockSpec((None, ns, TQ, qw), lambda g, qi, bb: (bb, 0, qi, g)),
        out_shape=jax.ShapeDtypeStruct((batch // ns, ns, seq_len, ATTN_WIDTH), F32),
        scratch_shapes=slot_scratch * ns,
        compiler_params=pltpu.CompilerParams(dimension_semantics=("arbitrary", "arbitrary", "arbitrary"),
                                             vmem_limit_bytes=VMEM_LIMIT),
        name="nsa_attn",
    )(rel_bias, *([att, kvc, kvc, att, att, att, att, gate] * ns), bgate, tiles, cmpb,
      jnp.asarray(ov, BF16), jnp.asarray(kside, BF16))
    return outs.reshape(n, ATTN_WIDTH)


def kernel(x, mix_norm_g, w_in, b_gate, cmp_pe_k, cmp_pe_v, cmp_k_w1, cmp_k_w2, cmp_v_w1, cmp_v_w2, rel_bias, rnn_conv_w, rnn_conv_b, rg_a_w, rg_a_b, rg_x_w, rg_x_b, rg_lambda, attn_out_g, rnn_out_g, w_out, ffn_norm_g, w_ffn_gate, w_ffn_up, ffn_conv_w, ffn_conv_b, w_ffn_down, final_norm_g):
    batch, seq_len, d_model = x.shape
    assert d_model == D_MODEL and w_in.shape[0] == 1, "single-layer model of the stated width"
    assert seq_len // CMP_STRIDE == N_CMP_PAD and seq_len % TM_FFN == 0
    n = batch * seq_len
    ng = N_KV_GROUPS
    x2 = x.reshape(n, D_MODEL)

    w0 = w_in[0].astype(BF16)
    gate0 = ATT_COLS
    rx0 = gate0 + N_BRANCH * N_Q_HEADS
    w_rnn = jnp.concatenate(
        [w0[:, rx0:rx0 + 2 * RNN_WIDTH], w0[:, gate0:rx0],
         jnp.zeros((D_MODEL, GATE_PAD - N_BRANCH * N_Q_HEADS), BF16)], axis=1)
    att, gate, chunks, nr = _in_proj(
        x2, mix_norm_g[0][None], w0, w_rnn, rnn_conv_w[0], rnn_conv_b[0][None],
        _block_diag(rg_a_w[0]).astype(BF16), rg_a_b[0][None], _block_diag(rg_x_w[0]).astype(BF16), rg_x_b[0][None],
        rg_lambda[0][None], rnn_out_g[0][None], seq_len)

    pe = jnp.stack([_pair_pe(cmp_pe_k[0]), _pair_pe(cmp_pe_v[0])]).astype(BF16)
    w1p = jnp.stack([_pair_weights(cmp_k_w1[0]), _pair_weights(cmp_v_w1[0])]).astype(BF16)
    w2 = jnp.stack([jnp.tile(cmp_k_w2[0], (1, 2)), jnp.tile(cmp_v_w2[0], (1, 2))]).astype(BF16)
    kvc = _cmp_tokens(chunks, pe, w1p, w2, batch).reshape(2, batch * ng, N_CMP_PAD, LANES)

    tiles, cmpb = _bias_tiles(rel_bias, seq_len)
    bgate = jnp.pad(b_gate[0], (0, GATE_PAD - b_gate.shape[1]))[None]
    o_attn = _nsa_attn(rel_bias, att, gate, kvc, bgate, tiles, cmpb, batch, seq_len)

    h, y = _out_proj(o_attn, nr, x2, attn_out_g[0][None], ffn_norm_g[0][None], w_out[0].astype(BF16))
    out = _conv_ffn(y, h, w_ffn_gate[0].astype(BF16), w_ffn_up[0].astype(BF16), w_ffn_down[0].astype(BF16),
                    ffn_conv_w[0], ffn_conv_b[0][None], final_norm_g[None], seq_len)
    return out.reshape(batch, seq_len, D_MODEL)
```

```python
import functools
import math

import jax
import jax.numpy as jnp
import numpy as np
from jax import lax
from jax.experimental import pallas as pl
from jax.experimental.pallas import tpu as pltpu

F32 = jnp.float32
BF16 = jnp.bfloat16

D_MODEL = 2048
N_Q_HEADS = 16
N_KV_GROUPS = 4
HEADS_PER_GROUP = 4
HEAD_DIM = 64
ATTN_WIDTH = 1024
KV_WIDTH = 256
SCALE = HEAD_DIM ** -0.5
CMP_BLOCK = 32
CMP_STRIDE = 16
CMP_HIDDEN = 256
SEL_BLOCK = 64
N_SELECT = 16
WINDOW = 512
N_BRANCH = 3
RNN_WIDTH = 1024
RNN_BLOCKS = 16
RNN_CONV_WIDTH = 4
RG_LRU_C = 8.0
N_BUCKETS = 32
MAX_DISTANCE = 128
D_FF = 5632
NORM_EPS = 1e-6
NEG_INF = -1e30
FORCE_BONUS = 1e3
LOG2E = 1.0 / math.log(2.0)
GELU_C0 = math.sqrt(2.0 / math.pi)
GELU_C1 = GELU_C0 * 0.044715

LANES = 128
SUBLANES = 8
VMEM_LIMIT = 56 * 1024 * 1024

TM_PROJ = 512
COL_CHUNK = 512
TM_FFN = 1024
TF_FFN = 512
FFN_ROW_BLOCKS = 2
H_PIECE = 256
TS_RNN = 512
CB_RNN = 256
RNN_ROWS = 256
PROJ_CHUNK = 256
TQ = 256
TK = 256
ATTN_SLOTS = 2
ROW_SPLIT = 512
N_CMP_PAD = 128
GATE_PAD = 128

ATT_COLS = ATTN_WIDTH + 6 * KV_WIDTH
RNN_COLS = 2 * RNN_WIDTH + GATE_PAD
CMP_COL0 = ATTN_WIDTH
N_CHUNK_SLABS = 2 * KV_WIDTH // LANES
CHUNK2_W = CMP_STRIDE * LANES


def _rms(x, g):
    return x * lax.rsqrt(jnp.mean(x * x, axis=-1, keepdims=True) + NORM_EPS) * g


def _gelu(x):
    return jax.nn.gelu(x, approximate=True)


def _dot(a, b):
    return jnp.dot(a, b, preferred_element_type=F32)


def _dot_nt(a, b):
    return lax.dot_general(a, b, (((1,), (1,)), ((), ())), preferred_element_type=F32)


def _dot_tn(a, b):
    return lax.dot_general(a, b, (((0,), (0,)), ((), ())), preferred_element_type=F32)


def _out_proj_kernel(oa_ref, nr_ref, x_ref, ga_ref, gf_ref, w_ref, h_ref, y_ref):
    na = _rms(oa_ref[...], ga_ref[...]).astype(BF16)
    nr = nr_ref[...]
    for c0 in range(0, D_MODEL, COL_CHUNK):
        sl = slice(c0, c0 + COL_CHUNK)
        h_ref[:, sl] = (x_ref[:, sl] + _dot(na, w_ref[0:ATTN_WIDTH, sl])
                        + _dot(nr, w_ref[ATTN_WIDTH:D_MODEL, sl]))
    y_ref[...] = _rms(h_ref[...], gf_ref[...]).astype(BF16)


def _out_proj(o_attn, nr, x2, ga, gf, w):
    n = x2.shape[0]
    return pl.pallas_call(
        _out_proj_kernel,
        grid=(n // TM_PROJ,),
        in_specs=[
            pl.BlockSpec((TM_PROJ, ATTN_WIDTH), lambda i: (i, 0)),
            pl.BlockSpec((TM_PROJ, RNN_WIDTH), lambda i: (i, 0)),
            pl.BlockSpec((TM_PROJ, D_MODEL), lambda i: (i, 0)),
            pl.BlockSpec((1, ATTN_WIDTH), lambda i: (0, 0)),
            pl.BlockSpec((1, D_MODEL), lambda i: (0, 0)),
            pl.BlockSpec((D_MODEL, D_MODEL), lambda i: (0, 0), pipeline_mode=pl.Buffered(1)),
        ],
        out_specs=[
            pl.BlockSpec((TM_PROJ, D_MODEL), lambda i: (i, 0)),
            pl.BlockSpec((TM_PROJ, D_MODEL), lambda i: (i, 0)),
        ],
        out_shape=[
            jax.ShapeDtypeStruct((n, D_MODEL), F32),
            jax.ShapeDtypeStruct((n, D_MODEL), BF16),
        ],
        compiler_params=pltpu.CompilerParams(dimension_semantics=("parallel",), vmem_limit_bytes=VMEM_LIMIT),
        name="out_proj",
    )(o_attn, nr, x2, ga, gf, w)


def _ffn_kernel(tiles_per_seq, y_ref, h_ref, wg_ref, wu_ref, wd_ref, cw_ref, cb_ref, gf_ref, out_ref, carry_ref):
    i = pl.program_id(0)
    j = pl.program_id(1)
    nj = pl.num_programs(1)

    @pl.when((i % tiles_per_seq) == 0)
    def _():
        carry_ref[j] = jnp.zeros((SUBLANES, TF_FFN), F32)

    @pl.when(j == 0)
    def _():
        out_ref[...] = jnp.zeros(out_ref.shape, F32)

    @pl.when(j < D_MODEL // H_PIECE)
    def _():
        cols = pl.ds(pl.multiple_of(j * H_PIECE, H_PIECE), H_PIECE)
        out_ref[:, cols] += h_ref[...]

    row = lax.broadcasted_iota(jnp.int32, (SUBLANES, TF_FFN), 0)
    cw = cw_ref[...]
    cb = cb_ref[...]
    prev = carry_ref[j]
    rows_blk = TM_FFN // FFN_ROW_BLOCKS
    for blk in range(FFN_ROW_BLOCKS):
        sl = slice(blk * rows_blk, (blk + 1) * rows_blk)
        y = y_ref[sl, :]
        gt = _dot(y, wg_ref[...])
        p1 = prev[SUBLANES - 1:SUBLANES, :]
        p2 = prev[SUBLANES - 2:SUBLANES - 1, :]
        r1 = pltpu.roll(gt, 1, 0)
        r2 = pltpu.roll(gt, 2, 0)
        h1 = jnp.where(row == 0, p1, r1[0:SUBLANES])
        h2 = jnp.where(row == 0, p2, jnp.where(row == 1, p1, r2[0:SUBLANES]))
        s1 = jnp.concatenate([h1, r1[SUBLANES:]], axis=0)
        s2 = jnp.concatenate([h2, r2[SUBLANES:]], axis=0)
        conv = cw[2:3, :] * gt + cw[1:2, :] * s1 + cw[0:1, :] * s2 + cb
        inner = conv * (GELU_C0 + GELU_C1 * (conv * conv))
        hx = 0.5 * conv
        u = hx + hx * jnp.tanh(inner)
        z = (u * _dot(y, wu_ref[...])).astype(BF16)
        out_ref[sl, :] += _dot(z, wd_ref[...])
        prev = gt[rows_blk - SUBLANES:rows_blk, :]
    carry_ref[j] = prev

    @pl.when(j == nj - 1)
    def _():
        out_ref[...] = _rms(out_ref[...], gf_ref[...])


def _conv_ffn(y, h, wg, wu, wd, cw, cb, gf, seq_len):
    n = y.shape[0]
    nj = D_FF // TF_FFN
    return pl.pallas_call(
        functools.partial(_ffn_kernel, seq_len // TM_FFN),
        grid=(n // TM_FFN, nj),
        in_specs=[
            pl.BlockSpec((TM_FFN, D_MODEL), lambda i, j: (i, 0)),
            pl.BlockSpec((TM_FFN, H_PIECE), lambda i, j: (i, jnp.minimum(j, D_MODEL // H_PIECE - 1))),
            pl.BlockSpec((D_MODEL, TF_FFN), lambda i, j: (0, j)),
            pl.BlockSpec((D_MODEL, TF_FFN), lambda i, j: (0, j)),
            pl.BlockSpec((TF_FFN, D_MODEL), lambda i, j: (j, 0)),
            pl.BlockSpec((3, TF_FFN), lambda i, j: (0, j)),
            pl.BlockSpec((1, TF_FFN), lambda i, j: (0, j)),
            pl.BlockSpec((1, D_MODEL), lambda i, j: (0, 0)),
        ],
        out_specs=pl.BlockSpec((TM_FFN, D_MODEL), lambda i, j: (i, 0)),
        out_shape=jax.ShapeDtypeStruct((n, D_MODEL), F32),
        scratch_shapes=[pltpu.VMEM((nj, SUBLANES, TF_FFN), F32)],
        compiler_params=pltpu.CompilerParams(dimension_semantics=("arbitrary", "arbitrary"),
                                             vmem_limit_bytes=VMEM_LIMIT),
        name="conv_ffn",
    )(y, h, wg, wu, wd, cw, cb, gf)


def _rglru_pieces(rx_ref, ry_ref, cw_ref, cb_ref, wa_ref, ba_ref, wx_ref, bx_ref, lam_ref, g_ref,
                  out_ref, tail_ref, hc_ref, a_ref, u_ref, xbuf_ref, ygate_ref):
    rmod = lax.broadcasted_iota(jnp.int32, (RNN_ROWS // SUBLANES, SUBLANES, CB_RNN), 1)
    lam = lam_ref[...]
    nlam = -lam
    softplus = jnp.maximum(nlam, 0.0) + jnp.log1p(jnp.exp(-jnp.abs(nlam)))
    log_a_unit = -RG_LRU_C * softplus

    def stage():
        xbuf_ref[0:SUBLANES, :] = tail_ref[...]
        xbuf_ref[SUBLANES:SUBLANES + TS_RNN, :] = rx_ref[...]
        tail_ref[...] = rx_ref[TS_RNN - SUBLANES:TS_RNN, :]

    def stage_gate(lo, hi):
        ygate_ref[lo:hi, :] = _gelu(ry_ref[lo:hi, :])

    def channel_block(cb, lo):
        sl = slice(cb * CB_RNN, (cb + 1) * CB_RNN)
        hi = lo + RNN_ROWS
        cw = cw_ref[:, sl]
        xr = cb_ref[:, sl]
        for d in (0, 1, 2, 3):
            xr = xr + cw[3 - d:4 - d, :] * xbuf_ref[SUBLANES - d + lo:SUBLANES - d + hi, sl]
        xb = xr.astype(BF16)
        r = jax.nn.sigmoid(_dot(xb, wa_ref[cb]) + ba_ref[:, sl])
        gi = jax.nn.sigmoid(_dot(xb, wx_ref[cb]) + bx_ref[:, sl])
        log_a = log_a_unit[:, sl] * r
        a = jnp.exp(log_a)
        th = jnp.tanh(log_a)
        w = -2.0 * th / (1.0 - th)
        u = jnp.where(w > 0.0, w * lax.rsqrt(w), 0.0) * (gi * xr)
        a = a.reshape(RNN_ROWS // SUBLANES, SUBLANES, CB_RNN)
        u = u.reshape(RNN_ROWS // SUBLANES, SUBLANES, CB_RNN)
        for d in (1, 2, 4):
            keep = rmod >= d
            a_s = pltpu.roll(a, d, 1)
            u_s = pltpu.roll(u, d, 1)
            u = jnp.where(keep, a * u_s + u, u)
            a = jnp.where(keep, a * a_s, a)
        a_ref[lo:hi, sl] = a.reshape(RNN_ROWS, CB_RNN)
        u_ref[lo:hi, sl] = u.reshape(RNN_ROWS, CB_RNN)

    def carry(lo, hi):
        hprev = hc_ref[...]
        for r0 in range(lo, hi, SUBLANES):
            h8 = u_ref[r0:r0 + SUBLANES, :] + a_ref[r0:r0 + SUBLANES, :] * hprev
            u_ref[r0:r0 + SUBLANES, :] = h8
            hprev = jnp.broadcast_to(h8[SUBLANES - 1:SUBLANES, :], (SUBLANES, RNN_WIDTH))
        hc_ref[...] = hprev

    def finish(lo, hi):
        out_ref[lo:hi, :] = _rms(u_ref[lo:hi, :] * ygate_ref[lo:hi, :], g_ref[...]).astype(BF16)

    quarter, half = TS_RNN // 4, TS_RNN // 2
    readers = [stage] + [functools.partial(stage_gate, k * half, (k + 1) * half) for k in range(2)]
    others = ([functools.partial(channel_block, cb, lo)
               for cb in range(RNN_WIDTH // CB_RNN) for lo in range(0, TS_RNN, RNN_ROWS)]
              + [functools.partial(carry, k * quarter, (k + 1) * quarter) for k in range(4)]
              + [functools.partial(finish, k * half, (k + 1) * half) for k in range(2)])
    return readers, others


def _in_proj_kernel(tiles_per_seq, x_ref, g_ref, w_ref, wr_ref,
                    cw_ref, cb_ref, wa_ref, ba_ref, wx_ref, bx_ref, lam_ref, gr_ref,
                    att_ref, gate_ref, chunk_ref, nr_ref,
                    stage_ref, buf_ref, tail_ref, hc_ref, a_ref, u_ref, xbuf_ref, ygate_ref):
    i = pl.program_id(0)

    @pl.when(i == 0)
    def _():
        buf_ref[...] = jnp.zeros(buf_ref.shape, F32)

    @pl.when(jnp.maximum(i - 1, 0) % tiles_per_seq == 0)
    def _():
        tail_ref[...] = jnp.zeros(tail_ref.shape, F32)
        hc_ref[...] = jnp.zeros(hc_ref.shape, F32)

    buffer_readers, other_pieces = _rglru_pieces(
        buf_ref.at[:, 0:RNN_WIDTH], buf_ref.at[:, RNN_WIDTH:2 * RNN_WIDTH],
        cw_ref, cb_ref, wa_ref, ba_ref, wx_ref, bx_ref, lam_ref, gr_ref,
        nr_ref, tail_ref, hc_ref, a_ref, u_ref, xbuf_ref, ygate_ref)
    xn = _rms(x_ref[...], g_ref[...]).astype(BF16)

    def att_chunk(c0):
        res = _dot(xn, w_ref[:, c0:c0 + PROJ_CHUNK])
        att_ref[:, c0:c0 + PROJ_CHUNK] = res.astype(BF16)
        if CMP_COL0 <= c0 < CMP_COL0 + N_CHUNK_SLABS * LANES:
            q0 = (c0 - CMP_COL0) // LANES
            slabs = range(q0, q0 + PROJ_CHUNK // LANES)
            for q in slabs:
                stage_ref[q] = res[:, (q - q0) * LANES:(q - q0 + 1) * LANES]
            for l in range(CMP_STRIDE):
                for q in slabs:
                    tok = stage_ref[q, pl.ds(l, TM_PROJ // CMP_STRIDE, stride=CMP_STRIDE), :]
                    chunk_ref[q, :, l * LANES:(l + 1) * LANES] = tok.astype(BF16)

    def rnn_chunk(c0):
        buf_ref[:, c0:c0 + PROJ_CHUNK] = _dot(xn, wr_ref[:, c0:c0 + PROJ_CHUNK])

    def gate_chunk():
        gate_ref[...] = _dot(xn, wr_ref[:, 2 * RNN_WIDTH:RNN_COLS])

    def interleave(matmuls, others):
        done = 0
        for k, piece in enumerate(matmuls):
            piece()
            upto = (k + 1) * len(others) // len(matmuls)
            for other in others[done:upto]:
                other()
            done = upto

    keep_buffer = [functools.partial(att_chunk, c0) for c0 in range(0, ATT_COLS, PROJ_CHUNK)] + [gate_chunk]
    refill = [functools.partial(rnn_chunk, c0) for c0 in range(0, 2 * RNN_WIDTH, PROJ_CHUNK)]
    pieces = buffer_readers + other_pieces
    assert len(keep_buffer) * len(pieces) // (len(keep_buffer) + len(refill)) >= len(buffer_readers)
    interleave(keep_buffer + refill, pieces)


def _in_proj(x2, g, w, w_rnn, cw, cb, wa, ba, wx, bx, lam, gr, seq_len):
    n = x2.shape[0]
    nt = n // TM_PROJ
    assert CMP_COL0 % PROJ_CHUNK == 0 and (N_CHUNK_SLABS * LANES) % PROJ_CHUNK == 0 and TS_RNN == TM_PROJ
    nb = RNN_WIDTH // CB_RNN
    cur = lambda i: jnp.minimum(i, nt - 1)
    vec = pl.BlockSpec((1, RNN_WIDTH), lambda i: (0, 0))
    wspec = pl.BlockSpec((nb, CB_RNN, CB_RNN), lambda i: (0, 0, 0))
    return pl.pallas_call(
        functools.partial(_in_proj_kernel, seq_len // TM_PROJ),
        grid=(nt + 1,),
        in_specs=[
            pl.BlockSpec((TM_PROJ, D_MODEL), lambda i: (cur(i), 0)),
            pl.BlockSpec((1, D_MODEL), lambda i: (0, 0)),
            pl.BlockSpec((D_MODEL, ATT_COLS), lambda i: (0, 0), pipeline_mode=pl.Buffered(1)),
            pl.BlockSpec((D_MODEL, RNN_COLS), lambda i: (0, 0), pipeline_mode=pl.Buffered(1)),
            pl.BlockSpec((RNN_CONV_WIDTH, RNN_WIDTH), lambda i: (0, 0)),
            vec, wspec, vec, wspec, vec, vec, vec,
        ],
        out_specs=[
            pl.BlockSpec((TM_PROJ, ATT_COLS), lambda i: (cur(i), 0)),
            pl.BlockSpec((TM_PROJ, GATE_PAD), lambda i: (cur(i), 0)),
            pl.BlockSpec((N_CHUNK_SLABS, TM_PROJ // CMP_STRIDE, CHUNK2_W), lambda i: (0, cur(i), 0)),
            pl.BlockSpec((TM_PROJ, RNN_WIDTH), lambda i: (jnp.maximum(i - 1, 0), 0)),
        ],
        out_shape=[
            jax.ShapeDtypeStruct((n, ATT_COLS), BF16),
            jax.ShapeDtypeStruct((n, GATE_PAD), F32),
            jax.ShapeDtypeStruct((N_CHUNK_SLABS, n // CMP_STRIDE, CHUNK2_W), BF16),
            jax.ShapeDtypeStruct((n, RNN_WIDTH), BF16),
        ],
        scratch_shapes=[
            pltpu.VMEM((N_CHUNK_SLABS, TM_PROJ, LANES), F32),
            pltpu.VMEM((TM_PROJ, 2 * RNN_WIDTH), F32),
            pltpu.VMEM((SUBLANES, RNN_WIDTH), F32),
            pltpu.VMEM((SUBLANES, RNN_WIDTH), F32),
            pltpu.VMEM((TS_RNN, RNN_WIDTH), F32),
            pltpu.VMEM((TS_RNN, RNN_WIDTH), F32),
            pltpu.VMEM((SUBLANES + TS_RNN, RNN_WIDTH), F32),
            pltpu.VMEM((TS_RNN, RNN_WIDTH), F32),
        ],
        compiler_params=pltpu.CompilerParams(dimension_semantics=("arbitrary",), vmem_limit_bytes=VMEM_LIMIT),
        name="in_proj",
    )(x2, g, w, w_rnn, cw, cb, wa, ba, wx, bx, lam, gr)


def _block_diag(w):
    per = CB_RNN // (RNN_WIDTH // RNN_BLOCKS)
    bd = w.shape[-1]
    w4 = w.reshape(RNN_BLOCKS // per, per, bd, bd)
    eye = jnp.eye(per, dtype=w.dtype)
    return jnp.einsum('cpij,pq->cpiqj', w4, eye).reshape(RNN_BLOCKS // per, per * bd, per * bd)


CMP_BATCHES = 8


def _cmp_tokens_kernel(x_ref, pe_ref, w1_ref, w2_ref, out_ref):
    x = x_ref[...]
    rows = x.shape[0]
    ya = _dot(x, w1_ref[0])
    yb = _dot(x, w1_ref[1])
    pterm = (_dot(jnp.broadcast_to(pe_ref[0], (SUBLANES, CHUNK2_W)), w1_ref[0])
             + _dot(jnp.broadcast_to(pe_ref[1], (SUBLANES, CHUNK2_W)), w1_ref[1]))[0:1, :]
    act = _gelu(ya + pltpu.roll(yb, rows - 1, 0) + pterm).astype(BF16)
    for e in range(2):
        tok = _dot(act[:, e * CMP_HIDDEN:(e + 1) * CMP_HIDDEN], w2_ref[...]).astype(BF16)
        out_ref[:, e] = tok.reshape(rows // N_CMP_PAD, N_CMP_PAD, LANES)


def _cmp_tokens(chunks, pe, w1p, w2, batch):
    nb = min(CMP_BATCHES, batch)
    tr = nb * N_CMP_PAD
    return pl.pallas_call(
        _cmp_tokens_kernel,
        grid=(N_CHUNK_SLABS, batch // nb),
        in_specs=[
            pl.BlockSpec((None, tr, CHUNK2_W), lambda s, i: (s, i, 0)),
            pl.BlockSpec((None, 2, 1, CHUNK2_W), lambda s, i: (s // 2, 0, 0, 0)),
            pl.BlockSpec((None, 2, CHUNK2_W, 2 * CMP_HIDDEN), lambda s, i: (s // 2, 0, 0, 0)),
            pl.BlockSpec((None, CMP_HIDDEN, LANES), lambda s, i: (s // 2, 0, 0)),
        ],
        out_specs=pl.BlockSpec((None, nb, None, 2, N_CMP_PAD, LANES), lambda s, i: (s // 2, i, s % 2, 0, 0, 0)),
        out_shape=jax.ShapeDtypeStruct((2, batch, 2, 2, N_CMP_PAD, LANES), BF16),
        compiler_params=pltpu.CompilerParams(dimension_semantics=("parallel", "parallel")),
        name="cmp_tokens",
    )(chunks, pe, w1p, w2)


def _pair_weights(w1):
    half = CMP_STRIDE * HEAD_DIM
    w = w1.reshape(2, CMP_STRIDE, HEAD_DIM, CMP_HIDDEN)
    eye = jnp.eye(2, dtype=w1.dtype)
    return jnp.einsum('hldj,ef->hledfj', w, eye).reshape(2, 2 * half, 2 * CMP_HIDDEN)


def _pair_pe(pe):
    p = pe.reshape(2, CMP_STRIDE, 1, HEAD_DIM)
    return jnp.broadcast_to(p, (2, CMP_STRIDE, 2, HEAD_DIM)).reshape(2, 1, CHUNK2_W)


N_TILE_TYPES = 4
TILE_DIAG, TILE_NEAR, TILE_WIN_LAST, TILE_NONE = 0, 1, 2, 3


def _t5_bucket_np(dist):
    n = np.maximum(dist, 0)
    max_exact = N_BUCKETS // 2
    nf = np.maximum(n, 1).astype(np.float64)
    large = max_exact + (np.log(nf / max_exact) / math.log(MAX_DISTANCE / max_exact)
                         * (N_BUCKETS - max_exact)).astype(np.int32)
    large = np.minimum(large, N_BUCKETS - 1)
    return np.where(n < max_exact, n, large).astype(np.int32)


def _bucket_constants(seq_len):
    i = np.arange(TQ)[:, None]
    j = np.arange(TK)[None, :]
    d0 = i - j
    d1 = TK + i - j
    d2 = 2 * TK + i - j
    valid = [d0 >= 0, np.ones_like(d1, bool), d2 < WINDOW, np.zeros_like(d0, bool)]
    tiles = np.stack([np.where(v, _t5_bucket_np(d), -1) for d, v in zip([d0, d1, d2, d0], valid)])
    t = np.arange(seq_len)[:, None]
    c = np.arange(N_CMP_PAD)[None, :]
    dc = t - (c * CMP_STRIDE + CMP_BLOCK - 1)
    cmp = np.where(dc >= 0, _t5_bucket_np(dc), -1)
    return tiles.astype(np.int32), cmp.astype(np.int32)


def _bias_kernel(tile_buckets, rb_ref, tb_ref, cbk_ref, tiles_ref, cmpb_ref):
    h = pl.program_id(0)

    def lookup(bkt, buckets):
        acc = jnp.full(bkt.shape, NEG_INF, F32)
        for b in buckets:
            acc = jnp.where(bkt == b, rb_ref[b, h] * LOG2E, acc)
        return acc

    for t in range(N_TILE_TYPES):
        tiles_ref[t] = lookup(tb_ref[t], tile_buckets[t])
    cmpb_ref[...] = lookup(cbk_ref[...], range(N_BUCKETS))


def _bias_tiles(rel_bias, seq_len):
    tb, cbk = _bucket_constants(seq_len)
    tile_buckets = tuple(tuple(int(b) for b in np.unique(t[t >= 0])) for t in tb)
    r = HEADS_PER_GROUP
    return pl.pallas_call(
        functools.partial(_bias_kernel, tile_buckets),
        grid=(N_Q_HEADS,),
        in_specs=[
            pl.BlockSpec(memory_space=pltpu.SMEM),
            pl.BlockSpec((N_TILE_TYPES, TQ, TK), lambda h: (0, 0, 0)),
            pl.BlockSpec((seq_len, N_CMP_PAD), lambda h: (0, 0)),
        ],
        out_specs=[
            pl.BlockSpec((None, N_TILE_TYPES, None, TQ, TK), lambda h: (h // r, 0, h % r, 0, 0)),
            pl.BlockSpec((None, None, seq_len, N_CMP_PAD), lambda h: (h // r, h % r, 0, 0)),
        ],
        out_shape=[
            jax.ShapeDtypeStruct((N_KV_GROUPS, N_TILE_TYPES, r, TQ, TK), F32),
            jax.ShapeDtypeStruct((N_KV_GROUPS, r, seq_len, N_CMP_PAD), F32),
        ],
        compiler_params=pltpu.CompilerParams(dimension_semantics=("parallel",)),
        name="bias_tiles",
    )(rel_bias, jnp.asarray(tb), jnp.asarray(cbk))


SIDE_FAR_HI, SIDE_FAR_LO, SIDE_OFF = 32, 33, 34
KSIDE_NEAR, KSIDE_FAR, KSIDE_OFF = 0, 1, 2


def _key_side_constants(seq_len):
    s = np.arange(seq_len)
    side = np.zeros((3, seq_len, LANES), np.float32)
    side[:, s, s // SEL_BLOCK] = 1.0
    side[KSIDE_FAR, :, SIDE_FAR_HI] = 1.0
    side[KSIDE_FAR, :, SIDE_FAR_LO] = 1.0
    side[KSIDE_OFF, :, SIDE_OFF] = 1.0
    return side


def _overlap_constants(seq_len):
    n_sb = seq_len // SEL_BLOCK
    c = np.arange(N_CMP_PAD)[None, :]
    j = np.arange(n_sb)[:, None]
    lo = np.maximum(c * CMP_STRIDE, j * SEL_BLOCK)
    hi = np.minimum(c * CMP_STRIDE + CMP_BLOCK, (j + 1) * SEL_BLOCK)
    ov = np.maximum(hi - lo, 0) / CMP_BLOCK
    ov[:, N_CMP_PAD - 1] = 0.0
    return ov.astype(np.float32)


def _nsa_kernel(n_sb, n_slots, rb_ref, *refs):
    per_slot_in = 8
    slot_in = [refs[s * per_slot_in:(s + 1) * per_slot_in] for s in range(n_slots)]
    shared = refs[n_slots * per_slot_in:n_slots * per_slot_in + 5]
    out_ref = refs[n_slots * per_slot_in + 5]
    scratch = refs[n_slots * per_slot_in + 6:]
    g = pl.program_id(0)
    qi = pl.program_id(1)
    slots = [_nsa_slot(n_sb, g, qi, rb_ref, *slot_in[s], *shared, out_ref.at[s], *scratch[4 * s:4 * s + 4])
             for s in range(n_slots)]

    def far_pairs(idx, carry):
        for far_pair, _ in slots:
            far_pair(idx)
        return carry

    lax.fori_loop(0, jnp.maximum(qi, 1) // 2, far_pairs, 0)
    for _, finish in slots:
        finish()


def _nsa_slot(n_sb, g, qi, rb_ref, q_ref, kc_ref, vc_ref, ks_ref, vs_ref, kw_ref, vw_ref, gate_ref, bg_ref,
              tiles_ref, cmpb_ref, ov_ref, kside_ref, out_ref,
              qs_ref, side_ref, m_ref, acc_ref):
    par = g % 2
    r_heads = HEADS_PER_GROUP
    rows = r_heads * TQ
    lane = lax.broadcasted_iota(jnp.int32, (TQ, LANES), 1)
    in_half = (lane // HEAD_DIM) == par

    for blk in range(2):
        qb = q_ref[:, blk * LANES:(blk + 1) * LANES].astype(F32)
        qr = pltpu.roll(qb, HEAD_DIM, 1)
        for hh in range(2):
            r = blk * 2 + hh
            src = jnp.where(par == hh, qb, qr)
            qs_ref[r * TQ:(r + 1) * TQ, :] = (jnp.where(in_half, src, 0.0) * (SCALE * LOG2E)).astype(BF16)

    gs = jax.nn.sigmoid(gate_ref[...] + bg_ref[...])
    gates = [[jnp.sum(jnp.where(lane == N_BRANCH * (r_heads * g + r) + br, gs, 0.0), axis=-1, keepdims=True)
              for br in range(N_BRANCH)] for r in range(r_heads)]

    def tile_rows(kj):
        return pl.ds(pl.multiple_of(kj * TK, TK), TK)

    ones = jnp.ones((TK, LANES), BF16)

    def scores(q, key_tiles, tile_types):
        keys = jnp.concatenate(key_tiles, axis=0)
        s = jnp.concatenate([_dot_nt(q[c:c + ROW_SPLIT], keys) for c in range(0, rows, ROW_SPLIT)], axis=0)
        out = []
        for t, kind in enumerate(tile_types):
            st = s[:, t * TK:(t + 1) * TK]
            if kind is not None:
                st = (st.reshape(r_heads, TQ, TK) + tiles_ref[kind]).reshape(rows, TK)
            out.append(st)
        return out

    def weighted(ps, v_ref, kjs):
        p = jnp.concatenate([x.astype(BF16) for x in ps], axis=1)
        v = jnp.concatenate([jnp.concatenate([v_ref[tile_rows(kj), :], ones], axis=1) for kj in kjs], axis=0)
        return jnp.concatenate([_dot(p[c:c + ROW_SPLIT], v) for c in range(0, rows, ROW_SPLIT)], axis=0)

    def row_max(ss):
        return jnp.max(functools.reduce(jnp.maximum, ss), axis=-1, keepdims=True)

    def normalised(acc):
        return acc[:, 0:LANES] * (1.0 / acc[:, LANES:2 * LANES])

    kj1 = jnp.maximum(qi - 1, 0)
    near_type = jnp.where(qi >= 1, TILE_NEAR, TILE_NONE)

    kj2 = jnp.maximum(qi - 2, 0)
    last_type = jnp.where(qi >= 2, TILE_WIN_LAST, TILE_NONE)
    sw = scores(qs_ref[...], [kw_ref[tile_rows(kj), :] for kj in (qi, kj1, kj2)],
                (TILE_DIAG, near_type, last_type))
    mw = row_max(sw)
    o_win = normalised(weighted([jnp.exp2(s - mw) for s in sw], vw_ref, (qi, kj1, kj2)))

    trow = qi * TQ + lax.broadcasted_iota(jnp.int32, (TQ, 1), 0)
    has_any = jnp.where(trow >= CMP_BLOCK - 1, 1.0, 0.0)
    kc = kc_ref[...]
    vc = vc_ref[...]
    psum = jnp.zeros((TQ, N_CMP_PAD), F32)
    o_cmp = []
    for r in range(r_heads):
        s = _dot_nt(qs_ref[r * TQ:(r + 1) * TQ, :], kc) + cmpb_ref[r]
        e = jnp.exp2(s - jnp.max(s, axis=-1, keepdims=True))
        p = e * (has_any / jnp.sum(e, axis=-1, keepdims=True))
        psum = psum + p
        o_cmp.append(_dot(p.astype(BF16), vc))

    p_hi = psum.astype(BF16)
    p_lo = (psum - p_hi.astype(F32)).astype(BF16)
    ov = ov_ref[...]
    imp = _dot_nt(ov, p_hi) + _dot_nt(ov, p_lo)
    jrow = lax.broadcasted_iota(jnp.int32, (n_sb, TQ), 0)
    tq = qi * TQ + lax.broadcasted_iota(jnp.int32, (n_sb, TQ), 1)
    cur = tq // SEL_BLOCK
    valid = jrow <= cur
    forced = jnp.where(jrow == 0, 1.0, jnp.where(jrow == cur, 1.0, jnp.where(jrow == cur - 1, 1.0, 0.0)))
    score = jnp.where(valid, imp + FORCE_BONUS * forced, -1.0)
    rank = jnp.zeros((n_sb, TQ), F32)
    for jp in range(n_sb):
        other = jnp.broadcast_to(score[jp:jp + 1, :], (n_sb, TQ))
        before = jnp.where(jrow > jp, 1.0, 0.0)
        rank = rank + jnp.where(other > score, 1.0, jnp.where(other == score, before, 0.0))
    chosen = jnp.where(valid, jnp.where(rank < float(min(N_SELECT, n_sb)), 0.0, NEG_INF), NEG_INF)

    srow = lax.broadcasted_iota(jnp.int32, (LANES - n_sb, TQ), 0) + n_sb
    side_t = jnp.concatenate([chosen, jnp.where(srow == SIDE_OFF, NEG_INF, 0.0)], axis=0)
    side = side_t.T
    for r in range(r_heads):
        far = jnp.full((TQ, LANES), rb_ref[N_BUCKETS - 1, r_heads * g + r] * LOG2E, F32)
        far_hi = far.astype(BF16).astype(F32)
        side_r = jnp.where(lane == SIDE_FAR_HI, far_hi, jnp.where(lane == SIDE_FAR_LO, far - far_hi, side))
        side_ref[r * TQ:(r + 1) * TQ, :] = side_r.astype(BF16)

    def key_side(kj, side_kind):
        return jnp.concatenate([ks_ref[tile_rows(kj), :], kside_ref[side_kind, tile_rows(kj), :]], axis=1)

    def q_side():
        return jnp.concatenate([qs_ref[...], side_ref[...]], axis=1)

    sn = scores(q_side(), [key_side(qi, KSIDE_NEAR), key_side(kj1, KSIDE_NEAR)], (TILE_DIAG, near_type))
    m0 = row_max(sn)
    m_ref[...] = jnp.broadcast_to(m0, (rows, LANES))
    acc_ref[...] = weighted([jnp.exp2(s - m0) for s in sn], vs_ref, (qi, kj1))

    def far_pair(idx):
        ka = qi - 2 - 2 * idx
        kb_raw = ka - 1
        kb = jnp.maximum(kb_raw, 0)
        sf = scores(q_side(), [key_side(ka, KSIDE_FAR),
                               key_side(kb, jnp.where(kb_raw >= 0, KSIDE_FAR, KSIDE_OFF))], (None, None))
        m_prev = m_ref[...]
        m_new = jnp.maximum(m_prev, row_max(sf))
        alpha = jnp.exp2(m_prev - m_new)
        m_wide = jnp.tile(m_new, (1, TK // LANES))
        acc_ref[...] = (jnp.tile(alpha, (1, 2)) * acc_ref[...]
                        + weighted([jnp.exp2(s - m_wide) for s in sf], vs_ref, (ka, kb)))
        m_ref[...] = m_new

    def placed(lo, hi):
        lo = jnp.where(par == 0, lo, pltpu.roll(lo, HEAD_DIM, 1))
        hi = jnp.where(par == 1, hi, pltpu.roll(hi, HEAD_DIM, 1))
        return jnp.where(lane < HEAD_DIM, lo, hi)

    for blk in range(2):
        part = [gates[r][0] * o_cmp[r] + gates[r][2] * o_win[r * TQ:(r + 1) * TQ, :] for r in (2 * blk, 2 * blk + 1)]
        out_ref[:, blk * LANES:(blk + 1) * LANES] = placed(*part)

    def finish():
        o_slc = normalised(acc_ref[...])
        for blk in range(2):
            part = [gates[r][1] * o_slc[r * TQ:(r + 1) * TQ, :] for r in (2 * blk, 2 * blk + 1)]
            out_ref[:, blk * LANES:(blk + 1) * LANES] += placed(*part)

    return far_pair, finish


def _nsa_attn(rel_bias, att, gate, kvc, bgate, tiles, cmpb, batch, seq_len):
    n = att.shape[0]
    nq = seq_len // TQ
    n_sb = seq_len // SEL_BLOCK
    ng = N_KV_GROUPS
    r = HEADS_PER_GROUP
    ov = _overlap_constants(seq_len)
    kside = _key_side_constants(seq_len)
    qw = r * HEAD_DIM
    kcol = ATTN_WIDTH + 2 * KV_WIDTH
    ns = ATTN_SLOTS if batch % ATTN_SLOTS == 0 else 1

    def slot_specs(s):
        seq = lambda bb: bb * ns + s
        kvb = lambda off: (lambda g, qi, bb: (seq(bb), (kcol + off * KV_WIDTH) // LANES + g // 2))
        return [
            pl.BlockSpec((TQ, qw), lambda g, qi, bb: (seq(bb) * nq + qi, g)),
            pl.BlockSpec((None, None, N_CMP_PAD, LANES), lambda g, qi, bb: (0, seq(bb) * ng + g, 0, 0)),
            pl.BlockSpec((None, None, N_CMP_PAD, LANES), lambda g, qi, bb: (1, seq(bb) * ng + g, 0, 0)),
            pl.BlockSpec((seq_len, LANES), kvb(0)),
            pl.BlockSpec((seq_len, LANES), kvb(1)),
            pl.BlockSpec((seq_len, LANES), kvb(2)),
            pl.BlockSpec((seq_len, LANES), kvb(3)),
            pl.BlockSpec((TQ, GATE_PAD), lambda g, qi, bb: (seq(bb) * nq + qi, 0)),
        ]

    shared_specs = [
        pl.BlockSpec((1, GATE_PAD), lambda g, qi, bb: (0, 0)),
        pl.BlockSpec((None, N_TILE_TYPES, r, TQ, TK), lambda g, qi, bb: (g, 0, 0, 0, 0)),
        pl.BlockSpec((None, r, TQ, N_CMP_PAD), lambda g, qi, bb: (g, 0, qi, 0)),
        pl.BlockSpec((n_sb, N_CMP_PAD), lambda g, qi, bb: (0, 0)),
        pl.BlockSpec((3, seq_len, LANES), lambda g, qi, bb: (0, 0, 0)),
    ]
    slot_scratch = [
        pltpu.VMEM((r * TQ, LANES), BF16),
        pltpu.VMEM((r * TQ, LANES), BF16),
        pltpu.VMEM((r * TQ, LANES), F32),
        pltpu.VMEM((r * TQ, 2 * LANES), F32),
    ]
    outs = pl.pallas_call(
        functools.partial(_nsa_kernel, n_sb, ns),
        grid=(ng, nq, batch // ns),
        in_specs=([pl.BlockSpec(memory_space=pltpu.SMEM)]
                  + [spec for s in range(ns) for spec in slot_specs(s)] + shared_specs),
        out_specs=pl.BlockSpec((None, ns, TQ, qw), lambda g, qi, bb: (bb, 0, qi, g)),
        out_shape=jax.ShapeDtypeStruct((batch // ns, ns, seq_len, ATTN_WIDTH), F32),
        scratch_shapes=slot_scratch * ns,
        compiler_params=pltpu.CompilerParams(dimension_semantics=("arbitrary", "arbitrary", "arbitrary"),
                                             vmem_limit_bytes=VMEM_LIMIT),
        name="nsa_attn",
    )(rel_bias, *([att, kvc, kvc, att, att, att, att, gate] * ns), bgate, tiles, cmpb,
      jnp.asarray(ov, BF16), jnp.asarray(kside, BF16))
    return outs.reshape(n, ATTN_WIDTH)


def kernel(x, mix_norm_g, w_in, b_gate, cmp_pe_k, cmp_pe_v, cmp_k_w1, cmp_k_w2, cmp_v_w1, cmp_v_w2, rel_bias, rnn_conv_w, rnn_conv_b, rg_a_w, rg_a_b, rg_x_w, rg_x_b, rg_lambda, attn_out_g, rnn_out_g, w_out, ffn_norm_g, w_ffn_gate, w_ffn_up, ffn_conv_w, ffn_conv_b, w_ffn_down, final_norm_g):
    batch, seq_len, d_model = x.shape
    assert d_model == D_MODEL and w_in.shape[0] == 1, "single-layer model of the stated width"
    assert seq_len // CMP_STRIDE == N_CMP_PAD and seq_len % TM_FFN == 0
    n = batch * seq_len
    ng = N_KV_GROUPS
    x2 = x.reshape(n, D_MODEL)

    w0 = w_in[0].astype(BF16)
    gate0 = ATT_COLS
    rx0 = gate0 + N_BRANCH * N_Q_HEADS
    w_rnn = jnp.concatenate(
        [w0[:, rx0:rx0 + 2 * RNN_WIDTH], w0[:, gate0:rx0],
         jnp.zeros((D_MODEL, GATE_PAD - N_BRANCH * N_Q_HEADS), BF16)], axis=1)
    att, gate, chunks, nr = _in_proj(
        x2, mix_norm_g[0][None], w0, w_rnn, rnn_conv_w[0], rnn_conv_b[0][None],
        _block_diag(rg_a_w[0]).astype(BF16), rg_a_b[0][None], _block_diag(rg_x_w[0]).astype(BF16), rg_x_b[0][None],
        rg_lambda[0][None], rnn_out_g[0][None], seq_len)

    pe = jnp.stack([_pair_pe(cmp_pe_k[0]), _pair_pe(cmp_pe_v[0])]).astype(BF16)
    w1p = jnp.stack([_pair_weights(cmp_k_w1[0]), _pair_weights(cmp_v_w1[0])]).astype(BF16)
    w2 = jnp.stack([jnp.tile(cmp_k_w2[0], (1, 2)), jnp.tile(cmp_v_w2[0], (1, 2))]).astype(BF16)
    kvc = _cmp_tokens(chunks, pe, w1p, w2, batch).reshape(2, batch * ng, N_CMP_PAD, LANES)

    tiles, cmpb = _bias_tiles(rel_bias, seq_len)
    bgate = jnp.pad(b_gate[0], (0, GATE_PAD - b_gate.shape[1]))[None]
    o_attn = _nsa_attn(rel_bias, att, gate, kvc, bgate, tiles, cmpb, batch, seq_len)

    h, y = _out_proj(o_attn, nr, x2, attn_out_g[0][None], ffn_norm_g[0][None], w_out[0].astype(BF16))
    out = _conv_ffn(y, h, w_ffn_gate[0].astype(BF16), w_ffn_up[0].astype(BF16), w_ffn_down[0].astype(BF16),
                    ffn_conv_w[0], ffn_conv_b[0][None], final_norm_g[None], seq_len)
    return out.reshape(batch, seq_len, D_MODEL)
```

```python
import functools
import math

import jax
import jax.numpy as jnp
import numpy as np
from jax import lax
from jax.experimental import pallas as pl
from jax.experimental.pallas import tpu as pltpu

F32 = jnp.float32
BF16 = jnp.bfloat16

D_MODEL = 2048
N_Q_HEADS = 16
N_KV_GROUPS = 4
HEADS_PER_GROUP = 4
HEAD_DIM = 64
ATTN_WIDTH = 1024
KV_WIDTH = 256
SCALE = HEAD_DIM ** -0.5
CMP_BLOCK = 32
CMP_STRIDE = 16
CMP_HIDDEN = 256
SEL_BLOCK = 64
N_SELECT = 16
WINDOW = 512
N_BRANCH = 3
RNN_WIDTH = 1024
RNN_BLOCKS = 16
RNN_CONV_WIDTH = 4
RG_LRU_C = 8.0
N_BUCKETS = 32
MAX_DISTANCE = 128
D_FF = 5632
NORM_EPS = 1e-6
NEG_INF = -1e30
FORCE_BONUS = 1e3
LOG2E = 1.0 / math.log(2.0)
GELU_C0 = math.sqrt(2.0 / math.pi)
GELU_C1 = GELU_C0 * 0.044715

LANES = 128
SUBLANES = 8
VMEM_LIMIT = 56 * 1024 * 1024

TM_PROJ = 512
COL_CHUNK = 512
TM_FFN = 1024
TF_FFN = 512
FFN_ROW_BLOCKS = 2
H_PIECE = 256
TS_RNN = 512
CB_RNN = 256
RNN_ROWS = 256
PROJ_CHUNK = 256
TQ = 256
TK = 256
ATTN_SLOTS = 2
ROW_SPLIT = 512
N_CMP_PAD = 128
GATE_PAD = 128

ATT_COLS = ATTN_WIDTH + 6 * KV_WIDTH
RNN_COLS = 2 * RNN_WIDTH + GATE_PAD
CMP_COL0 = ATTN_WIDTH
N_CHUNK_SLABS = 2 * KV_WIDTH // LANES
CHUNK2_W = CMP_STRIDE * LANES


def _rms(x, g):
    return x * lax.rsqrt(jnp.mean(x * x, axis=-1, keepdims=True) + NORM_EPS) * g


def _gelu(x):
    return jax.nn.gelu(x, approximate=True)


def _dot(a, b):
    return jnp.dot(a, b, preferred_element_type=F32)


def _dot_nt(a, b):
    return lax.dot_general(a, b, (((1,), (1,)), ((), ())), preferred_element_type=F32)


def _dot_tn(a, b):
    return lax.dot_general(a, b, (((0,), (0,)), ((), ())), preferred_element_type=F32)


def _out_proj_kernel(oa_ref, nr_ref, x_ref, ga_ref, gf_ref, w_ref, h_ref, y_ref):
    na = _rms(oa_ref[...], ga_ref[...]).astype(BF16)
    nr = nr_ref[...]
    for c0 in range(0, D_MODEL, COL_CHUNK):
        sl = slice(c0, c0 + COL_CHUNK)
        h_ref[:, sl] = (x_ref[:, sl] + _dot(na, w_ref[0:ATTN_WIDTH, sl])
                        + _dot(nr, w_ref[ATTN_WIDTH:D_MODEL, sl]))
    y_ref[...] = _rms(h_ref[...], gf_ref[...]).astype(BF16)


def _out_proj(o_attn, nr, x2, ga, gf, w):
    n = x2.shape[0]
    return pl.pallas_call(
        _out_proj_kernel,
        grid=(n // TM_PROJ,),
        in_specs=[
            pl.BlockSpec((TM_PROJ, ATTN_WIDTH), lambda i: (i, 0)),
            pl.BlockSpec((TM_PROJ, RNN_WIDTH), lambda i: (i, 0)),
            pl.BlockSpec((TM_PROJ, D_MODEL), lambda i: (i, 0)),
            pl.BlockSpec((1, ATTN_WIDTH), lambda i: (0, 0)),
            pl.BlockSpec((1, D_MODEL), lambda i: (0, 0)),
            pl.BlockSpec((D_MODEL, D_MODEL), lambda i: (0, 0), pipeline_mode=pl.Buffered(1)),
        ],
        out_specs=[
            pl.BlockSpec((TM_PROJ, D_MODEL), lambda i: (i, 0)),
            pl.BlockSpec((TM_PROJ, D_MODEL), lambda i: (i, 0)),
        ],
        out_shape=[
            jax.ShapeDtypeStruct((n, D_MODEL), F32),
            jax.ShapeDtypeStruct((n, D_MODEL), BF16),
        ],
        compiler_params=pltpu.CompilerParams(dimension_semantics=("parallel",), vmem_limit_bytes=VMEM_LIMIT),
        name="out_proj",
    )(o_attn, nr, x2, ga, gf, w)


def _ffn_kernel(tiles_per_seq, y_ref, h_ref, wg_ref, wu_ref, wd_ref, cw_ref, cb_ref, gf_ref, out_ref, carry_ref):
    i = pl.program_id(0)
    j = pl.program_id(1)
    nj = pl.num_programs(1)

    @pl.when((i % tiles_per_seq) == 0)
    def _():
        carry_ref[j] = jnp.zeros((SUBLANES, TF_FFN), F32)

    @pl.when(j == 0)
    def _():
        out_ref[...] = jnp.zeros(out_ref.shape, F32)

    @pl.when(j < D_MODEL // H_PIECE)
    def _():
        cols = pl.ds(pl.multiple_of(j * H_PIECE, H_PIECE), H_PIECE)
        out_ref[:, cols] += h_ref[...]

    row = lax.broadcasted_iota(jnp.int32, (SUBLANES, TF_FFN), 0)
    cw = cw_ref[...]
    cb = cb_ref[...]
    prev = carry_ref[j]
    rows_blk = TM_FFN // FFN_ROW_BLOCKS
    for blk in range(FFN_ROW_BLOCKS):
        sl = slice(blk * rows_blk, (blk + 1) * rows_blk)
        y = y_ref[sl, :]
        gt = _dot(y, wg_ref[...])
        p1 = prev[SUBLANES - 1:SUBLANES, :]
        p2 = prev[SUBLANES - 2:SUBLANES - 1, :]
        r1 = pltpu.roll(gt, 1, 0)
        r2 = pltpu.roll(gt, 2, 0)
        h1 = jnp.where(row == 0, p1, r1[0:SUBLANES])
        h2 = jnp.where(row == 0, p2, jnp.where(row == 1, p1, r2[0:SUBLANES]))
        s1 = jnp.concatenate([h1, r1[SUBLANES:]], axis=0)
        s2 = jnp.concatenate([h2, r2[SUBLANES:]], axis=0)
        conv = cw[2:3, :] * gt + cw[1:2, :] * s1 + cw[0:1, :] * s2 + cb
        inner = conv * (GELU_C0 + GELU_C1 * (conv * conv))
        hx = 0.5 * conv
        u = hx + hx * jnp.tanh(inner)
        z = (u * _dot(y, wu_ref[...])).astype(BF16)
        out_ref[sl, :] += _dot(z, wd_ref[...])
        prev = gt[rows_blk - SUBLANES:rows_blk, :]
    carry_ref[j] = prev

    @pl.when(j == nj - 1)
    def _():
        out_ref[...] = _rms(out_ref[...], gf_ref[...])


def _conv_ffn(y, h, wg, wu, wd, cw, cb, gf, seq_len):
    n = y.shape[0]
    nj = D_FF // TF_FFN
    return pl.pallas_call(
        functools.partial(_ffn_kernel, seq_len // TM_FFN),
        grid=(n // TM_FFN, nj),
        in_specs=[
            pl.BlockSpec((TM_FFN, D_MODEL), lambda i, j: (i, 0)),
            pl.BlockSpec((TM_FFN, H_PIECE), lambda i, j: (i, jnp.minimum(j, D_MODEL // H_PIECE - 1))),
            pl.BlockSpec((D_MODEL, TF_FFN), lambda i, j: (0, j)),
            pl.BlockSpec((D_MODEL, TF_FFN), lambda i, j: (0, j)),
            pl.BlockSpec((TF_FFN, D_MODEL), lambda i, j: (j, 0)),
            pl.BlockSpec((3, TF_FFN), lambda i, j: (0, j)),
            pl.BlockSpec((1, TF_FFN), lambda i, j: (0, j)),
            pl.BlockSpec((1, D_MODEL), lambda i, j: (0, 0)),
        ],
        out_specs=pl.BlockSpec((TM_FFN, D_MODEL), lambda i, j: (i, 0)),
        out_shape=jax.ShapeDtypeStruct((n, D_MODEL), F32),
        scratch_shapes=[pltpu.VMEM((nj, SUBLANES, TF_FFN), F32)],
        compiler_params=pltpu.CompilerParams(dimension_semantics=("arbitrary", "arbitrary"),
                                             vmem_limit_bytes=VMEM_LIMIT),
        name="conv_ffn",
    )(y, h, wg, wu, wd, cw, cb, gf)


def _rglru_pieces(rx_ref, ry_ref, cw_ref, cb_ref, wa_ref, ba_ref, wx_ref, bx_ref, lam_ref, g_ref,
                  out_ref, tail_ref, hc_ref, a_ref, u_ref, xbuf_ref, ygate_ref):
    rmod = lax.broadcasted_iota(jnp.int32, (RNN_ROWS // SUBLANES, SUBLANES, CB_RNN), 1)
    lam = lam_ref[...]
    nlam = -lam
    softplus = jnp.maximum(nlam, 0.0) + jnp.log1p(jnp.exp(-jnp.abs(nlam)))
    log_a_unit = -RG_LRU_C * softplus

    def stage():
        xbuf_ref[0:SUBLANES, :] = tail_ref[...]
        xbuf_ref[SUBLANES:SUBLANES + TS_RNN, :] = rx_ref[...]
        tail_ref[...] = rx_ref[TS_RNN - SUBLANES:TS_RNN, :]

    def stage_gate(lo, hi):
        ygate_ref[lo:hi, :] = _gelu(ry_ref[lo:hi, :])

    def channel_block(cb, lo):
        sl = slice(cb * CB_RNN, (cb + 1) * CB_RNN)
        hi = lo + RNN_ROWS
        cw = cw_ref[:, sl]
        xr = cb_ref[:, sl]
        for d in (0, 1, 2, 3):
            xr = xr + cw[3 - d:4 - d, :] * xbuf_ref[SUBLANES - d + lo:SUBLANES - d + hi, sl]
        xb = xr.astype(BF16)
        r = jax.nn.sigmoid(_dot(xb, wa_ref[cb]) + ba_ref[:, sl])
        gi = jax.nn.sigmoid(_dot(xb, wx_ref[cb]) + bx_ref[:, sl])
        log_a = log_a_unit[:, sl] * r
        a = jnp.exp(log_a)
        th = jnp.tanh(log_a)
        w = -2.0 * th / (1.0 - th)
        u = jnp.where(w > 0.0, w * lax.rsqrt(w), 0.0) * (gi * xr)
        a = a.reshape(RNN_ROWS // SUBLANES, SUBLANES, CB_RNN)
        u = u.reshape(RNN_ROWS // SUBLANES, SUBLANES, CB_RNN)
        for d in (1, 2, 4):
            keep = rmod >= d
            a_s = pltpu.roll(a, d, 1)
            u_s = pltpu.roll(u, d, 1)
            u = jnp.where(keep, a * u_s + u, u)
            a = jnp.where(keep, a * a_s, a)
        a_ref[lo:hi, sl] = a.reshape(RNN_ROWS, CB_RNN)
        u_ref[lo:hi, sl] = u.reshape(RNN_ROWS, CB_RNN)

    def carry(lo, hi):
        hprev = hc_ref[...]
        for r0 in range(lo, hi, SUBLANES):
            h8 = u_ref[r0:r0 + SUBLANES, :] + a_ref[r0:r0 + SUBLANES, :] * hprev
            u_ref[r0:r0 + SUBLANES, :] = h8
            hprev = jnp.broadcast_to(h8[SUBLANES - 1:SUBLANES, :], (SUBLANES, RNN_WIDTH))
        hc_ref[...] = hprev

    def finish(lo, hi):
        out_ref[lo:hi, :] = _rms(u_ref[lo:hi, :] * ygate_ref[lo:hi, :], g_ref[...]).astype(BF16)

    quarter, half = TS_RNN // 4, TS_RNN // 2
    readers = [stage] + [functools.partial(stage_gate, k * half, (k + 1) * half) for k in range(2)]
    others = ([functools.partial(channel_block, cb, lo)
               for cb in range(RNN_WIDTH // CB_RNN) for lo in range(0, TS_RNN, RNN_ROWS)]
              + [functools.partial(carry, k * quarter, (k + 1) * quarter) for k in range(4)]
              + [functools.partial(finish, k * half, (k + 1) * half) for k in range(2)])
    return readers, others


def _in_proj_kernel(tiles_per_seq, x_ref, g_ref, w_ref, wr_ref,
                    cw_ref, cb_ref, wa_ref, ba_ref, wx_ref, bx_ref, lam_ref, gr_ref,
                    att_ref, gate_ref, chunk_ref, nr_ref,
                    stage_ref, buf_ref, tail_ref, hc_ref, a_ref, u_ref, xbuf_ref, ygate_ref):
    i = pl.program_id(0)

    @pl.when(i == 0)
    def _():
        buf_ref[...] = jnp.zeros(buf_ref.shape, F32)

    @pl.when(jnp.maximum(i - 1, 0) % tiles_per_seq == 0)
    def _():
        tail_ref[...] = jnp.zeros(tail_ref.shape, F32)
        hc_ref[...] = jnp.zeros(hc_ref.shape, F32)

    buffer_readers, other_pieces = _rglru_pieces(
        buf_ref.at[:, 0:RNN_WIDTH], buf_ref.at[:, RNN_WIDTH:2 * RNN_WIDTH],
        cw_ref, cb_ref, wa_ref, ba_ref, wx_ref, bx_ref, lam_ref, gr_ref,
        nr_ref, tail_ref, hc_ref, a_ref, u_ref, xbuf_ref, ygate_ref)
    xn = _rms(x_ref[...], g_ref[...]).astype(BF16)

    def att_chunk(c0):
        res = _dot(xn, w_ref[:, c0:c0 + PROJ_CHUNK])
        att_ref[:, c0:c0 + PROJ_CHUNK] = res.astype(BF16)
        if CMP_COL0 <= c0 < CMP_COL0 + N_CHUNK_SLABS * LANES:
            q0 = (c0 - CMP_COL0) // LANES
            slabs = range(q0, q0 + PROJ_CHUNK // LANES)
            for q in slabs:
                stage_ref[q] = res[:, (q - q0) * LANES:(q - q0 + 1) * LANES]
            for l in range(CMP_STRIDE):
                for q in slabs:
                    tok = stage_ref[q, pl.ds(l, TM_PROJ // CMP_STRIDE, stride=CMP_STRIDE), :]
                    chunk_ref[q, :, l * LANES:(l + 1) * LANES] = tok.astype(BF16)

    def rnn_chunk(c0):
        buf_ref[:, c0:c0 + PROJ_CHUNK] = _dot(xn, wr_ref[:, c0:c0 + PROJ_CHUNK])

    def gate_chunk():
        gate_ref[...] = _dot(xn, wr_ref[:, 2 * RNN_WIDTH:RNN_COLS])

    def interleave(matmuls, others):
        done = 0
        for k, piece in enumerate(matmuls):
            piece()
            upto = (k + 1) * len(others) // len(matmuls)
            for other in others[done:upto]:
                other()
            done = upto

    keep_buffer = [functools.partial(att_chunk, c0) for c0 in range(0, ATT_COLS, PROJ_CHUNK)] + [gate_chunk]
    refill = [functools.partial(rnn_chunk, c0) for c0 in range(0, 2 * RNN_WIDTH, PROJ_CHUNK)]
    pieces = buffer_readers + other_pieces
    assert len(keep_buffer) * len(pieces) // (len(keep_buffer) + len(refill)) >= len(buffer_readers)
    interleave(keep_buffer + refill, pieces)


def _in_proj(x2, g, w, w_rnn, cw, cb, wa, ba, wx, bx, lam, gr, seq_len):
    n = x2.shape[0]
    nt = n // TM_PROJ
    assert CMP_COL0 % PROJ_CHUNK == 0 and (N_CHUNK_SLABS * LANES) % PROJ_CHUNK == 0 and TS_RNN == TM_PROJ
    nb = RNN_WIDTH // CB_RNN
    cur = lambda i: jnp.minimum(i, nt - 1)
    vec = pl.BlockSpec((1, RNN_WIDTH), lambda i: (0, 0))
    wspec = pl.BlockSpec((nb, CB_RNN, CB_RNN), lambda i: (0, 0, 0))
    return pl.pallas_call(
        functools.partial(_in_proj_kernel, seq_len // TM_PROJ),
        grid=(nt + 1,),
        in_specs=[
            pl.BlockSpec((TM_PROJ, D_MODEL), lambda i: (cur(i), 0)),
            pl.BlockSpec((1, D_MODEL), lambda i: (0, 0)),
            pl.BlockSpec((D_MODEL, ATT_COLS), lambda i: (0, 0), pipeline_mode=pl.Buffered(1)),
            pl.BlockSpec((D_MODEL, RNN_COLS), lambda i: (0, 0), pipeline_mode=pl.Buffered(1)),
            pl.BlockSpec((RNN_CONV_WIDTH, RNN_WIDTH), lambda i: (0, 0)),
            vec, wspec, vec, wspec, vec, vec, vec,
        ],
        out_specs=[
            pl.BlockSpec((TM_PROJ, ATT_COLS), lambda i: (cur(i), 0)),
            pl.BlockSpec((TM_PROJ, GATE_PAD), lambda i: (cur(i), 0)),
            pl.BlockSpec((N_CHUNK_SLABS, TM_PROJ // CMP_STRIDE, CHUNK2_W), lambda i: (0, cur(i), 0)),
            pl.BlockSpec((TM_PROJ, RNN_WIDTH), lambda i: (jnp.maximum(i - 1, 0), 0)),
        ],
        out_shape=[
            jax.ShapeDtypeStruct((n, ATT_COLS), BF16),
            jax.ShapeDtypeStruct((n, GATE_PAD), F32),
            jax.ShapeDtypeStruct((N_CHUNK_SLABS, n // CMP_STRIDE, CHUNK2_W), BF16),
            jax.ShapeDtypeStruct((n, RNN_WIDTH), BF16),
        ],
        scratch_shapes=[
            pltpu.VMEM((N_CHUNK_SLABS, TM_PROJ, LANES), F32),
            pltpu.VMEM((TM_PROJ, 2 * RNN_WIDTH), F32),
            pltpu.VMEM((SUBLANES, RNN_WIDTH), F32),
            pltpu.VMEM((SUBLANES, RNN_WIDTH), F32),
            pltpu.VMEM((TS_RNN, RNN_WIDTH), F32),
            pltpu.VMEM((TS_RNN, RNN_WIDTH), F32),
            pltpu.VMEM((SUBLANES + TS_RNN, RNN_WIDTH), F32),
            pltpu.VMEM((TS_RNN, RNN_WIDTH), F32),
        ],
        compiler_params=pltpu.CompilerParams(dimension_semantics=("arbitrary",), vmem_limit_bytes=VMEM_LIMIT),
        name="in_proj",
    )(x2, g, w, w_rnn, cw, cb, wa, ba, wx, bx, lam, gr)


def _block_diag(w):
    per = CB_RNN // (RNN_WIDTH // RNN_BLOCKS)
    bd = w.shape[-1]
    w4 = w.reshape(RNN_BLOCKS // per, per, bd, bd)
    eye = jnp.eye(per, dtype=w.dtype)
    return jnp.einsum('cpij,pq->cpiqj', w4, eye).reshape(RNN_BLOCKS // per, per * bd, per * bd)


CMP_BATCHES = 8


def _cmp_tokens_kernel(x_ref, pe_ref, w1_ref, w2_ref, out_ref):
    x = x_ref[...]
    rows = x.shape[0]
    ya = _dot(x, w1_ref[0])
    yb = _dot(x, w1_ref[1])
    pterm = (_dot(jnp.broadcast_to(pe_ref[0], (SUBLANES, CHUNK2_W)), w1_ref[0])
             + _dot(jnp.broadcast_to(pe_ref[1], (SUBLANES, CHUNK2_W)), w1_ref[1]))[0:1, :]
    act = _gelu(ya + pltpu.roll(yb, rows - 1, 0) + pterm).astype(BF16)
    for e in range(2):
        tok = _dot(act[:, e * CMP_HIDDEN:(e + 1) * CMP_HIDDEN], w2_ref[...]).astype(BF16)
        out_ref[:, e] = tok.reshape(rows // N_CMP_PAD, N_CMP_PAD, LANES)


def _cmp_tokens(chunks, pe, w1p, w2, batch):
    nb = min(CMP_BATCHES, batch)
    tr = nb * N_CMP_PAD
    return pl.pallas_call(
        _cmp_tokens_kernel,
        grid=(N_CHUNK_SLABS, batch // nb),
        in_specs=[
            pl.BlockSpec((None, tr, CHUNK2_W), lambda s, i: (s, i, 0)),
            pl.BlockSpec((None, 2, 1, CHUNK2_W), lambda s, i: (s // 2, 0, 0, 0)),
            pl.BlockSpec((None, 2, CHUNK2_W, 2 * CMP_HIDDEN), lambda s, i: (s // 2, 0, 0, 0)),
            pl.BlockSpec((None, CMP_HIDDEN, LANES), lambda s, i: (s // 2, 0, 0)),
        ],
        out_specs=pl.BlockSpec((None, nb, None, 2, N_CMP_PAD, LANES), lambda s, i: (s // 2, i, s % 2, 0, 0, 0)),
        out_shape=jax.ShapeDtypeStruct((2, batch, 2, 2, N_CMP_PAD, LANES), BF16),
        compiler_params=pltpu.CompilerParams(dimension_semantics=("parallel", "parallel")),
        name="cmp_tokens",
    )(chunks, pe, w1p, w2)


def _pair_weights(w1):
    half = CMP_STRIDE * HEAD_DIM
    w = w1.reshape(2, CMP_STRIDE, HEAD_DIM, CMP_HIDDEN)
    eye = jnp.eye(2, dtype=w1.dtype)
    return jnp.einsum('hldj,ef->hledfj', w, eye).reshape(2, 2 * half, 2 * CMP_HIDDEN)


def _pair_pe(pe):
    p = pe.reshape(2, CMP_STRIDE, 1, HEAD_DIM)
    return jnp.broadcast_to(p, (2, CMP_STRIDE, 2, HEAD_DIM)).reshape(2, 1, CHUNK2_W)


N_TILE_TYPES = 4
TILE_DIAG, TILE_NEAR, TILE_WIN_LAST, TILE_NONE = 0, 1, 2, 3


def _t5_bucket_np(dist):
    n = np.maximum(dist, 0)
    max_exact = N_BUCKETS // 2
    nf = np.maximum(n, 1).astype(np.float64)
    large = max_exact + (np.log(nf / max_exact) / math.log(MAX_DISTANCE / max_exact)
                         * (N_BUCKETS - max_exact)).astype(np.int32)
    large = np.minimum(large, N_BUCKETS - 1)
    return np.where(n < max_exact, n, large).astype(np.int32)


def _bucket_constants(seq_len):
    i = np.arange(TQ)[:, None]
    j = np.arange(TK)[None, :]
    d0 = i - j
    d1 = TK + i - j
    d2 = 2 * TK + i - j
    valid = [d0 >= 0, np.ones_like(d1, bool), d2 < WINDOW, np.zeros_like(d0, bool)]
    tiles = np.stack([np.where(v, _t5_bucket_np(d), -1) for d, v in zip([d0, d1, d2, d0], valid)])
    t = np.arange(seq_len)[:, None]
    c = np.arange(N_CMP_PAD)[None, :]
    dc = t - (c * CMP_STRIDE + CMP_BLOCK - 1)
    cmp = np.where(dc >= 0, _t5_bucket_np(dc), -1)
    return tiles.astype(np.int32), cmp.astype(np.int32)


def _bias_kernel(tile_buckets, rb_ref, tb_ref, cbk_ref, tiles_ref, cmpb_ref):
    h = pl.program_id(0)

    def lookup(bkt, buckets):
        acc = jnp.full(bkt.shape, NEG_INF, F32)
        for b in buckets:
            acc = jnp.where(bkt == b, rb_ref[b, h] * LOG2E, acc)
        return acc

    for t in range(N_TILE_TYPES):
        tiles_ref[t] = lookup(tb_ref[t], tile_buckets[t])
    cmpb_ref[...] = lookup(cbk_ref[...], range(N_BUCKETS))


def _bias_tiles(rel_bias, seq_len):
    tb, cbk = _bucket_constants(seq_len)
    tile_buckets = tuple(tuple(int(b) for b in np.unique(t[t >= 0])) for t in tb)
    r = HEADS_PER_GROUP
    return pl.pallas_call(
        functools.partial(_bias_kernel, tile_buckets),
        grid=(N_Q_HEADS,),
        in_specs=[
            pl.BlockSpec(memory_space=pltpu.SMEM),
            pl.BlockSpec((N_TILE_TYPES, TQ, TK), lambda h: (0, 0, 0)),
            pl.BlockSpec((seq_len, N_CMP_PAD), lambda h: (0, 0)),
        ],
        out_specs=[
            pl.BlockSpec((None, N_TILE_TYPES, None, TQ, TK), lambda h: (h // r, 0, h % r, 0, 0)),
            pl.BlockSpec((None, None, seq_len, N_CMP_PAD), lambda h: (h // r, h % r, 0, 0)),
        ],
        out_shape=[
            jax.ShapeDtypeStruct((N_KV_GROUPS, N_TILE_TYPES, r, TQ, TK), F32),
            jax.ShapeDtypeStruct((N_KV_GROUPS, r, seq_len, N_CMP_PAD), F32),
        ],
        compiler_params=pltpu.CompilerParams(dimension_semantics=("parallel",)),
        name="bias_tiles",
    )(rel_bias, jnp.asarray(tb), jnp.asarray(cbk))


SIDE_FAR_HI, SIDE_FAR_LO, SIDE_OFF = 32, 33, 34
KSIDE_NEAR, KSIDE_FAR, KSIDE_OFF = 0, 1, 2


def _key_side_constants(seq_len):
    s = np.arange(seq_len)
    side = np.zeros((3, seq_len, LANES), np.float32)
    side[:, s, s // SEL_BLOCK] = 1.0
    side[KSIDE_FAR, :, SIDE_FAR_HI] = 1.0
    side[KSIDE_FAR, :, SIDE_FAR_LO] = 1.0
    side[KSIDE_OFF, :, SIDE_OFF] = 1.0
    return side


def _overlap_constants(seq_len):
    n_sb = seq_len // SEL_BLOCK
    c = np.arange(N_CMP_PAD)[None, :]
    j = np.arange(n_sb)[:, None]
    lo = np.maximum(c * CMP_STRIDE, j * SEL_BLOCK)
    hi = np.minimum(c * CMP_STRIDE + CMP_BLOCK, (j + 1) * SEL_BLOCK)
    ov = np.maximum(hi - lo, 0) / CMP_BLOCK
    ov[:, N_CMP_PAD - 1] = 0.0
    return ov.astype(np.float32)


def _nsa_kernel(n_sb, n_slots, rb_ref, *refs):
    per_slot_in = 8
    slot_in = [refs[s * per_slot_in:(s + 1) * per_slot_in] for s in range(n_slots)]
    shared = refs[n_slots * per_slot_in:n_slots * per_slot_in + 5]
    out_ref = refs[n_slots * per_slot_in + 5]
    scratch = refs[n_slots * per_slot_in + 6:]
    g = pl.program_id(0)
    qi = pl.program_id(1)
    slots = [_nsa_slot(n_sb, g, qi, rb_ref, *slot_in[s], *shared, out_ref.at[s], *scratch[4 * s:4 * s + 4])
             for s in range(n_slots)]

    def far_pairs(idx, carry):
        for far_pair, _ in slots:
            far_pair(idx)
        return carry

    lax.fori_loop(0, jnp.maximum(qi, 1) // 2, far_pairs, 0)
    for _, finish in slots:
        finish()


def _nsa_slot(n_sb, g, qi, rb_ref, q_ref, kc_ref, vc_ref, ks_ref, vs_ref, kw_ref, vw_ref, gate_ref, bg_ref,
              tiles_ref, cmpb_ref, ov_ref, kside_ref, out_ref,
              qs_ref, side_ref, m_ref, acc_ref):
    par = g % 2
    r_heads = HEADS_PER_GROUP
    rows = r_heads * TQ
    lane = lax.broadcasted_iota(jnp.int32, (TQ, LANES), 1)
    in_half = (lane // HEAD_DIM) == par

    for blk in range(2):
        qb = q_ref[:, blk * LANES:(blk + 1) * LANES].astype(F32)
        qr = pltpu.roll(qb, HEAD_DIM, 1)
        for hh in range(2):
            r = blk * 2 + hh
            src = jnp.where(par == hh, qb, qr)
            qs_ref[r * TQ:(r + 1) * TQ, :] = (jnp.where(in_half, src, 0.0) * (SCALE * LOG2E)).astype(BF16)

    gs = jax.nn.sigmoid(gate_ref[...] + bg_ref[...])
    gates = [[jnp.sum(jnp.where(lane == N_BRANCH * (r_heads * g + r) + br, gs, 0.0), axis=-1, keepdims=True)
              for br in range(N_BRANCH)] for r in range(r_heads)]

    def tile_rows(kj):
        return pl.ds(pl.multiple_of(kj * TK, TK), TK)

    ones = jnp.ones((TK, LANES), BF16)

    def scores(q, key_tiles, tile_types):
        keys = jnp.concatenate(key_tiles, axis=0)
        s = jnp.concatenate([_dot_nt(q[c:c + ROW_SPLIT], keys) for c in range(0, rows, ROW_SPLIT)], axis=0)
        out = []
        for t, kind in enumerate(tile_types):
            st = s[:, t * TK:(t + 1) * TK]
            if kind is not None:
                st = (st.reshape(r_heads, TQ, TK) + tiles_ref[kind]).reshape(rows, TK)
            out.append(st)
        return out

    def weighted(ps, v_ref, kjs):
        p = jnp.concatenate([x.astype(BF16) for x in ps], axis=1)
        v = jnp.concatenate([jnp.concatenate([v_ref[tile_rows(kj), :], ones], axis=1) for kj in kjs], axis=0)
        return jnp.concatenate([_dot(p[c:c + ROW_SPLIT], v) for c in range(0, rows, ROW_SPLIT)], axis=0)

    def row_max(ss):
        return jnp.max(functools.reduce(jnp.maximum, ss), axis=-1, keepdims=True)

    def normalised(acc):
        return acc[:, 0:LANES] * (1.0 / acc[:, LANES:2 * LANES])

    kj1 = jnp.maximum(qi - 1, 0)
    near_type = jnp.where(qi >= 1, TILE_NEAR, TILE_NONE)

    kj2 = jnp.maximum(qi - 2, 0)
    last_type = jnp.where(qi >= 2, TILE_WIN_LAST, TILE_NONE)
    sw = scores(qs_ref[...], [kw_ref[tile_rows(kj), :] for kj in (qi, kj1, kj2)],
                (TILE_DIAG, near_type, last_type))
    mw = row_max(sw)
    o_win = normalised(weighted([jnp.exp2(s - mw) for s in sw], vw_ref, (qi, kj1, kj2)))

    trow = qi * TQ + lax.broadcasted_iota(jnp.int32, (TQ, 1), 0)
    has_any = jnp.where(trow >= CMP_BLOCK - 1, 1.0, 0.0)
    sc = _dot_nt(qs_ref[...], kc_ref[...]).reshape(r_heads, TQ, N_CMP_PAD) + cmpb_ref[...]
    ec = jnp.exp2(sc - jnp.max(sc, axis=-1, keepdims=True))
    pc = ec * (has_any / jnp.sum(ec, axis=-1, keepdims=True))
    psum = jnp.sum(pc, axis=0)
    oc = _dot(pc.reshape(rows, N_CMP_PAD).astype(BF16), vc_ref[...])
    o_cmp = [oc[r * TQ:(r + 1) * TQ, :] for r in range(r_heads)]

    p_hi = psum.astype(BF16)
    p_lo = (psum - p_hi.astype(F32)).astype(BF16)
    ov = ov_ref[...]
    imp = _dot_nt(ov, p_hi) + _dot_nt(ov, p_lo)
    jrow = lax.broadcasted_iota(jnp.int32, (n_sb, TQ), 0)
    tq = qi * TQ + lax.broadcasted_iota(jnp.int32, (n_sb, TQ), 1)
    cur = tq // SEL_BLOCK
    valid = jrow <= cur
    forced = jnp.where(jrow == 0, 1.0, jnp.where(jrow == cur, 1.0, jnp.where(jrow == cur - 1, 1.0, 0.0)))
    score = jnp.where(valid, imp + FORCE_BONUS * forced, -1.0)
    rank = jnp.zeros((n_sb, TQ), F32)
    for jp in range(n_sb):
        other = jnp.broadcast_to(score[jp:jp + 1, :], (n_sb, TQ))
        before = jnp.where(jrow > jp, 1.0, 0.0)
        rank = rank + jnp.where(other > score, 1.0, jnp.where(other == score, before, 0.0))
    chosen = jnp.where(valid, jnp.where(rank < float(min(N_SELECT, n_sb)), 0.0, NEG_INF), NEG_INF)

    srow = lax.broadcasted_iota(jnp.int32, (LANES - n_sb, TQ), 0) + n_sb
    side_t = jnp.concatenate([chosen, jnp.where(srow == SIDE_OFF, NEG_INF, 0.0)], axis=0)
    side = side_t.T
    for r in range(r_heads):
        far = jnp.full((TQ, LANES), rb_ref[N_BUCKETS - 1, r_heads * g + r] * LOG2E, F32)
        far_hi = far.astype(BF16).astype(F32)
        side_r = jnp.where(lane == SIDE_FAR_HI, far_hi, jnp.where(lane == SIDE_FAR_LO, far - far_hi, side))
        side_ref[r * TQ:(r + 1) * TQ, :] = side_r.astype(BF16)

    def key_side(kj, side_kind):
        return jnp.concatenate([ks_ref[tile_rows(kj), :], kside_ref[side_kind, tile_rows(kj), :]], axis=1)

    def q_side():
        return jnp.concatenate([qs_ref[...], side_ref[...]], axis=1)

    sn = scores(q_side(), [key_side(qi, KSIDE_NEAR), key_side(kj1, KSIDE_NEAR)], (TILE_DIAG, near_type))
    m0 = row_max(sn)
    m_ref[...] = jnp.broadcast_to(m0, (rows, LANES))
    acc_ref[...] = weighted([jnp.exp2(s - m0) for s in sn], vs_ref, (qi, kj1))

    def far_pair(idx):
        ka = qi - 2 - 2 * idx
        kb_raw = ka - 1
        kb = jnp.maximum(kb_raw, 0)
        sf = scores(q_side(), [key_side(ka, KSIDE_FAR),
                               key_side(kb, jnp.where(kb_raw >= 0, KSIDE_FAR, KSIDE_OFF))], (None, None))
        m_prev = m_ref[...]
        m_new = jnp.maximum(m_prev, row_max(sf))
        alpha = jnp.exp2(m_prev - m_new)
        m_wide = jnp.tile(m_new, (1, TK // LANES))
        acc_ref[...] = (jnp.tile(alpha, (1, 2)) * acc_ref[...]
                        + weighted([jnp.exp2(s - m_wide) for s in sf], vs_ref, (ka, kb)))
        m_ref[...] = m_new

    def placed(lo, hi):
        lo = jnp.where(par == 0, lo, pltpu.roll(lo, HEAD_DIM, 1))
        hi = jnp.where(par == 1, hi, pltpu.roll(hi, HEAD_DIM, 1))
        return jnp.where(lane < HEAD_DIM, lo, hi)

    for blk in range(2):
        part = [gates[r][0] * o_cmp[r] + gates[r][2] * o_win[r * TQ:(r + 1) * TQ, :] for r in (2 * blk, 2 * blk + 1)]
        out_ref[:, blk * LANES:(blk + 1) * LANES] = placed(*part)

    def finish():
        o_slc = normalised(acc_ref[...])
        for blk in range(2):
            part = [gates[r][1] * o_slc[r * TQ:(r + 1) * TQ, :] for r in (2 * blk, 2 * blk + 1)]
            out_ref[:, blk * LANES:(blk + 1) * LANES] += placed(*part)

    return far_pair, finish


def _nsa_attn(rel_bias, att, gate, kvc, bgate, tiles, cmpb, batch, seq_len):
    n = att.shape[0]
    nq = seq_len // TQ
    n_sb = seq_len // SEL_BLOCK
    ng = N_KV_GROUPS
    r = HEADS_PER_GROUP
    ov = _overlap_constants(seq_len)
    kside = _key_side_constants(seq_len)
    qw = r * HEAD_DIM
    kcol = ATTN_WIDTH + 2 * KV_WIDTH
    ns = ATTN_SLOTS if batch % ATTN_SLOTS == 0 else 1

    def slot_specs(s):
        seq = lambda bb: bb * ns + s
        kvb = lambda off: (lambda g, qi, bb: (seq(bb), (kcol + off * KV_WIDTH) // LANES + g // 2))
        return [
            pl.BlockSpec((TQ, qw), lambda g, qi, bb: (seq(bb) * nq + qi, g)),
            pl.BlockSpec((None, None, N_CMP_PAD, LANES), lambda g, qi, bb: (0, seq(bb) * ng + g, 0, 0)),
            pl.BlockSpec((None, None, N_CMP_PAD, LANES), lambda g, qi, bb: (1, seq(bb) * ng + g, 0, 0)),
            pl.BlockSpec((seq_len, LANES), kvb(0)),
            pl.BlockSpec((seq_len, LANES), kvb(1)),
            pl.BlockSpec((seq_len, LANES), kvb(2)),
            pl.BlockSpec((seq_len, LANES), kvb(3)),
            pl.BlockSpec((TQ, GATE_PAD), lambda g, qi, bb: (seq(bb) * nq + qi, 0)),
        ]

    shared_specs = [
        pl.BlockSpec((1, GATE_PAD), lambda g, qi, bb: (0, 0)),
        pl.BlockSpec((None, N_TILE_TYPES, r, TQ, TK), lambda g, qi, bb: (g, 0, 0, 0, 0)),
        pl.BlockSpec((None, r, TQ, N_CMP_PAD), lambda g, qi, bb: (g, 0, qi, 0)),
        pl.BlockSpec((n_sb, N_CMP_PAD), lambda g, qi, bb: (0, 0)),
        pl.BlockSpec((3, seq_len, LANES), lambda g, qi, bb: (0, 0, 0)),
    ]
    slot_scratch = [
        pltpu.VMEM((r * TQ, LANES), BF16),
        pltpu.VMEM((r * TQ, LANES), BF16),
        pltpu.VMEM((r * TQ, LANES), F32),
        pltpu.VMEM((r * TQ, 2 * LANES), F32),
    ]
    outs = pl.pallas_call(
        functools.partial(_nsa_kernel, n_sb, ns),
        grid=(ng, nq, batch // ns),
        in_specs=([pl.BlockSpec(memory_space=pltpu.SMEM)]
                  + [spec for s in range(ns) for spec in slot_specs(s)] + shared_specs),
        out_specs=pl.BlockSpec((None, ns, TQ, qw), lambda g, qi, bb: (bb, 0, qi, g)),
        out_shape=jax.ShapeDtypeStruct((batch // ns, ns, seq_len, ATTN_WIDTH), F32),
        scratch_shapes=slot_scratch * ns,
        compiler_params=pltpu.CompilerParams(dimension_semantics=("arbitrary", "arbitrary", "arbitrary"),
                                             vmem_limit_bytes=VMEM_LIMIT),
        name="nsa_attn",
    )(rel_bias, *([att, kvc, kvc, att, att, att, att, gate] * ns), bgate, tiles, cmpb,
      jnp.asarray(ov, BF16), jnp.asarray(kside, BF16))
    return outs.reshape(n, ATTN_WIDTH)


def kernel(x, mix_norm_g, w_in, b_gate, cmp_pe_k, cmp_pe_v, cmp_k_w1, cmp_k_w2, cmp_v_w1, cmp_v_w2, rel_bias, rnn_conv_w, rnn_conv_b, rg_a_w, rg_a_b, rg_x_w, rg_x_b, rg_lambda, attn_out_g, rnn_out_g, w_out, ffn_norm_g, w_ffn_gate, w_ffn_up, ffn_conv_w, ffn_conv_b, w_ffn_down, final_norm_g):
    batch, seq_len, d_model = x.shape
    assert d_model == D_MODEL and w_in.shape[0] == 1, "single-layer model of the stated width"
    assert seq_len // CMP_STRIDE == N_CMP_PAD and seq_len % TM_FFN == 0
    n = batch * seq_len
    ng = N_KV_GROUPS
    x2 = x.reshape(n, D_MODEL)

    w0 = w_in[0].astype(BF16)
    gate0 = ATT_COLS
    rx0 = gate0 + N_BRANCH * N_Q_HEADS
    w_rnn = jnp.concatenate(
        [w0[:, rx0:rx0 + 2 * RNN_WIDTH], w0[:, gate0:rx0],
         jnp.zeros((D_MODEL, GATE_PAD - N_BRANCH * N_Q_HEADS), BF16)], axis=1)
    att, gate, chunks, nr = _in_proj(
        x2, mix_norm_g[0][None], w0, w_rnn, rnn_conv_w[0], rnn_conv_b[0][None],
        _block_diag(rg_a_w[0]).astype(BF16), rg_a_b[0][None], _block_diag(rg_x_w[0]).astype(BF16), rg_x_b[0][None],
        rg_lambda[0][None], rnn_out_g[0][None], seq_len)

    pe = jnp.stack([_pair_pe(cmp_pe_k[0]), _pair_pe(cmp_pe_v[0])]).astype(BF16)
    w1p = jnp.stack([_pair_weights(cmp_k_w1[0]), _pair_weights(cmp_v_w1[0])]).astype(BF16)
    w2 = jnp.stack([jnp.tile(cmp_k_w2[0], (1, 2)), jnp.tile(cmp_v_w2[0], (1, 2))]).astype(BF16)
    kvc = _cmp_tokens(chunks, pe, w1p, w2, batch).reshape(2, batch * ng, N_CMP_PAD, LANES)

    tiles, cmpb = _bias_tiles(rel_bias, seq_len)
    bgate = jnp.pad(b_gate[0], (0, GATE_PAD - b_gate.shape[1]))[None]
    o_attn = _nsa_attn(rel_bias, att, gate, kvc, bgate, tiles, cmpb, batch, seq_len)

    h, y = _out_proj(o_attn, nr, x2, attn_out_g[0][None], ffn_norm_g[0][None], w_out[0].astype(BF16))
    out = _conv_ffn(y, h, w_ffn_gate[0].astype(BF16), w_ffn_up[0].astype(BF16), w_ffn_down[0].astype(BF16),
                    ffn_conv_w[0], ffn_conv_b[0][None], final_norm_g[None], seq_len)
    return out.reshape(batch, seq_len, D_MODEL)
```

```python
import functools
import math

import jax
import jax.numpy as jnp
import numpy as np
from jax import lax
from jax.experimental import pallas as pl
from jax.experimental.pallas import tpu as pltpu

F32 = jnp.float32
BF16 = jnp.bfloat16

D_MODEL = 2048
N_Q_HEADS = 16
N_KV_GROUPS = 4
HEADS_PER_GROUP = 4
HEAD_DIM = 64
ATTN_WIDTH = 1024
KV_WIDTH = 256
SCALE = HEAD_DIM ** -0.5
CMP_BLOCK = 32
CMP_STRIDE = 16
CMP_HIDDEN = 256
SEL_BLOCK = 64
N_SELECT = 16
WINDOW = 512
N_BRANCH = 3
RNN_WIDTH = 1024
RNN_BLOCKS = 16
RNN_CONV_WIDTH = 4
RG_LRU_C = 8.0
N_BUCKETS = 32
MAX_DISTANCE = 128
D_FF = 5632
NORM_EPS = 1e-6
NEG_INF = -1e30
FORCE_BONUS = 1e3
LOG2E = 1.0 / math.log(2.0)
GELU_C0 = math.sqrt(2.0 / math.pi)
GELU_C1 = GELU_C0 * 0.044715

LANES = 128
SUBLANES = 8
VMEM_LIMIT = 56 * 1024 * 1024

TM_PROJ = 512
COL_CHUNK = 512
TM_FFN = 1024
TF_FFN = 512
FFN_ROW_BLOCKS = 2
H_PIECE = 256
TS_RNN = 512
CB_RNN = 256
RNN_ROWS = 256
PROJ_CHUNK = 256
TQ = 256
TK = 256
ATTN_SLOTS = 2
ROW_SPLIT = 512
N_CMP_PAD = 128
GATE_PAD = 128

ATT_COLS = ATTN_WIDTH + 6 * KV_WIDTH
RNN_COLS = 2 * RNN_WIDTH + GATE_PAD
CMP_COL0 = ATTN_WIDTH
N_CHUNK_SLABS = 2 * KV_WIDTH // LANES
CHUNK2_W = CMP_STRIDE * LANES


def _rms(x, g):
    return x * lax.rsqrt(jnp.mean(x * x, axis=-1, keepdims=True) + NORM_EPS) * g


def _gelu(x):
    return jax.nn.gelu(x, approximate=True)


def _dot(a, b):
    return jnp.dot(a, b, preferred_element_type=F32)


def _dot_nt(a, b):
    return lax.dot_general(a, b, (((1,), (1,)), ((), ())), preferred_element_type=F32)


def _dot_tn(a, b):
    return lax.dot_general(a, b, (((0,), (0,)), ((), ())), preferred_element_type=F32)


def _out_proj_kernel(oa_ref, nr_ref, x_ref, ga_ref, gf_ref, w_ref, h_ref, y_ref):
    na = _rms(oa_ref[...].astype(F32), ga_ref[...]).astype(BF16)
    nr = nr_ref[...]
    for c0 in range(0, D_MODEL, COL_CHUNK):
        sl = slice(c0, c0 + COL_CHUNK)
        h_ref[:, sl] = (x_ref[:, sl] + _dot(na, w_ref[0:ATTN_WIDTH, sl])
                        + _dot(nr, w_ref[ATTN_WIDTH:D_MODEL, sl]))
    y_ref[...] = _rms(h_ref[...], gf_ref[...]).astype(BF16)


def _out_proj(o_attn, nr, x2, ga, gf, w):
    n = x2.shape[0]
    return pl.pallas_call(
        _out_proj_kernel,
        grid=(n // TM_PROJ,),
        in_specs=[
            pl.BlockSpec((TM_PROJ, ATTN_WIDTH), lambda i: (i, 0)),
            pl.BlockSpec((TM_PROJ, RNN_WIDTH), lambda i: (i, 0)),
            pl.BlockSpec((TM_PROJ, D_MODEL), lambda i: (i, 0)),
            pl.BlockSpec((1, ATTN_WIDTH), lambda i: (0, 0)),
            pl.BlockSpec((1, D_MODEL), lambda i: (0, 0)),
            pl.BlockSpec((D_MODEL, D_MODEL), lambda i: (0, 0), pipeline_mode=pl.Buffered(1)),
        ],
        out_specs=[
            pl.BlockSpec((TM_PROJ, D_MODEL), lambda i: (i, 0)),
            pl.BlockSpec((TM_PROJ, D_MODEL), lambda i: (i, 0)),
        ],
        out_shape=[
            jax.ShapeDtypeStruct((n, D_MODEL), F32),
            jax.ShapeDtypeStruct((n, D_MODEL), BF16),
        ],
        compiler_params=pltpu.CompilerParams(dimension_semantics=("parallel",), vmem_limit_bytes=VMEM_LIMIT),
        name="out_proj",
    )(o_attn, nr, x2, ga, gf, w)


def _ffn_kernel(tiles_per_seq, y_ref, h_ref, wg_ref, wu_ref, wd_ref, cw_ref, cb_ref, gf_ref, out_ref, carry_ref):
    i = pl.program_id(0)
    j = pl.program_id(1)
    nj = pl.num_programs(1)

    @pl.when((i % tiles_per_seq) == 0)
    def _():
        carry_ref[j] = jnp.zeros((SUBLANES, TF_FFN), F32)

    @pl.when(j == 0)
    def _():
        out_ref[...] = jnp.zeros(out_ref.shape, F32)

    @pl.when(j < D_MODEL // H_PIECE)
    def _():
        cols = pl.ds(pl.multiple_of(j * H_PIECE, H_PIECE), H_PIECE)
        out_ref[:, cols] += h_ref[...]

    row = lax.broadcasted_iota(jnp.int32, (SUBLANES, TF_FFN), 0)
    cw = cw_ref[...]
    cb = cb_ref[...]
    prev = carry_ref[j]
    rows_blk = TM_FFN // FFN_ROW_BLOCKS
    for blk in range(FFN_ROW_BLOCKS):
        sl = slice(blk * rows_blk, (blk + 1) * rows_blk)
        y = y_ref[sl, :]
        gt = _dot(y, wg_ref[...])
        p1 = prev[SUBLANES - 1:SUBLANES, :]
        p2 = prev[SUBLANES - 2:SUBLANES - 1, :]
        r1 = pltpu.roll(gt, 1, 0)
        r2 = pltpu.roll(gt, 2, 0)
        h1 = jnp.where(row == 0, p1, r1[0:SUBLANES])
        h2 = jnp.where(row == 0, p2, jnp.where(row == 1, p1, r2[0:SUBLANES]))
        s1 = jnp.concatenate([h1, r1[SUBLANES:]], axis=0)
        s2 = jnp.concatenate([h2, r2[SUBLANES:]], axis=0)
        conv = cw[2:3, :] * gt + cw[1:2, :] * s1 + cw[0:1, :] * s2 + cb
        inner = conv * (GELU_C0 + GELU_C1 * (conv * conv))
        hx = 0.5 * conv
        u = hx + hx * jnp.tanh(inner)
        z = (u * _dot(y, wu_ref[...])).astype(BF16)
        out_ref[sl, :] += _dot(z, wd_ref[...])
        prev = gt[rows_blk - SUBLANES:rows_blk, :]
    carry_ref[j] = prev

    @pl.when(j == nj - 1)
    def _():
        out_ref[...] = _rms(out_ref[...], gf_ref[...])


def _conv_ffn(y, h, wg, wu, wd, cw, cb, gf, seq_len):
    n = y.shape[0]
    nj = D_FF // TF_FFN
    return pl.pallas_call(
        functools.partial(_ffn_kernel, seq_len // TM_FFN),
        grid=(n // TM_FFN, nj),
        in_specs=[
            pl.BlockSpec((TM_FFN, D_MODEL), lambda i, j: (i, 0)),
            pl.BlockSpec((TM_FFN, H_PIECE), lambda i, j: (i, jnp.minimum(j, D_MODEL // H_PIECE - 1))),
            pl.BlockSpec((D_MODEL, TF_FFN), lambda i, j: (0, j)),
            pl.BlockSpec((D_MODEL, TF_FFN), lambda i, j: (0, j)),
            pl.BlockSpec((TF_FFN, D_MODEL), lambda i, j: (j, 0)),
            pl.BlockSpec((3, TF_FFN), lambda i, j: (0, j)),
            pl.BlockSpec((1, TF_FFN), lambda i, j: (0, j)),
            pl.BlockSpec((1, D_MODEL), lambda i, j: (0, 0)),
        ],
        out_specs=pl.BlockSpec((TM_FFN, D_MODEL), lambda i, j: (i, 0)),
        out_shape=jax.ShapeDtypeStruct((n, D_MODEL), F32),
        scratch_shapes=[pltpu.VMEM((nj, SUBLANES, TF_FFN), F32)],
        compiler_params=pltpu.CompilerParams(dimension_semantics=("arbitrary", "arbitrary"),
                                             vmem_limit_bytes=VMEM_LIMIT),
        name="conv_ffn",
    )(y, h, wg, wu, wd, cw, cb, gf)


def _rglru_pieces(rx_ref, ry_ref, cw_ref, cb_ref, wa_ref, ba_ref, wx_ref, bx_ref, lam_ref, g_ref,
                  out_ref, tail_ref, hc_ref, a_ref, u_ref, xbuf_ref, ygate_ref):
    rmod = lax.broadcasted_iota(jnp.int32, (RNN_ROWS // SUBLANES, SUBLANES, CB_RNN), 1)
    lam = lam_ref[...]
    nlam = -lam
    softplus = jnp.maximum(nlam, 0.0) + jnp.log1p(jnp.exp(-jnp.abs(nlam)))
    log_a_unit = -RG_LRU_C * softplus

    def stage():
        xbuf_ref[0:SUBLANES, :] = tail_ref[...]
        xbuf_ref[SUBLANES:SUBLANES + TS_RNN, :] = rx_ref[...]
        tail_ref[...] = rx_ref[TS_RNN - SUBLANES:TS_RNN, :]

    def stage_gate(lo, hi):
        ygate_ref[lo:hi, :] = _gelu(ry_ref[lo:hi, :])

    def channel_block(cb, lo):
        sl = slice(cb * CB_RNN, (cb + 1) * CB_RNN)
        hi = lo + RNN_ROWS
        cw = cw_ref[:, sl]
        xr = cb_ref[:, sl]
        for d in (0, 1, 2, 3):
            xr = xr + cw[3 - d:4 - d, :] * xbuf_ref[SUBLANES - d + lo:SUBLANES - d + hi, sl]
        xb = xr.astype(BF16)
        r = jax.nn.sigmoid(_dot(xb, wa_ref[cb]) + ba_ref[:, sl])
        gi = jax.nn.sigmoid(_dot(xb, wx_ref[cb]) + bx_ref[:, sl])
        log_a = log_a_unit[:, sl] * r
        a = jnp.exp(log_a)
        th = jnp.tanh(log_a)
        w = -2.0 * th / (1.0 - th)
        u = jnp.where(w > 0.0, w * lax.rsqrt(w), 0.0) * (gi * xr)
        a = a.reshape(RNN_ROWS // SUBLANES, SUBLANES, CB_RNN)
        u = u.reshape(RNN_ROWS // SUBLANES, SUBLANES, CB_RNN)
        for d in (1, 2, 4):
            keep = rmod >= d
            a_s = pltpu.roll(a, d, 1)
            u_s = pltpu.roll(u, d, 1)
            u = jnp.where(keep, a * u_s + u, u)
            a = jnp.where(keep, a * a_s, a)
        a_ref[lo:hi, sl] = a.reshape(RNN_ROWS, CB_RNN)
        u_ref[lo:hi, sl] = u.reshape(RNN_ROWS, CB_RNN)

    def carry(lo, hi):
        hprev = hc_ref[...]
        for r0 in range(lo, hi, SUBLANES):
            h8 = u_ref[r0:r0 + SUBLANES, :] + a_ref[r0:r0 + SUBLANES, :] * hprev
            u_ref[r0:r0 + SUBLANES, :] = h8
            hprev = jnp.broadcast_to(h8[SUBLANES - 1:SUBLANES, :], (SUBLANES, RNN_WIDTH))
        hc_ref[...] = hprev

    def finish(lo, hi):
        out_ref[lo:hi, :] = _rms(u_ref[lo:hi, :] * ygate_ref[lo:hi, :], g_ref[...]).astype(BF16)

    quarter, half = TS_RNN // 4, TS_RNN // 2
    readers = [stage] + [functools.partial(stage_gate, k * half, (k + 1) * half) for k in range(2)]
    others = ([functools.partial(channel_block, cb, lo)
               for cb in range(RNN_WIDTH // CB_RNN) for lo in range(0, TS_RNN, RNN_ROWS)]
              + [functools.partial(carry, k * quarter, (k + 1) * quarter) for k in range(4)]
              + [functools.partial(finish, k * half, (k + 1) * half) for k in range(2)])
    return readers, others


def _in_proj_kernel(tiles_per_seq, x_ref, g_ref, w_ref, wr_ref,
                    cw_ref, cb_ref, wa_ref, ba_ref, wx_ref, bx_ref, lam_ref, gr_ref,
                    att_ref, gate_ref, chunk_ref, nr_ref,
                    stage_ref, buf_ref, tail_ref, hc_ref, a_ref, u_ref, xbuf_ref, ygate_ref):
    i = pl.program_id(0)

    @pl.when(i == 0)
    def _():
        buf_ref[...] = jnp.zeros(buf_ref.shape, F32)

    @pl.when(jnp.maximum(i - 1, 0) % tiles_per_seq == 0)
    def _():
        tail_ref[...] = jnp.zeros(tail_ref.shape, F32)
        hc_ref[...] = jnp.zeros(hc_ref.shape, F32)

    buffer_readers, other_pieces = _rglru_pieces(
        buf_ref.at[:, 0:RNN_WIDTH], buf_ref.at[:, RNN_WIDTH:2 * RNN_WIDTH],
        cw_ref, cb_ref, wa_ref, ba_ref, wx_ref, bx_ref, lam_ref, gr_ref,
        nr_ref, tail_ref, hc_ref, a_ref, u_ref, xbuf_ref, ygate_ref)
    xn = _rms(x_ref[...], g_ref[...]).astype(BF16)

    def att_chunk(c0):
        res = _dot(xn, w_ref[:, c0:c0 + PROJ_CHUNK])
        att_ref[:, c0:c0 + PROJ_CHUNK] = res.astype(BF16)
        if CMP_COL0 <= c0 < CMP_COL0 + N_CHUNK_SLABS * LANES:
            q0 = (c0 - CMP_COL0) // LANES
            slabs = range(q0, q0 + PROJ_CHUNK // LANES)
            for q in slabs:
                stage_ref[q] = res[:, (q - q0) * LANES:(q - q0 + 1) * LANES]
            for l in range(CMP_STRIDE):
                for q in slabs:
                    tok = stage_ref[q, pl.ds(l, TM_PROJ // CMP_STRIDE, stride=CMP_STRIDE), :]
                    chunk_ref[q, :, l * LANES:(l + 1) * LANES] = tok.astype(BF16)

    def rnn_chunk(c0):
        buf_ref[:, c0:c0 + PROJ_CHUNK] = _dot(xn, wr_ref[:, c0:c0 + PROJ_CHUNK])

    def gate_chunk():
        gate_ref[...] = _dot(xn, wr_ref[:, 2 * RNN_WIDTH:RNN_COLS])

    def interleave(matmuls, others):
        done = 0
        for k, piece in enumerate(matmuls):
            piece()
            upto = (k + 1) * len(others) // len(matmuls)
            for other in others[done:upto]:
                other()
            done = upto

    keep_buffer = [functools.partial(att_chunk, c0) for c0 in range(0, ATT_COLS, PROJ_CHUNK)] + [gate_chunk]
    refill = [functools.partial(rnn_chunk, c0) for c0 in range(0, 2 * RNN_WIDTH, PROJ_CHUNK)]
    pieces = buffer_readers + other_pieces
    assert len(keep_buffer) * len(pieces) // (len(keep_buffer) + len(refill)) >= len(buffer_readers)
    interleave(keep_buffer + refill, pieces)


def _in_proj(x2, g, w, w_rnn, cw, cb, wa, ba, wx, bx, lam, gr, seq_len):
    n = x2.shape[0]
    nt = n // TM_PROJ
    assert CMP_COL0 % PROJ_CHUNK == 0 and (N_CHUNK_SLABS * LANES) % PROJ_CHUNK == 0 and TS_RNN == TM_PROJ
    nb = RNN_WIDTH // CB_RNN
    cur = lambda i: jnp.minimum(i, nt - 1)
    vec = pl.BlockSpec((1, RNN_WIDTH), lambda i: (0, 0))
    wspec = pl.BlockSpec((nb, CB_RNN, CB_RNN), lambda i: (0, 0, 0))
    return pl.pallas_call(
        functools.partial(_in_proj_kernel, seq_len // TM_PROJ),
        grid=(nt + 1,),
        in_specs=[
            pl.BlockSpec((TM_PROJ, D_MODEL), lambda i: (cur(i), 0)),
            pl.BlockSpec((1, D_MODEL), lambda i: (0, 0)),
            pl.BlockSpec((D_MODEL, ATT_COLS), lambda i: (0, 0), pipeline_mode=pl.Buffered(1)),
            pl.BlockSpec((D_MODEL, RNN_COLS), lambda i: (0, 0), pipeline_mode=pl.Buffered(1)),
            pl.BlockSpec((RNN_CONV_WIDTH, RNN_WIDTH), lambda i: (0, 0)),
            vec, wspec, vec, wspec, vec, vec, vec,
        ],
        out_specs=[
            pl.BlockSpec((TM_PROJ, ATT_COLS), lambda i: (cur(i), 0)),
            pl.BlockSpec((TM_PROJ, GATE_PAD), lambda i: (cur(i), 0)),
            pl.BlockSpec((N_CHUNK_SLABS, TM_PROJ // CMP_STRIDE, CHUNK2_W), lambda i: (0, cur(i), 0)),
            pl.BlockSpec((TM_PROJ, RNN_WIDTH), lambda i: (jnp.maximum(i - 1, 0), 0)),
        ],
        out_shape=[
            jax.ShapeDtypeStruct((n, ATT_COLS), BF16),
            jax.ShapeDtypeStruct((n, GATE_PAD), F32),
            jax.ShapeDtypeStruct((N_CHUNK_SLABS, n // CMP_STRIDE, CHUNK2_W), BF16),
            jax.ShapeDtypeStruct((n, RNN_WIDTH), BF16),
        ],
        scratch_shapes=[
            pltpu.VMEM((N_CHUNK_SLABS, TM_PROJ, LANES), F32),
            pltpu.VMEM((TM_PROJ, 2 * RNN_WIDTH), F32),
            pltpu.VMEM((SUBLANES, RNN_WIDTH), F32),
            pltpu.VMEM((SUBLANES, RNN_WIDTH), F32),
            pltpu.VMEM((TS_RNN, RNN_WIDTH), F32),
            pltpu.VMEM((TS_RNN, RNN_WIDTH), F32),
            pltpu.VMEM((SUBLANES + TS_RNN, RNN_WIDTH), F32),
            pltpu.VMEM((TS_RNN, RNN_WIDTH), F32),
        ],
        compiler_params=pltpu.CompilerParams(dimension_semantics=("arbitrary",), vmem_limit_bytes=VMEM_LIMIT),
        name="in_proj",
    )(x2, g, w, w_rnn, cw, cb, wa, ba, wx, bx, lam, gr)


def _block_diag(w):
    per = CB_RNN // (RNN_WIDTH // RNN_BLOCKS)
    bd = w.shape[-1]
    w4 = w.reshape(RNN_BLOCKS // per, per, bd, bd)
    eye = jnp.eye(per, dtype=w.dtype)
    return jnp.einsum('cpij,pq->cpiqj', w4, eye).reshape(RNN_BLOCKS // per, per * bd, per * bd)


CMP_BATCHES = 8


def _cmp_tokens_kernel(x_ref, pe_ref, w1_ref, w2_ref, out_ref):
    x = x_ref[...]
    rows = x.shape[0]
    ya = _dot(x, w1_ref[0])
    yb = _dot(x, w1_ref[1])
    pterm = (_dot(jnp.broadcast_to(pe_ref[0], (SUBLANES, CHUNK2_W)), w1_ref[0])
             + _dot(jnp.broadcast_to(pe_ref[1], (SUBLANES, CHUNK2_W)), w1_ref[1]))[0:1, :]
    act = _gelu(ya + pltpu.roll(yb, rows - 1, 0) + pterm).astype(BF16)
    for e in range(2):
        tok = _dot(act[:, e * CMP_HIDDEN:(e + 1) * CMP_HIDDEN], w2_ref[...]).astype(BF16)
        out_ref[:, e] = tok.reshape(rows // N_CMP_PAD, N_CMP_PAD, LANES)


def _cmp_tokens(chunks, pe, w1p, w2, batch):
    nb = min(CMP_BATCHES, batch)
    tr = nb * N_CMP_PAD
    return pl.pallas_call(
        _cmp_tokens_kernel,
        grid=(N_CHUNK_SLABS, batch // nb),
        in_specs=[
            pl.BlockSpec((None, tr, CHUNK2_W), lambda s, i: (s, i, 0)),
            pl.BlockSpec((None, 2, 1, CHUNK2_W), lambda s, i: (s // 2, 0, 0, 0)),
            pl.BlockSpec((None, 2, CHUNK2_W, 2 * CMP_HIDDEN), lambda s, i: (s // 2, 0, 0, 0)),
            pl.BlockSpec((None, CMP_HIDDEN, LANES), lambda s, i: (s // 2, 0, 0)),
        ],
        out_specs=pl.BlockSpec((None, nb, None, 2, N_CMP_PAD, LANES), lambda s, i: (s // 2, i, s % 2, 0, 0, 0)),
        out_shape=jax.ShapeDtypeStruct((2, batch, 2, 2, N_CMP_PAD, LANES), BF16),
        compiler_params=pltpu.CompilerParams(dimension_semantics=("parallel", "parallel")),
        name="cmp_tokens",
    )(chunks, pe, w1p, w2)


def _pair_weights(w1):
    half = CMP_STRIDE * HEAD_DIM
    w = w1.reshape(2, CMP_STRIDE, HEAD_DIM, CMP_HIDDEN)
    eye = jnp.eye(2, dtype=w1.dtype)
    return jnp.einsum('hldj,ef->hledfj', w, eye).reshape(2, 2 * half, 2 * CMP_HIDDEN)


def _pair_pe(pe):
    p = pe.reshape(2, CMP_STRIDE, 1, HEAD_DIM)
    return jnp.broadcast_to(p, (2, CMP_STRIDE, 2, HEAD_DIM)).reshape(2, 1, CHUNK2_W)


N_TILE_TYPES = 4
TILE_DIAG, TILE_NEAR, TILE_WIN_LAST, TILE_NONE = 0, 1, 2, 3


def _t5_bucket_np(dist):
    n = np.maximum(dist, 0)
    max_exact = N_BUCKETS // 2
    nf = np.maximum(n, 1).astype(np.float64)
    large = max_exact + (np.log(nf / max_exact) / math.log(MAX_DISTANCE / max_exact)
                         * (N_BUCKETS - max_exact)).astype(np.int32)
    large = np.minimum(large, N_BUCKETS - 1)
    return np.where(n < max_exact, n, large).astype(np.int32)


def _bucket_constants(seq_len):
    i = np.arange(TQ)[:, None]
    j = np.arange(TK)[None, :]
    d0 = i - j
    d1 = TK + i - j
    d2 = 2 * TK + i - j
    valid = [d0 >= 0, np.ones_like(d1, bool), d2 < WINDOW, np.zeros_like(d0, bool)]
    tiles = np.stack([np.where(v, _t5_bucket_np(d), -1) for d, v in zip([d0, d1, d2, d0], valid)])
    t = np.arange(seq_len)[:, None]
    c = np.arange(N_CMP_PAD)[None, :]
    dc = t - (c * CMP_STRIDE + CMP_BLOCK - 1)
    cmp = np.where(dc >= 0, _t5_bucket_np(dc), -1)
    return tiles.astype(np.int32), cmp.astype(np.int32)


def _bias_kernel(tile_buckets, rb_ref, tb_ref, cbk_ref, tiles_ref, cmpb_ref):
    h = pl.program_id(0)

    def lookup(bkt, buckets):
        acc = jnp.full(bkt.shape, NEG_INF, F32)
        for b in buckets:
            acc = jnp.where(bkt == b, rb_ref[b, h] * LOG2E, acc)
        return acc

    for t in range(N_TILE_TYPES):
        tiles_ref[t] = lookup(tb_ref[t], tile_buckets[t])
    cmpb_ref[...] = lookup(cbk_ref[...], range(N_BUCKETS))


def _bias_tiles(rel_bias, seq_len):
    tb, cbk = _bucket_constants(seq_len)
    tile_buckets = tuple(tuple(int(b) for b in np.unique(t[t >= 0])) for t in tb)
    r = HEADS_PER_GROUP
    return pl.pallas_call(
        functools.partial(_bias_kernel, tile_buckets),
        grid=(N_Q_HEADS,),
        in_specs=[
            pl.BlockSpec(memory_space=pltpu.SMEM),
            pl.BlockSpec((N_TILE_TYPES, TQ, TK), lambda h: (0, 0, 0)),
            pl.BlockSpec((seq_len, N_CMP_PAD), lambda h: (0, 0)),
        ],
        out_specs=[
            pl.BlockSpec((None, N_TILE_TYPES, None, TQ, TK), lambda h: (h // r, 0, h % r, 0, 0)),
            pl.BlockSpec((None, None, seq_len, N_CMP_PAD), lambda h: (h // r, h % r, 0, 0)),
        ],
        out_shape=[
            jax.ShapeDtypeStruct((N_KV_GROUPS, N_TILE_TYPES, r, TQ, TK), F32),
            jax.ShapeDtypeStruct((N_KV_GROUPS, r, seq_len, N_CMP_PAD), F32),
        ],
        compiler_params=pltpu.CompilerParams(dimension_semantics=("parallel",)),
        name="bias_tiles",
    )(rel_bias, jnp.asarray(tb), jnp.asarray(cbk))


SIDE_FAR_HI, SIDE_FAR_LO, SIDE_OFF = 32, 33, 34
KSIDE_NEAR, KSIDE_FAR, KSIDE_OFF = 0, 1, 2


def _key_side_constants(seq_len):
    s = np.arange(seq_len)
    side = np.zeros((3, seq_len, LANES), np.float32)
    side[:, s, s // SEL_BLOCK] = 1.0
    side[KSIDE_FAR, :, SIDE_FAR_HI] = 1.0
    side[KSIDE_FAR, :, SIDE_FAR_LO] = 1.0
    side[KSIDE_OFF, :, SIDE_OFF] = 1.0
    return side


def _overlap_constants(seq_len):
    n_sb = seq_len // SEL_BLOCK
    c = np.arange(N_CMP_PAD)[None, :]
    j = np.arange(n_sb)[:, None]
    lo = np.maximum(c * CMP_STRIDE, j * SEL_BLOCK)
    hi = np.minimum(c * CMP_STRIDE + CMP_BLOCK, (j + 1) * SEL_BLOCK)
    ov = np.maximum(hi - lo, 0) / CMP_BLOCK
    ov[:, N_CMP_PAD - 1] = 0.0
    return ov.astype(np.float32)


def _nsa_kernel(n_sb, n_slots, rb_ref, *refs):
    per_slot_in = 8
    slot_in = [refs[s * per_slot_in:(s + 1) * per_slot_in] for s in range(n_slots)]
    shared = refs[n_slots * per_slot_in:n_slots * per_slot_in + 5]
    out_ref = refs[n_slots * per_slot_in + 5]
    scratch = refs[n_slots * per_slot_in + 6:]
    g = pl.program_id(0)
    qi = pl.program_id(1)
    slots = [_nsa_slot(n_sb, g, qi, rb_ref, *slot_in[s], *shared, out_ref.at[s], *scratch[5 * s:5 * s + 5])
             for s in range(n_slots)]

    def far_pairs(idx, carry):
        for far_pair, _ in slots:
            far_pair(idx)
        return carry

    lax.fori_loop(0, jnp.maximum(qi, 1) // 2, far_pairs, 0)
    for _, finish in slots:
        finish()


def _nsa_slot(n_sb, g, qi, rb_ref, q_ref, kc_ref, vc_ref, ks_ref, vs_ref, kw_ref, vw_ref, gate_ref, bg_ref,
              tiles_ref, cmpb_ref, ov_ref, kside_ref, out_ref,
              qs_ref, side_ref, m_ref, acc_ref, park_ref):
    par = g % 2
    r_heads = HEADS_PER_GROUP
    rows = r_heads * TQ
    lane = lax.broadcasted_iota(jnp.int32, (TQ, LANES), 1)
    in_half = (lane // HEAD_DIM) == par

    for blk in range(2):
        qb = q_ref[:, blk * LANES:(blk + 1) * LANES].astype(F32)
        qr = pltpu.roll(qb, HEAD_DIM, 1)
        for hh in range(2):
            r = blk * 2 + hh
            src = jnp.where(par == hh, qb, qr)
            qs_ref[r * TQ:(r + 1) * TQ, :] = (jnp.where(in_half, src, 0.0) * (SCALE * LOG2E)).astype(BF16)

    gs = jax.nn.sigmoid(gate_ref[...] + bg_ref[...])
    gates = [[jnp.sum(jnp.where(lane == N_BRANCH * (r_heads * g + r) + br, gs, 0.0), axis=-1, keepdims=True)
              for br in range(N_BRANCH)] for r in range(r_heads)]

    def tile_rows(kj):
        return pl.ds(pl.multiple_of(kj * TK, TK), TK)

    ones = jnp.ones((TK, LANES), BF16)

    def scores(q, key_tiles, tile_types):
        keys = jnp.concatenate(key_tiles, axis=0)
        s = jnp.concatenate([_dot_nt(q[c:c + ROW_SPLIT], keys) for c in range(0, rows, ROW_SPLIT)], axis=0)
        out = []
        for t, kind in enumerate(tile_types):
            st = s[:, t * TK:(t + 1) * TK]
            if kind is not None:
                st = (st.reshape(r_heads, TQ, TK) + tiles_ref[kind]).reshape(rows, TK)
            out.append(st)
        return out

    def weighted(ps, v_ref, kjs):
        p = jnp.concatenate([x.astype(BF16) for x in ps], axis=1)
        v = jnp.concatenate([jnp.concatenate([v_ref[tile_rows(kj), :], ones], axis=1) for kj in kjs], axis=0)
        return jnp.concatenate([_dot(p[c:c + ROW_SPLIT], v) for c in range(0, rows, ROW_SPLIT)], axis=0)

    def row_max(ss):
        return jnp.max(functools.reduce(jnp.maximum, ss), axis=-1, keepdims=True)

    def normalised(acc):
        return acc[:, 0:LANES] * (1.0 / acc[:, LANES:2 * LANES])

    kj1 = jnp.maximum(qi - 1, 0)
    near_type = jnp.where(qi >= 1, TILE_NEAR, TILE_NONE)

    kj2 = jnp.maximum(qi - 2, 0)
    last_type = jnp.where(qi >= 2, TILE_WIN_LAST, TILE_NONE)
    sw = scores(qs_ref[...], [kw_ref[tile_rows(kj), :] for kj in (qi, kj1, kj2)],
                (TILE_DIAG, near_type, last_type))
    mw = row_max(sw)
    o_win = normalised(weighted([jnp.exp2(s - mw) for s in sw], vw_ref, (qi, kj1, kj2)))

    trow = qi * TQ + lax.broadcasted_iota(jnp.int32, (TQ, 1), 0)
    has_any = jnp.where(trow >= CMP_BLOCK - 1, 1.0, 0.0)
    sc = _dot_nt(qs_ref[...], kc_ref[...]).reshape(r_heads, TQ, N_CMP_PAD) + cmpb_ref[...]
    ec = jnp.exp2(sc - jnp.max(sc, axis=-1, keepdims=True))
    pc = ec * (has_any / jnp.sum(ec, axis=-1, keepdims=True))
    psum = jnp.sum(pc, axis=0)
    oc = _dot(pc.reshape(rows, N_CMP_PAD).astype(BF16), vc_ref[...])
    o_cmp = [oc[r * TQ:(r + 1) * TQ, :] for r in range(r_heads)]

    p_hi = psum.astype(BF16)
    p_lo = (psum - p_hi.astype(F32)).astype(BF16)
    ov = ov_ref[...]
    imp = _dot_nt(ov, p_hi) + _dot_nt(ov, p_lo)
    jrow = lax.broadcasted_iota(jnp.int32, (n_sb, TQ), 0)
    tq = qi * TQ + lax.broadcasted_iota(jnp.int32, (n_sb, TQ), 1)
    cur = tq // SEL_BLOCK
    valid = jrow <= cur
    forced = jnp.where(jrow == 0, 1.0, jnp.where(jrow == cur, 1.0, jnp.where(jrow == cur - 1, 1.0, 0.0)))
    score = jnp.where(valid, imp + FORCE_BONUS * forced, -1.0)
    rank = jnp.zeros((n_sb, TQ), F32)
    for jp in range(n_sb):
        other = jnp.broadcast_to(score[jp:jp + 1, :], (n_sb, TQ))
        before = jnp.where(jrow > jp, 1.0, 0.0)
        rank = rank + jnp.where(other > score, 1.0, jnp.where(other == score, before, 0.0))
    chosen = jnp.where(valid, jnp.where(rank < float(min(N_SELECT, n_sb)), 0.0, NEG_INF), NEG_INF)

    srow = lax.broadcasted_iota(jnp.int32, (LANES - n_sb, TQ), 0) + n_sb
    side_t = jnp.concatenate([chosen, jnp.where(srow == SIDE_OFF, NEG_INF, 0.0)], axis=0)
    side = side_t.T
    for r in range(r_heads):
        far = jnp.full((TQ, LANES), rb_ref[N_BUCKETS - 1, r_heads * g + r] * LOG2E, F32)
        far_hi = far.astype(BF16).astype(F32)
        side_r = jnp.where(lane == SIDE_FAR_HI, far_hi, jnp.where(lane == SIDE_FAR_LO, far - far_hi, side))
        side_ref[r * TQ:(r + 1) * TQ, :] = side_r.astype(BF16)

    def key_side(kj, side_kind):
        return jnp.concatenate([ks_ref[tile_rows(kj), :], kside_ref[side_kind, tile_rows(kj), :]], axis=1)

    def q_side():
        return jnp.concatenate([qs_ref[...], side_ref[...]], axis=1)

    sn = scores(q_side(), [key_side(qi, KSIDE_NEAR), key_side(kj1, KSIDE_NEAR)], (TILE_DIAG, near_type))
    m0 = row_max(sn)
    m_ref[...] = jnp.broadcast_to(m0, (rows, LANES))
    acc_ref[...] = weighted([jnp.exp2(s - m0) for s in sn], vs_ref, (qi, kj1))

    def far_pair(idx):
        ka = qi - 2 - 2 * idx
        kb_raw = ka - 1
        kb = jnp.maximum(kb_raw, 0)
        sf = scores(q_side(), [key_side(ka, KSIDE_FAR),
                               key_side(kb, jnp.where(kb_raw >= 0, KSIDE_FAR, KSIDE_OFF))], (None, None))
        m_prev = m_ref[...]
        m_new = jnp.maximum(m_prev, row_max(sf))
        alpha = jnp.exp2(m_prev - m_new)
        m_wide = jnp.tile(m_new, (1, TK // LANES))
        acc_ref[...] = (jnp.tile(alpha, (1, 2)) * acc_ref[...]
                        + weighted([jnp.exp2(s - m_wide) for s in sf], vs_ref, (ka, kb)))
        m_ref[...] = m_new

    def placed(lo, hi):
        lo = jnp.where(par == 0, lo, pltpu.roll(lo, HEAD_DIM, 1))
        hi = jnp.where(par == 1, hi, pltpu.roll(hi, HEAD_DIM, 1))
        return jnp.where(lane < HEAD_DIM, lo, hi)

    for blk in range(2):
        part = [gates[r][0] * o_cmp[r] + gates[r][2] * o_win[r * TQ:(r + 1) * TQ, :] for r in (2 * blk, 2 * blk + 1)]
        park_ref[:, blk * LANES:(blk + 1) * LANES] = placed(*part)

    def finish():
        o_slc = normalised(acc_ref[...])
        for blk in range(2):
            part = [gates[r][1] * o_slc[r * TQ:(r + 1) * TQ, :] for r in (2 * blk, 2 * blk + 1)]
            cols = slice(blk * LANES, (blk + 1) * LANES)
            out_ref[:, cols] = (park_ref[:, cols] + placed(*part)).astype(BF16)

    return far_pair, finish


def _nsa_attn(rel_bias, att, gate, kvc, bgate, tiles, cmpb, batch, seq_len):
    n = att.shape[0]
    nq = seq_len // TQ
    n_sb = seq_len // SEL_BLOCK
    ng = N_KV_GROUPS
    r = HEADS_PER_GROUP
    ov = _overlap_constants(seq_len)
    kside = _key_side_constants(seq_len)
    qw = r * HEAD_DIM
    kcol = ATTN_WIDTH + 2 * KV_WIDTH
    ns = ATTN_SLOTS if batch % ATTN_SLOTS == 0 else 1

    def slot_specs(s):
        seq = lambda bb: bb * ns + s
        kvb = lambda off: (lambda g, qi, bb: (seq(bb), (kcol + off * KV_WIDTH) // LANES + g // 2))
        return [
            pl.BlockSpec((TQ, qw), lambda g, qi, bb: (seq(bb) * nq + qi, g)),
            pl.BlockSpec((None, None, N_CMP_PAD, LANES), lambda g, qi, bb: (0, seq(bb) * ng + g, 0, 0)),
            pl.BlockSpec((None, None, N_CMP_PAD, LANES), lambda g, qi, bb: (1, seq(bb) * ng + g, 0, 0)),
            pl.BlockSpec((seq_len, LANES), kvb(0)),
            pl.BlockSpec((seq_len, LANES), kvb(1)),
            pl.BlockSpec((seq_len, LANES), kvb(2)),
            pl.BlockSpec((seq_len, LANES), kvb(3)),
            pl.BlockSpec((TQ, GATE_PAD), lambda g, qi, bb: (seq(bb) * nq + qi, 0)),
        ]

    shared_specs = [
        pl.BlockSpec((1, GATE_PAD), lambda g, qi, bb: (0, 0)),
        pl.BlockSpec((None, N_TILE_TYPES, r, TQ, TK), lambda g, qi, bb: (g, 0, 0, 0, 0)),
        pl.BlockSpec((None, r, TQ, N_CMP_PAD), lambda g, qi, bb: (g, 0, qi, 0)),
        pl.BlockSpec((n_sb, N_CMP_PAD), lambda g, qi, bb: (0, 0)),
        pl.BlockSpec((3, seq_len, LANES), lambda g, qi, bb: (0, 0, 0)),
    ]
    slot_scratch = [
        pltpu.VMEM((r * TQ, LANES), BF16),
        pltpu.VMEM((r * TQ, LANES), BF16),
        pltpu.VMEM((r * TQ, LANES), F32),
        pltpu.VMEM((r * TQ, 2 * LANES), F32),
        pltpu.VMEM((TQ, qw), F32),
    ]
    outs = pl.pallas_call(
        functools.partial(_nsa_kernel, n_sb, ns),
        grid=(ng, nq, batch // ns),
        in_specs=([pl.BlockSpec(memory_space=pltpu.SMEM)]
                  + [spec for s in range(ns) for spec in slot_specs(s)] + shared_specs),
        out_specs=pl.BlockSpec((None, ns, TQ, qw), lambda g, qi, bb: (bb, 0, qi, g)),
        out_shape=jax.ShapeDtypeStruct((batch // ns, ns, seq_len, ATTN_WIDTH), BF16),
        scratch_shapes=slot_scratch * ns,
        compiler_params=pltpu.CompilerParams(dimension_semantics=("arbitrary", "arbitrary", "arbitrary"),
                                             vmem_limit_bytes=VMEM_LIMIT),
        name="nsa_attn",
    )(rel_bias, *([att, kvc, kvc, att, att, att, att, gate] * ns), bgate, tiles, cmpb,
      jnp.asarray(ov, BF16), jnp.asarray(kside, BF16))
    return outs.reshape(n, ATTN_WIDTH)


def kernel(x, mix_norm_g, w_in, b_gate, cmp_pe_k, cmp_pe_v, cmp_k_w1, cmp_k_w2, cmp_v_w1, cmp_v_w2, rel_bias, rnn_conv_w, rnn_conv_b, rg_a_w, rg_a_b, rg_x_w, rg_x_b, rg_lambda, attn_out_g, rnn_out_g, w_out, ffn_norm_g, w_ffn_gate, w_ffn_up, ffn_conv_w, ffn_conv_b, w_ffn_down, final_norm_g):
    batch, seq_len, d_model = x.shape
    assert d_model == D_MODEL and w_in.shape[0] == 1, "single-layer model of the stated width"
    assert seq_len // CMP_STRIDE == N_CMP_PAD and seq_len % TM_FFN == 0
    n = batch * seq_len
    ng = N_KV_GROUPS
    x2 = x.reshape(n, D_MODEL)

    w0 = w_in[0].astype(BF16)
    gate0 = ATT_COLS
    rx0 = gate0 + N_BRANCH * N_Q_HEADS
    w_rnn = jnp.concatenate(
        [w0[:, rx0:rx0 + 2 * RNN_WIDTH], w0[:, gate0:rx0],
         jnp.zeros((D_MODEL, GATE_PAD - N_BRANCH * N_Q_HEADS), BF16)], axis=1)
    att, gate, chunks, nr = _in_proj(
        x2, mix_norm_g[0][None], w0, w_rnn, rnn_conv_w[0], rnn_conv_b[0][None],
        _block_diag(rg_a_w[0]).astype(BF16), rg_a_b[0][None], _block_diag(rg_x_w[0]).astype(BF16), rg_x_b[0][None],
        rg_lambda[0][None], rnn_out_g[0][None], seq_len)

    pe = jnp.stack([_pair_pe(cmp_pe_k[0]), _pair_pe(cmp_pe_v[0])]).astype(BF16)
    w1p = jnp.stack([_pair_weights(cmp_k_w1[0]), _pair_weights(cmp_v_w1[0])]).astype(BF16)
    w2 = jnp.stack([jnp.tile(cmp_k_w2[0], (1, 2)), jnp.tile(cmp_v_w2[0], (1, 2))]).astype(BF16)
    kvc = _cmp_tokens(chunks, pe, w1p, w2, batch).reshape(2, batch * ng, N_CMP_PAD, LANES)

    tiles, cmpb = _bias_tiles(rel_bias, seq_len)
    bgate = jnp.pad(b_gate[0], (0, GATE_PAD - b_gate.shape[1]))[None]
    o_attn = _nsa_attn(rel_bias, att, gate, kvc, bgate, tiles, cmpb, batch, seq_len)

    h, y = _out_proj(o_attn, nr, x2, attn_out_g[0][None], ffn_norm_g[0][None], w_out[0].astype(BF16))
    out = _conv_ffn(y, h, w_ffn_gate[0].astype(BF16), w_ffn_up[0].astype(BF16), w_ffn_down[0].astype(BF16),
                    ffn_conv_w[0], ffn_conv_b[0][None], final_norm_g[None], seq_len)
    return out.reshape(batch, seq_len, D_MODEL)
```
